```python
import math
import jax
import jax.numpy as jnp
from jax import lax
import numpy as np

D_MODEL = 1024
BATCH = 8
SEQ = 2048
DEPTH = 2
DEC_BATCH = 128
DEC_SEQ = 1
PAST_LEN = 16384
PAGE_SIZE = 128

N_AB = (DEPTH + 1) // 2
N_RET = DEPTH // 2
GDN_HEADS = 8
GDN_DK = 128
GDN_DV = 128
SSM_HEADS = 16
SSM_P = 64
SSM_N = 128
SSM_G = 2
CONV_W = 4
RET_HEADS = 4
RET_DK = 256
RET_DV = 512
ROPE_BASE = 10000.0
PEER_HEADS = 8
PEER_NKEYS = 128
PEER_EXPERTS = PEER_NKEYS * PEER_NKEYS
PEER_TOPK = 16
PEER_DQ = 256
PEER_BLOCK = 256
CHUNK = 64
EPS = 1e-6

GDN_QK = GDN_HEADS * GDN_DK
GDN_V = GDN_HEADS * GDN_DV
SSM_INNER = SSM_HEADS * SSM_P
SSM_BC = SSM_G * SSM_N
CONV_CH = 2 * GDN_QK + GDN_V + SSM_INNER + 2 * SSM_BC
AB_SPLIT = [CONV_CH, GDN_V, SSM_INNER, GDN_HEADS, GDN_HEADS, SSM_HEADS]
AB_IN = sum(AB_SPLIT)
CONV_SPLIT = [GDN_QK, GDN_QK, GDN_V, SSM_INNER, SSM_BC, SSM_BC]
AB_OUT = GDN_V + SSM_INNER
RET_QK = RET_HEADS * RET_DK
RET_V = RET_HEADS * RET_DV
RET_SPLIT = [RET_QK, RET_QK, RET_V, RET_V]
RET_IN = sum(RET_SPLIT)

kernel_name = 'hybrid_gdn_ssd_retention_peer_step'


def _cuts(sizes):
    return np.cumsum(sizes)[:-1].tolist()


def rmsnorm(x, g):
    xf = x.astype(jnp.float32)
    y = xf * lax.rsqrt(jnp.mean(xf * xf, axis=-1, keepdims=True) + EPS)
    return (y * g.astype(jnp.float32)).astype(x.dtype)


def l2norm(x):
    return x * lax.rsqrt(jnp.sum(x * x, axis=-1, keepdims=True) + EPS)


def causal_conv_silu(u, buf, w, b):
    L = u.shape[1]
    up = jnp.concatenate([buf.astype(u.dtype), u], axis=1)
    y = b + sum(up[:, i:i + L] * w[i] for i in range(CONV_W))
    return jax.nn.silu(y), up[:, L:]


def rope(x, pos0):
    L, d = x.shape[1], x.shape[-1]
    inv = ROPE_BASE ** (-jnp.arange(0, d, 2, dtype=jnp.float32) / d)
    ang = (pos0 + jnp.arange(L, dtype=jnp.float32))[:, None] * inv[None, :]
    cos, sin = jnp.cos(ang)[None, :, None, :], jnp.sin(ang)[None, :, None, :]
    x1, x2 = x[..., :d // 2], x[..., d // 2:]
    return jnp.concatenate([x1 * cos - x2 * sin, x1 * sin + x2 * cos], axis=-1)


def _chunking(L):
    c = min(CHUNK, L)
    n = -(-L // c)
    return c, n, n * c - L


def _to_chunks(a, c, n, pad):
    a = jnp.pad(a, [(0, 0), (0, pad)] + [(0, 0)] * (a.ndim - 2))
    a = a.reshape(a.shape[0], n, c, *a.shape[2:])
    return jnp.moveaxis(a, 1, 0)


def _from_chunks(a, L):
    a = jnp.moveaxis(a, 0, 1)
    a = a.reshape(a.shape[0], -1, *a.shape[3:])
    return a[:, :L]


def decay_linear_attn(q, k, v, log_a, s0):
    f32 = jnp.float32
    L = q.shape[1]
    c, n, pad = _chunking(L)
    qc, kc, vc, gc = (_to_chunks(t.astype(f32), c, n, pad) for t in (q, k, v, log_a))
    mask = jnp.tril(jnp.ones((c, c), dtype=bool))

    def step(s, inp):
        qi, ki, vi, gi = inp
        cum = jnp.cumsum(gi, axis=1)
        diff = cum[:, :, None, :] - cum[:, None, :, :]
        dec = jnp.exp(jnp.where(mask[None, :, :, None], diff, -jnp.inf))
        scores = jnp.einsum('bihd,bjhd->bijh', qi, ki) * dec
        o = (jnp.einsum('bijh,bjhe->bihe', scores, vi)
             + jnp.einsum('bihd,bhde->bihe', qi * jnp.exp(cum)[..., None], s))
        last = cum[:, -1]
        kd = ki * jnp.exp(last[:, None, :] - cum)[..., None]
        s = s * jnp.exp(last)[:, :, None, None] + jnp.einsum('bjhd,bjhe->bhde', kd, vi)
        return s, o

    s, o = lax.scan(step, s0.astype(f32), (qc, kc, vc, gc))
    return _from_chunks(o, L), s


def gated_delta_rule(q, k, v, beta, g, s0):
    f32 = jnp.float32
    L, dv = q.shape[1], v.shape[-1]
    c, n, pad = _chunking(L)
    heads_first = lambda t: jnp.moveaxis(_to_chunks(t.astype(f32), c, n, pad), 3, 2)
    qh, kh, vh, bh, gh = (heads_first(t) for t in (q, k, v, beta, g))
    gh = jnp.cumsum(gh, axis=-1)
    incl = jnp.tril(jnp.ones((c, c), dtype=bool))
    strict = jnp.tril(jnp.ones((c, c), dtype=bool), -1)
    diff = gh[..., :, None] - gh[..., None, :]
    dec_strict = jnp.exp(jnp.where(strict, diff, -jnp.inf))
    dec_incl = jnp.exp(jnp.where(incl, diff, -jnp.inf))
    lower = (bh[..., :, None] * jnp.einsum('nbhid,nbhjd->nbhij', kh, kh) * dec_strict
             + jnp.eye(c, dtype=f32))
    rhs = jnp.concatenate([vh * bh[..., None], kh * (bh * jnp.exp(gh))[..., None]], axis=-1)
    sol = lax.linalg.triangular_solve(lower, rhs, left_side=True, lower=True, unit_diagonal=True)
    u0, w = sol[..., :dv], sol[..., dv:]
    qk = jnp.einsum('nbhid,nbhjd->nbhij', qh, kh) * dec_incl

    def step(s, inp):
        qi, ki, u0i, wi, qki, gi = inp
        u = u0i - jnp.einsum('bhid,bhde->bhie', wi, s)
        o = (jnp.einsum('bhid,bhde->bhie', qi * jnp.exp(gi)[..., None], s)
             + jnp.einsum('bhij,bhje->bhie', qki, u))
        last = gi[..., -1]
        kd = ki * jnp.exp(last[..., None] - gi)[..., None]
        s = s * jnp.exp(last)[..., None, None] + jnp.einsum('bhjd,bhje->bhde', kd, u)
        return s, o

    s, o = lax.scan(step, s0.astype(f32), (qh, kh, u0, w, qk, gh))
    return _from_chunks(jnp.moveaxis(o, 2, 3), L), s


def mixer_ab(h, conv_buf, gdn_s, ssm_s, w_in, conv_w, conv_b, gdn_a_log, gdn_dt_bias, gdn_norm_g,
             ssm_a_log, ssm_dt_bias, ssm_d, ssm_norm_g, w_out):
    f32 = jnp.float32
    B_, L, _ = h.shape
    proj = (h @ w_in).astype(f32)
    xbc, gate, z, b_raw, a_raw, dt_raw = jnp.split(proj, _cuts(AB_SPLIT), axis=-1)
    xbc, conv_new = causal_conv_silu(xbc, conv_buf, conv_w.astype(f32), conv_b.astype(f32))
    q, k, v, xs, bm, cm = jnp.split(xbc, _cuts(CONV_SPLIT), axis=-1)
    q = l2norm(q.reshape(B_, L, GDN_HEADS, GDN_DK)) * GDN_DK ** -0.5
    k = l2norm(k.reshape(B_, L, GDN_HEADS, GDN_DK))
    v = v.reshape(B_, L, GDN_HEADS, GDN_DV)
    beta = jax.nn.sigmoid(b_raw)
    g = -jnp.exp(gdn_a_log.astype(f32)) * jax.nn.softplus(a_raw + gdn_dt_bias.astype(f32))
    o_a, gdn_new = gated_delta_rule(q, k, v, beta, g, gdn_s)
    o_a = rmsnorm(o_a, gdn_norm_g) * jax.nn.silu(gate.reshape(B_, L, GDN_HEADS, GDN_DV))
    dt = jax.nn.softplus(dt_raw + ssm_dt_bias.astype(f32))
    xs = xs.reshape(B_, L, SSM_HEADS, SSM_P)
    rep = SSM_HEADS // SSM_G
    bm = jnp.repeat(bm.reshape(B_, L, SSM_G, SSM_N), rep, axis=2)
    cm = jnp.repeat(cm.reshape(B_, L, SSM_G, SSM_N), rep, axis=2)
    log_a = -dt * jnp.exp(ssm_a_log.astype(f32))
    o_b, ssm_new = decay_linear_attn(cm, bm * dt[..., None], xs, log_a, ssm_s)
    o_b = (o_b + ssm_d.astype(f32)[:, None] * xs).reshape(B_, L, SSM_INNER)
    o_b = rmsnorm(o_b * jax.nn.silu(z), ssm_norm_g)
    out = jnp.concatenate([o_a.reshape(B_, L, GDN_V), o_b], axis=-1) @ w_out
    return out.astype(h.dtype), conv_new, gdn_new, ssm_new


def mixer_ret(h, ret_s, pos0, w_in, norm_g, w_out):
    f32 = jnp.float32
    B_, L, _ = h.shape
    proj = (h @ w_in).astype(f32)
    q, k, v, gate = jnp.split(proj, _cuts(RET_SPLIT), axis=-1)
    q = rope(q.reshape(B_, L, RET_HEADS, RET_DK), pos0)
    k = rope(k.reshape(B_, L, RET_HEADS, RET_DK), pos0) * RET_DK ** -0.5
    v = v.reshape(B_, L, RET_HEADS, RET_DV)
    log_gamma = jnp.log(1.0 - 2.0 ** (-5.0 - jnp.arange(RET_HEADS, dtype=f32)))
    log_a = jnp.broadcast_to(log_gamma, (B_, L, RET_HEADS))
    o, s = decay_linear_attn(q, k, v, log_a, ret_s)
    mu = jnp.mean(o, axis=-1, keepdims=True)
    var = jnp.mean(jnp.square(o - mu), axis=-1, keepdims=True)
    o = (o - mu) * lax.rsqrt(var + EPS) * norm_g.astype(f32)
    o = jax.nn.silu(gate) * o.reshape(B_, L, RET_V)
    return (o @ w_out).astype(h.dtype), s


def peer(h, w_q, keys, u, v):
    B_, L, D = h.shape
    T = B_ * L
    blk = min(PEER_BLOCK, T)
    nblk = -(-T // blk)
    xt = jnp.pad(h.reshape(T, D), ((0, nblk * blk - T), (0, 0))).reshape(nblk, blk, D)

    def block(xb):
        q = (xb @ w_q).astype(jnp.float32).reshape(blk, PEER_HEADS, 2, PEER_DQ // 2)
        s = jnp.einsum('thpd,pnd->thpn', q, keys.astype(jnp.float32))
        sv, si = lax.top_k(s, PEER_TOPK)
        cand = (sv[:, :, 0, :, None] + sv[:, :, 1, None, :]).reshape(blk, PEER_HEADS, PEER_TOPK * PEER_TOPK)
        cv, ci = lax.top_k(cand, PEER_TOPK)
        idx = (jnp.take_along_axis(si[:, :, 0], ci // PEER_TOPK, axis=-1) * PEER_NKEYS
               + jnp.take_along_axis(si[:, :, 1], ci % PEER_TOPK, axis=-1))
        gate = jax.nn.softmax(cv, axis=-1)
        act = jax.nn.gelu(jnp.einsum('td,thkd->thk', xb, u[idx]).astype(jnp.float32), approximate=False) * gate
        return jnp.einsum('thk,thkd->td', act.astype(xb.dtype), v[idx])

    y = lax.map(block, xt)
    return y.reshape(nblk * blk, D)[:T].reshape(B_, L, D)


def trunk(x, c, pos0, conv0, gdn0, ssm0, ret0, weights):
    (ada_w, ada_b, norm1_g, norm2_g, ab_w_in, ab_conv_w, ab_conv_b, gdn_a_log, gdn_dt_bias,
     gdn_norm_g, ssm_a_log, ssm_dt_bias, ssm_d, ssm_norm_g, ab_w_out, ret_w_in, ret_norm_g,
     ret_w_out, peer_w_q, peer_keys, peer_u, peer_v, final_g) = weights
    convs, gdns, ssms, rets = [], [], [], []
    for layer in range(DEPTH):
        mod = jax.nn.silu(c) @ ada_w[layer] + ada_b[layer]
        sh1, sc1, g1, sh2, sc2, g2 = jnp.split(mod[:, None, :], 6, axis=-1)
        hm = rmsnorm(x, norm1_g[layer]) * (1 + sc1) + sh1
        i = layer // 2
        if layer % 2 == 0:
            out, cs, gs, ss = mixer_ab(hm, conv0[i], gdn0[i], ssm0[i], ab_w_in[i], ab_conv_w[i],
                                       ab_conv_b[i], gdn_a_log[i], gdn_dt_bias[i], gdn_norm_g[i],
                                       ssm_a_log[i], ssm_dt_bias[i], ssm_d[i], ssm_norm_g[i],
                                       ab_w_out[i])
            convs.append(cs)
            gdns.append(gs)
            ssms.append(ss)
        else:
            out, rs = mixer_ret(hm, ret0[i], pos0, ret_w_in[i], ret_norm_g[i], ret_w_out[i])
            rets.append(rs)
        x = x + g1 * out
        hm = rmsnorm(x, norm2_g[layer]) * (1 + sc2) + sh2
        x = x + g2 * peer(hm, peer_w_q[layer], peer_keys[layer], peer_u[layer], peer_v[layer])
    return rmsnorm(x, final_g), jnp.stack(convs), jnp.stack(gdns), jnp.stack(ssms), jnp.stack(rets)


def setup_inputs(seed: int = 0) -> dict:
    key = jax.random.key(seed)
    ks = iter(jax.random.split(key, 40))
    f32 = jnp.float32

    def nrm(shape, scale):
        return jax.random.normal(next(ks), shape, f32) * scale

    def dt_bias(shape):
        dt = jnp.exp(jax.random.uniform(next(ks), shape, f32, math.log(1e-3), math.log(1e-1)))
        return dt + jnp.log(-jnp.expm1(-dt))

    def a_log(shape):
        return jnp.log(jax.random.uniform(next(ks), shape, f32, 1.0, 16.0))

    return {
        'x_prompt': nrm((BATCH, SEQ, D_MODEL), 1.0),
        'x_sample': nrm((DEC_BATCH, DEC_SEQ, D_MODEL), 1.0),
        'c_prompt': nrm((BATCH, D_MODEL), 1.0),
        'c_sample': nrm((DEC_BATCH, D_MODEL), 1.0),
        'state_conv': nrm((N_AB, DEC_BATCH, CONV_W - 1, CONV_CH), 1.0),
        'state_gdn': nrm((N_AB, DEC_BATCH, GDN_HEADS, GDN_DK, GDN_DV), 0.1),
        'state_ssm': nrm((N_AB, DEC_BATCH, SSM_HEADS, SSM_N, SSM_P), 0.1),
        'state_ret': nrm((N_RET, DEC_BATCH, RET_HEADS, RET_DK, RET_DV), 0.1),
        'ada_w': nrm((DEPTH, D_MODEL, 6 * D_MODEL), 0.5 * D_MODEL ** -0.5),
        'ada_b': nrm((DEPTH, 6 * D_MODEL), 0.02),
        'norm1_g': 1.0 + nrm((DEPTH, D_MODEL), 0.02),
        'norm2_g': 1.0 + nrm((DEPTH, D_MODEL), 0.02),
        'ab_w_in': nrm((N_AB, D_MODEL, AB_IN), D_MODEL ** -0.5),
        'ab_conv_w': nrm((N_AB, CONV_W, CONV_CH), CONV_W ** -0.5),
        'ab_conv_b': nrm((N_AB, CONV_CH), 0.02),
        'gdn_a_log': a_log((N_AB, GDN_HEADS)),
        'gdn_dt_bias': dt_bias((N_AB, GDN_HEADS)),
        'gdn_norm_g': 1.0 + nrm((N_AB, GDN_DV), 0.02),
        'ssm_a_log': a_log((N_AB, SSM_HEADS)),
        'ssm_dt_bias': dt_bias((N_AB, SSM_HEADS)),
        'ssm_d': 1.0 + nrm((N_AB, SSM_HEADS), 0.1),
        'ssm_norm_g': 1.0 + nrm((N_AB, SSM_INNER), 0.02),
        'ab_w_out': nrm((N_AB, AB_OUT, D_MODEL), AB_OUT ** -0.5),
        'ret_w_in': nrm((N_RET, D_MODEL, RET_IN), D_MODEL ** -0.5),
        'ret_norm_g': 1.0 + nrm((N_RET, RET_HEADS, RET_DV), 0.02),
        'ret_w_out': nrm((N_RET, RET_V, D_MODEL), RET_V ** -0.5),
        'peer_w_q': nrm((DEPTH, D_MODEL, PEER_HEADS * PEER_DQ), D_MODEL ** -0.5),
        'peer_keys': nrm((DEPTH, 2, PEER_NKEYS, PEER_DQ // 2), (PEER_DQ // 2) ** -0.5),
        'peer_u': nrm((DEPTH, PEER_EXPERTS, D_MODEL), D_MODEL ** -0.5),
        'peer_v': nrm((DEPTH, PEER_EXPERTS, D_MODEL), 0.3),
        'final_g': 1.0 + nrm((D_MODEL,), 0.02),
    }


def reference(x_prompt, x_sample, c_prompt, c_sample, state_conv, state_gdn, state_ssm, state_ret,
              ada_w, ada_b, norm1_g, norm2_g, ab_w_in, ab_conv_w, ab_conv_b, gdn_a_log, gdn_dt_bias,
              gdn_norm_g, ssm_a_log, ssm_dt_bias, ssm_d, ssm_norm_g, ab_w_out, ret_w_in, ret_norm_g,
              ret_w_out, peer_w_q, peer_keys, peer_u, peer_v, final_g):
    weights = (ada_w, ada_b, norm1_g, norm2_g, ab_w_in, ab_conv_w, ab_conv_b, gdn_a_log, gdn_dt_bias,
               gdn_norm_g, ssm_a_log, ssm_dt_bias, ssm_d, ssm_norm_g, ab_w_out, ret_w_in, ret_norm_g,
               ret_w_out, peer_w_q, peer_keys, peer_u, peer_v, final_g)
    nb = x_prompt.shape[0]
    zeros = lambda s: jnp.zeros((s.shape[0], nb) + s.shape[2:], s.dtype)
    y_prompt, p_conv, p_gdn, p_ssm, p_ret = trunk(
        x_prompt, c_prompt, 0, zeros(state_conv), zeros(state_gdn), zeros(state_ssm), zeros(state_ret), weights)
    y_sample, s_conv, s_gdn, s_ssm, s_ret = trunk(
        x_sample, c_sample, PAST_LEN, state_conv, state_gdn, state_ssm, state_ret, weights)
    return (y_prompt, y_sample, p_conv, p_gdn, p_ssm, p_ret, s_conv, s_gdn, s_ssm, s_ret)
```

```python
import functools
import math

import jax
import jax.numpy as jnp
from jax import lax
from jax.experimental import pallas as pl
from jax.experimental.pallas import tpu as pltpu

f32 = jnp.float32
bf16 = jnp.bfloat16
HI = lax.Precision.HIGHEST

D_MODEL = 1024
DEPTH = 2
PAST_LEN = 16384
GDN_HEADS = 8
GDN_DK = 128
GDN_DV = 128
SSM_HEADS = 16
SSM_P = 64
SSM_N = 128
SSM_G = 2
CONV_W = 4
RET_HEADS = 4
RET_DK = 256
RET_DV = 512
ROPE_BASE = 10000.0
PEER_HEADS = 8
PEER_NKEYS = 128
PEER_TOPK = 16
PEER_DQ = 256
CHUNK = 64
EPS = 1e-6

GDN_QK = GDN_HEADS * GDN_DK
GDN_V = GDN_HEADS * GDN_DV
SSM_INNER = SSM_HEADS * SSM_P
SSM_BC = SSM_G * SSM_N
CONV_CH = 2 * GDN_QK + GDN_V + SSM_INNER + 2 * SSM_BC
AB_MAIN = CONV_CH + GDN_V + SSM_INNER
AB_SMALL = 2 * GDN_HEADS + SSM_HEADS
RET_QK = RET_HEADS * RET_DK
RET_V = RET_HEADS * RET_DV
RET_IN = 2 * RET_QK + 2 * RET_V
N_EXPERTS = PEER_NKEYS * PEER_NKEYS

VMEM_LIMIT = 56 * 1024 * 1024

_CAND = [(a, b) for a in range(PEER_TOPK) for b in range(PEER_TOPK) if (a + 1) * (b + 1) <= PEER_TOPK]
_NCAND = len(_CAND)
_NCAND_PAD = -(-_NCAND // 16) * 16


def _params(sem):
    return pltpu.CompilerParams(dimension_semantics=sem, vmem_limit_bytes=VMEM_LIMIT)


def _nt(a, b, **kw):
    return lax.dot_general(a, b, (((1,), (1,)), ((), ())), preferred_element_type=f32, **kw)


def _tn(a, b, **kw):
    return lax.dot_general(a, b, (((0,), (0,)), ((), ())), preferred_element_type=f32, **kw)


def _mm(a, b, **kw):
    return jnp.dot(a, b, preferred_element_type=f32, **kw)


def _silu(x):
    return x * jax.nn.sigmoid(x)


def _gelu(x):
    return 0.5 * x * (1.0 + lax.erf(x * (2.0 ** -0.5)))


def _rms(x):
    return x * lax.rsqrt(jnp.mean(x * x, axis=-1, keepdims=True) + EPS)


def _mod_kernel(c_ref, w_ref, b_ref, o_ref):
    a = _silu(c_ref[...]).astype(bf16)
    o_ref[0] = _mm(a, w_ref[0].astype(bf16)) + b_ref[0]


def ada_mod(c_all, ada_w, ada_b):
    m = c_all.shape[0]
    tn = 768
    n = ada_w.shape[-1]
    return pl.pallas_call(
        _mod_kernel,
        grid=(DEPTH, n // tn),
        in_specs=[
            pl.BlockSpec((m, D_MODEL), lambda l, j: (0, 0)),
            pl.BlockSpec((1, D_MODEL, tn), lambda l, j: (l, 0, j)),
            pl.BlockSpec((1, 1, tn), lambda l, j: (l, 0, j)),
        ],
        out_specs=pl.BlockSpec((1, m, tn), lambda l, j: (l, 0, j)),
        out_shape=jax.ShapeDtypeStruct((DEPTH, m, n), f32),
        compiler_params=_params(("arbitrary", "arbitrary")),
        name="ada_mod",
    )(c_all, ada_w, ada_b.reshape(DEPTH, 1, n))


def _proj_kernel(x_ref, g_ref, sc_ref, sh_ref, w_ref, *rest, has_small):
    if has_small:
        ws_ref, o_ref, os_ref, hm_ref = rest
    else:
        o_ref, hm_ref = rest
    j = pl.program_id(1)

    @pl.when(j == 0)
    def _():
        y = _rms(x_ref[...]) * g_ref[...]
        hm = (y * (1.0 + sc_ref[0]) + sh_ref[0]).astype(bf16)
        hm_ref[...] = hm
        if has_small:
            os_ref[...] = _nt(ws_ref[...], hm)

    o_ref[...] = _mm(hm_ref[...], w_ref[...])


def norm_proj(x, gamma, sc, sh, mod_map, w, ws_t, tm, tn):
    t = x.shape[0]
    n = w.shape[1]
    r = sc.shape[1]
    has_small = ws_t is not None
    in_specs = [
        pl.BlockSpec((tm, D_MODEL), lambda i, j: (i, 0)),
        pl.BlockSpec((1, D_MODEL), lambda i, j: (0, 0)),
        pl.BlockSpec((1, r, D_MODEL), lambda i, j: mod_map(i)),
        pl.BlockSpec((1, r, D_MODEL), lambda i, j: mod_map(i)),
        pl.BlockSpec((D_MODEL, tn), lambda i, j: (0, j)),
    ]
    out_specs = [pl.BlockSpec((tm, tn), lambda i, j: (i, j))]
    out_shape = [jax.ShapeDtypeStruct((t, n), f32)]
    args = [x, gamma.reshape(1, D_MODEL), sc, sh, w]
    if has_small:
        s = ws_t.shape[0]
        in_specs.append(pl.BlockSpec((s, D_MODEL), lambda i, j: (0, 0)))
        out_specs.append(pl.BlockSpec((s, tm), lambda i, j: (0, i)))
        out_shape.append(jax.ShapeDtypeStruct((s, t), f32))
        args.append(ws_t)
    res = pl.pallas_call(
        functools.partial(_proj_kernel, has_small=has_small),
        grid=(t // tm, n // tn),
        in_specs=in_specs,
        out_specs=out_specs,
        out_shape=out_shape,
        scratch_shapes=[pltpu.VMEM((tm, D_MODEL), bf16)],
        compiler_params=_params(("arbitrary", "arbitrary")),
        name="norm_proj",
    )(*args)
    return res if has_small else res[0]


def _out_kernel(o_ref, w_ref, x_ref, g_ref, y_ref):
    y_ref[...] = x_ref[...] + g_ref[0] * _mm(o_ref[...], w_ref[...])


def out_proj(o, w, x, gate, mod_map, tm):
    t, k = o.shape
    r = gate.shape[1]
    return pl.pallas_call(
        _out_kernel,
        grid=(t // tm,),
        in_specs=[
            pl.BlockSpec((tm, k), lambda i: (i, 0)),
            pl.BlockSpec((k, D_MODEL), lambda i: (0, 0)),
            pl.BlockSpec((tm, D_MODEL), lambda i: (i, 0)),
            pl.BlockSpec((1, r, D_MODEL), lambda i: mod_map(i)),
        ],
        out_specs=pl.BlockSpec((tm, D_MODEL), lambda i: (i, 0)),
        out_shape=jax.ShapeDtypeStruct((t, D_MODEL), f32),
        compiler_params=_params(("arbitrary",)),
        name="out_proj",
    )(o, w, x, gate)


def _tri(c, strict=False):
    r = lax.broadcasted_iota(jnp.int32, (c, c), 0)
    q = lax.broadcasted_iota(jnp.int32, (c, c), 1)
    return (r > q) if strict else (r >= q)


def _unit_lower_inverse(lm, c):
    eye = (lax.broadcasted_iota(jnp.int32, (c, c), 0) == lax.broadcasted_iota(jnp.int32, (c, c), 1)).astype(f32)
    p = eye - lm
    lp = lm
    k = 2
    while k < c:
        lp = _mm(lp, lp, precision=HI)
        p = p + _mm(p, lp, precision=HI)
        k *= 2
    return p


def _ab_kernel(proj_ref, smt_ref, conv0_ref, gdn0_ref, ssm0_ref, convw_ref, convb_ref, pcol_ref, gng_ref,
               dskip_ref, sng_ref, o_ref, convn_ref, gdnn_ref, ssmn_ref, xbuf, act, sg, ss, ob, *, c, lv):
    l = pl.program_id(1)
    nl = pl.num_programs(1)

    @pl.when(l == 0)
    def _():
        xbuf[...] = jnp.zeros(xbuf.shape, f32)
        xbuf[5:8, :] = conv0_ref[0]
        sg[...] = gdn0_ref[0]
        ss[...] = ssm0_ref[0]

    xbuf[8:8 + lv, :] = proj_ref[0, :, 0:CONV_CH]
    y = convb_ref[...]
    for i in range(CONV_W):
        y = y + convw_ref[i:i + 1, :] * xbuf[5 + i:5 + i + c, :]
    act[...] = _silu(y)
    tail = xbuf[5 + lv:8 + lv, :]
    xbuf[5:8, :] = tail

    if lv == c:
        sm = smt_ref[0, 0]
    else:
        lane = lax.broadcasted_iota(jnp.int32, (AB_SMALL, c), 1)
        sm = jnp.where(lane < lv, jnp.broadcast_to(smt_ref[0, 0], (AB_SMALL, c)), 0.0)
    valid = lax.broadcasted_iota(jnp.int32, (AB_SMALL, c), 1) < lv
    bias_col = pcol_ref[:, 0:1]
    alog_col = pcol_ref[:, 1:2]
    beta_t = jnp.where(valid, jax.nn.sigmoid(sm), 0.0)
    sp_t = jnp.where(valid, jax.nn.softplus(sm + bias_col), 0.0)
    la_t = -jnp.exp(alog_col) * sp_t
    rr = lax.broadcasted_iota(jnp.int32, (c, c), 0)
    qq = lax.broadcasted_iota(jnp.int32, (c, c), 1)
    triu = (rr <= qq).astype(f32)
    eye = (rr == qq).astype(f32)
    cum_t = _mm(la_t, triu, precision=HI)
    last_t = cum_t[:, c - 1:c]
    ecum_t = jnp.exp(cum_t)
    elc_t = jnp.exp(last_t - cum_t)
    elast_t = jnp.exp(last_t)
    g0, g1, s0, s1 = 0, GDN_HEADS, 2 * GDN_HEADS, AB_SMALL
    rows = jnp.concatenate([
        beta_t[g0:g1],
        cum_t[g1:s0],
        ecum_t[g1:s0],
        beta_t[g0:g1] * ecum_t[g1:s0],
        elc_t[g1:s0],
        cum_t[s0:s1],
        ecum_t[s0:s1],
        sp_t[s0:s1] * elc_t[s0:s1],
        jnp.zeros((128 - 88, c), f32),
    ], axis=0)
    cols = _nt(eye, rows, precision=HI)
    incl = rr >= qq
    strict = rr > qq

    def col(k):
        return cols[:, k:k + 1]

    for h in range(GDN_HEADS):
        q = act[:, h * GDN_DK:(h + 1) * GDN_DK]
        k = act[:, GDN_QK + h * GDN_DK:GDN_QK + (h + 1) * GDN_DK]
        v = act[:, 2 * GDN_QK + h * GDN_DV:2 * GDN_QK + (h + 1) * GDN_DV]
        q = q * lax.rsqrt(jnp.sum(q * q, axis=-1, keepdims=True) + EPS) * (GDN_DK ** -0.5)
        k = k * lax.rsqrt(jnp.sum(k * k, axis=-1, keepdims=True) + EPS)
        beta_c, cum_c, ecum_c, becum_c, elc_c = col(h), col(8 + h), col(16 + h), col(24 + h), col(32 + h)
        cum_r = cum_t[g1 + h:g1 + h + 1, :]
        dec = jnp.exp(jnp.where(incl, cum_c - cum_r, -jnp.inf))
        kb = k.astype(bf16)
        qb = q.astype(bf16)
        lm = jnp.where(strict, beta_c * _nt(kb, kb) * dec, 0.0)
        pinv = _unit_lower_inverse(lm, c)
        rhs = jnp.concatenate([v * beta_c, k * becum_c], axis=1)
        sol = _mm(pinv, rhs, precision=HI)
        u0 = sol[:, :GDN_DV]
        w = sol[:, GDN_DV:]
        qk = _nt(qb, kb) * dec
        s_old = sg[h]
        sb = s_old.astype(bf16)
        u = u0 - _mm(w.astype(bf16), sb)
        ub = u.astype(bf16)
        o = _mm((q * ecum_c).astype(bf16), sb) + _mm(qk.astype(bf16), ub)
        kd = (k * elc_c).astype(bf16)
        sg[h] = s_old * elast_t[g1 + h:g1 + h + 1, :] + _tn(kd, ub)
        gate = proj_ref[0, :, CONV_CH + h * GDN_DV:CONV_CH + (h + 1) * GDN_DV]
        oa = _rms(o) * gng_ref[...]
        o_ref[0, :, h * GDN_DV:(h + 1) * GDN_DV] = (oa[0:lv] * _silu(gate)).astype(bf16)

    rep = SSM_HEADS // SSM_G
    xs0 = 2 * GDN_QK + GDN_V
    bm0 = xs0 + SSM_INNER
    cm0 = bm0 + SSM_BC
    for g in range(SSM_G):
        bmb = act[:, bm0 + g * SSM_N:bm0 + (g + 1) * SSM_N].astype(bf16)
        cmb = act[:, cm0 + g * SSM_N:cm0 + (g + 1) * SSM_N].astype(bf16)
        cb = _nt(cmb, bmb)
        for hh in range(rep):
            h = g * rep + hh
            cum_c, ecum_c, dtelc_c = col(40 + h), col(56 + h), col(72 + h)
            cum_r = cum_t[s0 + h:s0 + h + 1, :]
            dt_r = sp_t[s0 + h:s0 + h + 1, :]
            m = cb * jnp.exp(jnp.where(incl, cum_c - cum_r, -jnp.inf)) * dt_r
            xs = act[:, xs0 + h * SSM_P:xs0 + (h + 1) * SSM_P]
            s_old = ss[h]
            o = (_mm(m.astype(bf16), xs.astype(bf16)) + _mm(cmb, s_old.astype(bf16)) * ecum_c
                 + dskip_ref[:, h * SSM_P:(h + 1) * SSM_P] * xs)
            ss[h] = s_old * elast_t[s0 + h:s0 + h + 1, :] + _tn(bmb, (xs * dtelc_c).astype(bf16))
            ob[:, h * SSM_P:(h + 1) * SSM_P] = o
    z = proj_ref[0, :, CONV_CH + GDN_V:CONV_CH + GDN_V + SSM_INNER]
    obv = _rms(ob[0:lv, :] * _silu(z)) * sng_ref[...]
    o_ref[0, :, GDN_V:GDN_V + SSM_INNER] = obv.astype(bf16)

    @pl.when(l == nl - 1)
    def _():
        convn_ref[0] = xbuf[5:8, :]
        gdnn_ref[0] = sg[...]
        ssmn_ref[0] = ss[...]


def mixer_ab(proj, smt, conv0, gdn0, ssm0, conv_w, conv_b, pcol, gdn_norm_g, dskip, ssm_norm_g, c):
    b, l, npad = proj.shape
    tl = min(c, l)
    nl = l // tl
    kern = functools.partial(_ab_kernel, c=c, lv=tl)
    full = lambda shape: pl.BlockSpec(shape, lambda i, j: (0,) * len(shape))
    return pl.pallas_call(
        kern,
        grid=(b, nl),
        in_specs=[
            pl.BlockSpec((1, tl, npad), lambda i, j: (i, j, 0)),
            pl.BlockSpec((1, 1, AB_SMALL, tl), lambda i, j: (i, j, 0, 0)),
            pl.BlockSpec((1, CONV_W - 1, CONV_CH), lambda i, j: (i, 0, 0)),
            pl.BlockSpec((1, GDN_HEADS, GDN_DK, GDN_DV), lambda i, j: (i, 0, 0, 0)),
            pl.BlockSpec((1, SSM_HEADS, SSM_N, SSM_P), lambda i, j: (i, 0, 0, 0)),
            full((CONV_W, CONV_CH)),
            full((1, CONV_CH)),
            full((AB_SMALL, 2)),
            full((1, GDN_DV)),
            full((1, SSM_INNER)),
            full((1, SSM_INNER)),
        ],
        out_specs=[
            pl.BlockSpec((1, tl, GDN_V + SSM_INNER), lambda i, j: (i, j, 0)),
            pl.BlockSpec((1, CONV_W - 1, CONV_CH), lambda i, j: (i, 0, 0)),
            pl.BlockSpec((1, GDN_HEADS, GDN_DK, GDN_DV), lambda i, j: (i, 0, 0, 0)),
            pl.BlockSpec((1, SSM_HEADS, SSM_N, SSM_P), lambda i, j: (i, 0, 0, 0)),
        ],
        out_shape=[
            jax.ShapeDtypeStruct((b, l, GDN_V + SSM_INNER), bf16),
            jax.ShapeDtypeStruct((b, CONV_W - 1, CONV_CH), f32),
            jax.ShapeDtypeStruct((b, GDN_HEADS, GDN_DK, GDN_DV), f32),
            jax.ShapeDtypeStruct((b, SSM_HEADS, SSM_N, SSM_P), f32),
        ],
        scratch_shapes=[
            pltpu.VMEM((8 + c, CONV_CH), f32),
            pltpu.VMEM((c, CONV_CH), f32),
            pltpu.VMEM((GDN_HEADS, GDN_DK, GDN_DV), f32),
            pltpu.VMEM((SSM_HEADS, SSM_N, SSM_P), f32),
            pltpu.VMEM((c, SSM_INNER), f32),
        ],
        compiler_params=_params(("arbitrary", "arbitrary")),
        name="mixer_ab",
    )(proj, smt, conv0, gdn0, ssm0, conv_w, conv_b, pcol, gdn_norm_g, dskip, ssm_norm_g)


def _ret_kernel(proj_ref, cos_ref, sin_ref, ret0_ref, ng_ref, o_ref, retn_ref, st, buf, *, c, lv):
    l = pl.program_id(1)
    nl = pl.num_programs(1)

    @pl.when(l == 0)
    def _():
        st[...] = ret0_ref[0]
        if lv < c:
            buf[...] = jnp.zeros(buf.shape, f32)

    if lv == c:
        src = proj_ref.at[0]
        cos = cos_ref[...]
        sin = sin_ref[...]
    else:
        buf[0:lv, :] = proj_ref[0]
        src = buf
        cos = jnp.broadcast_to(cos_ref[...], (c, RET_DK // 2))
        sin = jnp.broadcast_to(sin_ref[...], (c, RET_DK // 2))

    ri = lax.broadcasted_iota(jnp.int32, (c, c), 0)
    ci = lax.broadcasted_iota(jnp.int32, (c, c), 1)
    incl = ri >= ci
    cnt_r = jnp.minimum(ri + 1, lv).astype(f32)
    cnt_c = jnp.minimum(ci + 1, lv).astype(f32)
    cnt_col = jnp.minimum(lax.broadcasted_iota(jnp.int32, (c, 1), 0) + 1, lv).astype(f32)
    half = RET_DK // 2

    def rope(x):
        x1, x2 = x[:, :half], x[:, half:]
        return jnp.concatenate([x1 * cos - x2 * sin, x1 * sin + x2 * cos], axis=1)

    for h in range(RET_HEADS):
        lg = math.log(1.0 - 2.0 ** (-5.0 - h))
        q = rope(src[:, h * RET_DK:(h + 1) * RET_DK])
        k = rope(src[:, RET_QK + h * RET_DK:RET_QK + (h + 1) * RET_DK]) * (RET_DK ** -0.5)
        v = src[:, 2 * RET_QK + h * RET_DV:2 * RET_QK + (h + 1) * RET_DV].astype(bf16)
        gate = src[:, 2 * RET_QK + RET_V + h * RET_DV:2 * RET_QK + RET_V + (h + 1) * RET_DV]
        dec = jnp.exp(jnp.where(incl, (cnt_r - cnt_c) * lg, -jnp.inf))
        scores = _nt(q.astype(bf16), k.astype(bf16)) * dec
        s_old = st[h]
        o = _mm(scores.astype(bf16), v) + _mm((q * jnp.exp(cnt_col * lg)).astype(bf16), s_old.astype(bf16))
        kd = (k * jnp.exp((lv - cnt_col) * lg)).astype(bf16)
        st[h] = s_old * math.exp(lv * lg) + _tn(kd, v)
        mu = jnp.mean(o, axis=-1, keepdims=True)
        var = jnp.mean(jnp.square(o - mu), axis=-1, keepdims=True)
        o = (o - mu) * lax.rsqrt(var + EPS) * ng_ref[h:h + 1, :]
        o_ref[0, :, h * RET_DV:(h + 1) * RET_DV] = (_silu(gate) * o)[0:lv].astype(bf16)

    @pl.when(l == nl - 1)
    def _():
        retn_ref[0] = st[...]


def mixer_ret(proj, cos, sin, ret0, norm_g, c):
    b, l, n = proj.shape
    tl = min(c, l)
    nl = l // tl
    kern = functools.partial(_ret_kernel, c=c, lv=tl)
    return pl.pallas_call(
        kern,
        grid=(b, nl),
        in_specs=[
            pl.BlockSpec((1, tl, n), lambda i, j: (i, j, 0)),
            pl.BlockSpec((tl, RET_DK // 2), lambda i, j: (j, 0)),
            pl.BlockSpec((tl, RET_DK // 2), lambda i, j: (j, 0)),
            pl.BlockSpec((1, RET_HEADS, RET_DK, RET_DV), lambda i, j: (i, 0, 0, 0)),
            pl.BlockSpec((RET_HEADS, RET_DV), lambda i, j: (0, 0)),
        ],
        out_specs=[
            pl.BlockSpec((1, tl, RET_V), lambda i, j: (i, j, 0)),
            pl.BlockSpec((1, RET_HEADS, RET_DK, RET_DV), lambda i, j: (i, 0, 0, 0)),
        ],
        out_shape=[
            jax.ShapeDtypeStruct((b, l, RET_V), bf16),
            jax.ShapeDtypeStruct((b, RET_HEADS, RET_DK, RET_DV), f32),
        ],
        scratch_shapes=[
            pltpu.VMEM((RET_HEADS, RET_DK, RET_DV), f32),
            pltpu.VMEM((c, n), f32),
        ],
        compiler_params=_params(("arbitrary", "arbitrary")),
        name="mixer_ret",
    )(proj, cos, sin, ret0, norm_g)


def _top16(s, n):
    t = s.shape[1]
    rows = lax.broadcasted_iota(jnp.int32, (n, t), 0)
    krow = lax.broadcasted_iota(jnp.int32, (PEER_TOPK, t), 0)
    rank = jnp.full((n, t), float(PEER_TOPK), f32)
    sv = jnp.zeros((PEER_TOPK, t), f32)
    work = s
    for k in range(PEER_TOPK):
        m = jnp.max(work, axis=0, keepdims=True)
        idx = jnp.min(jnp.where(work == m, rows, n), axis=0, keepdims=True)
        sel = rows == idx
        rank = jnp.where(sel, float(k), rank)
        work = jnp.where(sel, -jnp.inf, work)
        sv = jnp.where(krow == k, m, sv)
    return sv, rank


def _topk_kernel(x_ref, g_ref, sc_ref, sh_ref, wqt_ref, keys_ref, a1_ref, a2_ref, a1t_ref,
                 hmt_ref, r2_ref, e2_ref, lc_ref, g1_ref):
    y = _rms(x_ref[...]) * g_ref[...]
    hm = y * (1.0 + sc_ref[0]) + sh_ref[0]
    hmt = hm.T.astype(bf16)
    hmt_ref[...] = hmt
    qt = _mm(wqt_ref[...], hmt)
    tt = qt.shape[1]
    crow = lax.broadcasted_iota(jnp.int32, (_NCAND_PAD, tt), 0)
    half = PEER_DQ // 2
    for h in range(PEER_HEADS):
        s1 = _mm(keys_ref[0], qt[h * PEER_DQ:h * PEER_DQ + half, :].astype(bf16))
        s2 = _mm(keys_ref[1], qt[h * PEER_DQ + half:(h + 1) * PEER_DQ, :].astype(bf16))
        sv1, rank1 = _top16(s1, PEER_NKEYS)
        sv2, rank2 = _top16(s2, PEER_NKEYS)
        cand = _mm(a1_ref[...], sv1, precision=HI) + _mm(a2_ref[...], sv2, precision=HI)
        cand = jnp.where(crow < _NCAND, cand, -jnp.inf)
        cmax = cand[0:1, :]
        work = cand
        sel_all = jnp.zeros(cand.shape, f32)
        for _ in range(PEER_TOPK):
            m = jnp.max(work, axis=0, keepdims=True)
            idx = jnp.min(jnp.where(work == m, crow, _NCAND_PAD), axis=0, keepdims=True)
            sel = crow == idx
            sel_all = jnp.where(sel, 1.0, sel_all)
            work = jnp.where(sel, -jnp.inf, work)
        z = jnp.sum(sel_all * jnp.exp(jnp.where(crow < _NCAND, cand - cmax, 0.0)), axis=0, keepdims=True)
        cnt = _mm(a1t_ref[...], sel_all.astype(bf16))
        lc = jnp.zeros((PEER_NKEYS, tt), f32)
        for k1 in range(PEER_TOPK):
            lc = jnp.where(rank1 == float(k1), cnt[k1:k1 + 1, :], lc)
        r2_ref[h] = rank2
        e2_ref[h] = jnp.exp(s2 - sv2[0:1, :])
        lc_ref[h] = lc
        g1_ref[h] = jnp.exp(s1 - sv1[0:1, :]) / z


def peer_topk(x, gamma, sc, sh, mod_map, wq_t, keys, tt):
    t = x.shape[0]
    r = sc.shape[1]
    a1 = jnp.zeros((_NCAND_PAD, PEER_TOPK), f32).at[jnp.arange(_NCAND), jnp.array([a for a, _ in _CAND])].set(1.0)
    a2 = jnp.zeros((_NCAND_PAD, PEER_TOPK), f32).at[jnp.arange(_NCAND), jnp.array([b for _, b in _CAND])].set(1.0)
    full = lambda shape: pl.BlockSpec(shape, lambda i: (0,) * len(shape))
    tok = pl.BlockSpec((PEER_HEADS, PEER_NKEYS, tt), lambda i: (0, 0, i))
    tok_shape = jax.ShapeDtypeStruct((PEER_HEADS, PEER_NKEYS, t), f32)
    return pl.pallas_call(
        _topk_kernel,
        grid=(t // tt,),
        in_specs=[
            pl.BlockSpec((tt, D_MODEL), lambda i: (i, 0)),
            full((1, D_MODEL)),
            pl.BlockSpec((1, r, D_MODEL), lambda i: mod_map(i)),
            pl.BlockSpec((1, r, D_MODEL), lambda i: mod_map(i)),
            full((PEER_HEADS * PEER_DQ, D_MODEL)),
            full((2, PEER_NKEYS, PEER_DQ // 2)),
            full((_NCAND_PAD, PEER_TOPK)),
            full((_NCAND_PAD, PEER_TOPK)),
            full((PEER_TOPK, _NCAND_PAD)),
        ],
        out_specs=[pl.BlockSpec((D_MODEL, tt), lambda i: (0, i)), tok, tok, tok, tok],
        out_shape=[jax.ShapeDtypeStruct((D_MODEL, t), bf16), tok_shape, tok_shape, tok_shape, tok_shape],
        compiler_params=_params(("arbitrary",)),
        name="peer_topk",
    )(x, gamma.reshape(1, D_MODEL), sc, sh, wq_t, keys, a1, a2, a1.T.astype(bf16))


def _peer_kernel(hmt_ref, u_ref, vt_ref, r2_ref, e2_ref, lc_ref, g1_ref, x_ref, g_ref, fg_ref, o_ref,
                 yt, actt, *, nblk, final):
    j = pl.program_id(1)
    nj = pl.num_programs(1)

    @pl.when(j == 0)
    def _():
        yt[...] = jnp.zeros(yt.shape, f32)

    ht = _mm(u_ref[...], hmt_ref[...])
    for a in range(nblk):
        n1 = j * nblk + a
        w = None
        for h in range(PEER_HEADS):
            lrow = lc_ref[h, pl.ds(n1, 1), :]
            grow = g1_ref[h, pl.ds(n1, 1), :]
            wh = jnp.where(r2_ref[h] < lrow, e2_ref[h] * grow, 0.0)
            w = wh if w is None else w + wh
        hb = ht[a * PEER_NKEYS:(a + 1) * PEER_NKEYS, :]
        actt[a * PEER_NKEYS:(a + 1) * PEER_NKEYS, :] = (_gelu(hb) * w).astype(bf16)
    yt[...] += _mm(vt_ref[...], actt[...])

    @pl.when(j == nj - 1)
    def _():
        xn = x_ref[...] + g_ref[0] * yt[...].T
        if final:
            xn = _rms(xn) * fg_ref[...]
        o_ref[...] = xn


def peer_dense(hmt, u, vt, r2, e2, lc, g1, x, gate, mod_map, final_g, tt, eb):
    t = x.shape[0]
    r = gate.shape[1]
    nblk = eb // PEER_NKEYS
    final = final_g is not None
    fg = (final_g if final else jnp.ones((D_MODEL,), f32)).reshape(1, D_MODEL)
    tok = pl.BlockSpec((PEER_HEADS, PEER_NKEYS, tt), lambda i, j: (0, 0, i))
    return pl.pallas_call(
        functools.partial(_peer_kernel, nblk=nblk, final=final),
        grid=(t // tt, N_EXPERTS // eb),
        in_specs=[
            pl.BlockSpec((D_MODEL, tt), lambda i, j: (0, i)),
            pl.BlockSpec((eb, D_MODEL), lambda i, j: (j, 0)),
            pl.BlockSpec((D_MODEL, eb), lambda i, j: (0, j)),
            tok, tok, tok, tok,
            pl.BlockSpec((tt, D_MODEL), lambda i, j: (i, 0)),
            pl.BlockSpec((1, r, D_MODEL), lambda i, j: mod_map(i)),
            pl.BlockSpec((1, D_MODEL), lambda i, j: (0, 0)),
        ],
        out_specs=pl.BlockSpec((tt, D_MODEL), lambda i, j: (i, 0)),
        out_shape=jax.ShapeDtypeStruct((t, D_MODEL), f32),
        scratch_shapes=[pltpu.VMEM((D_MODEL, tt), f32), pltpu.VMEM((eb, tt), bf16)],
        compiler_params=_params(("arbitrary", "arbitrary")),
        name="peer_dense",
    )(hmt, u, vt, r2, e2, lc, g1, x, gate, fg)


def _pad_cols(w, mult):
    n = w.shape[1]
    return jnp.pad(w, ((0, 0), (0, -n % mult)))


def _prepare(ab_w_in, ab_conv_w, ab_conv_b, gdn_a_log, gdn_dt_bias, gdn_norm_g, ssm_a_log, ssm_dt_bias, ssm_d,
             ssm_norm_g, ab_w_out, ret_w_in, ret_norm_g, ret_w_out, peer_w_q, peer_keys, peer_u, peer_v):
    zeros8 = jnp.zeros((GDN_HEADS,), f32)
    return dict(
        ab_w_main=_pad_cols(ab_w_in[0][:, :AB_MAIN], 512).astype(bf16),
        ab_w_small_t=ab_w_in[0][:, AB_MAIN:].T.astype(bf16),
        conv_w=ab_conv_w[0],
        conv_b=ab_conv_b[0].reshape(1, CONV_CH),
        pcol=jnp.stack([jnp.concatenate([zeros8, gdn_dt_bias[0], ssm_dt_bias[0]]),
                        jnp.concatenate([zeros8, gdn_a_log[0], ssm_a_log[0]])], axis=1),
        gdn_norm_g=gdn_norm_g[0].reshape(1, GDN_DV),
        dskip=jnp.repeat(ssm_d[0], SSM_P).reshape(1, SSM_INNER),
        ssm_norm_g=ssm_norm_g[0].reshape(1, SSM_INNER),
        ab_w_out=ab_w_out[0].astype(bf16),
        ret_w_in=ret_w_in[0].astype(bf16),
        ret_norm_g=ret_norm_g[0],
        ret_w_out=ret_w_out[0].astype(bf16),
        wq_t=[peer_w_q[i].T.astype(bf16) for i in range(DEPTH)],
        keys=[peer_keys[i].astype(bf16) for i in range(DEPTH)],
        u=[peer_u[i].astype(bf16) for i in range(DEPTH)],
        vt=[peer_v[i].T.astype(bf16) for i in range(DEPTH)],
    )


def _rope_tables(pos0, length):
    inv = ROPE_BASE ** (-jnp.arange(0, RET_DK, 2, dtype=f32) / RET_DK)
    ang = (pos0 + jnp.arange(length, dtype=f32))[:, None] * inv[None, :]
    return jnp.cos(ang), jnp.sin(ang)


def _trunk(x, mods, pos0, conv0, gdn0, ssm0, ret0, p, norm1_g, norm2_g, final_g):
    b, l, _ = x.shape
    t = b * l
    c = CHUNK if l >= CHUNK else 8
    xt = x.reshape(t, D_MODEL)
    if l > 1:
        tm = min(512, l)
        tt = min(256, l)
        mod_arr = lambda m: m.reshape(b, 1, D_MODEL)
        mod_map = lambda rows: (lambda i: (i * rows // l, 0, 0))
    else:
        tm = t
        tt = t
        mod_arr = lambda m: m.reshape(1, t, D_MODEL)
        mod_map = lambda rows: (lambda i: (0, i, 0))
    convs = gdns = ssms = rets = None
    for layer in range(DEPTH):
        sh1, sc1, g1, sh2, sc2, g2 = (mod_arr(m) for m in jnp.split(mods[layer], 6, axis=-1))
        if layer == 0:
            proj, smt = norm_proj(xt, norm1_g[layer], sc1, sh1, mod_map(tm), p["ab_w_main"], p["ab_w_small_t"], tm, 512)
            tl = min(c, l)
            smt = smt.reshape(AB_SMALL, b, l // tl, tl).transpose(1, 2, 0, 3)
            o, convs, gdns, ssms = mixer_ab(proj.reshape(b, l, -1), smt, conv0, gdn0, ssm0, p["conv_w"], p["conv_b"],
                                            p["pcol"], p["gdn_norm_g"], p["dskip"], p["ssm_norm_g"], c)
            xt = out_proj(o.reshape(t, -1), p["ab_w_out"], xt, g1, mod_map(tm), tm)
        else:
            proj = norm_proj(xt, norm1_g[layer], sc1, sh1, mod_map(tm), p["ret_w_in"], None, tm, 512)
            cos, sin = _rope_tables(pos0, l)
            o, rets = mixer_ret(proj.reshape(b, l, -1), cos, sin, ret0, p["ret_norm_g"], c)
            xt = out_proj(o.reshape(t, -1), p["ret_w_out"], xt, g1, mod_map(tm), tm)
        hmt, r2, e2, lc, gg = peer_topk(xt, norm2_g[layer], sc2, sh2, mod_map(tt), p["wq_t"][layer], p["keys"][layer], tt)
        xt = peer_dense(hmt, p["u"][layer], p["vt"][layer], r2, e2, lc, gg, xt, g2, mod_map(tt),
                        final_g if layer == DEPTH - 1 else None, tt, 512)
    return xt.reshape(b, l, D_MODEL), convs[None], gdns[None], ssms[None], rets[None]


def kernel(x_prompt, x_sample, c_prompt, c_sample, state_conv, state_gdn, state_ssm, state_ret, ada_w, ada_b,
           norm1_g, norm2_g, ab_w_in, ab_conv_w, ab_conv_b, gdn_a_log, gdn_dt_bias, gdn_norm_g, ssm_a_log,
           ssm_dt_bias, ssm_d, ssm_norm_g, ab_w_out, ret_w_in, ret_norm_g, ret_w_out, peer_w_q, peer_keys,
           peer_u, peer_v, final_g):
    p = _prepare(ab_w_in, ab_conv_w, ab_conv_b, gdn_a_log, gdn_dt_bias, gdn_norm_g, ssm_a_log, ssm_dt_bias, ssm_d,
                 ssm_norm_g, ab_w_out, ret_w_in, ret_norm_g, ret_w_out, peer_w_q, peer_keys, peer_u, peer_v)
    nb = x_prompt.shape[0]
    mods = ada_mod(jnp.concatenate([c_prompt, c_sample], axis=0), ada_w, ada_b)
    zeros = lambda s: jnp.zeros((nb,) + s.shape[2:], s.dtype)
    y_p, p_conv, p_gdn, p_ssm, p_ret = _trunk(
        x_prompt, mods[:, :nb], 0, zeros(state_conv), zeros(state_gdn), zeros(state_ssm), zeros(state_ret),
        p, norm1_g, norm2_g, final_g)
    y_s, s_conv, s_gdn, s_ssm, s_ret = _trunk(
        x_sample, mods[:, nb:], PAST_LEN, state_conv[0], state_gdn[0], state_ssm[0], state_ret[0],
        p, norm1_g, norm2_g, final_g)
    return (y_p, y_s, p_conv, p_gdn, p_ssm, p_ret, s_conv, s_gdn, s_ssm, s_ret)
```

```python
import functools
import math

import jax
import jax.numpy as jnp
from jax import lax
from jax.experimental import pallas as pl
from jax.experimental.pallas import tpu as pltpu

f32 = jnp.float32
bf16 = jnp.bfloat16
HI = lax.Precision.HIGHEST

D_MODEL = 1024
DEPTH = 2
PAST_LEN = 16384
GDN_HEADS = 8
GDN_DK = 128
GDN_DV = 128
SSM_HEADS = 16
SSM_P = 64
SSM_N = 128
SSM_G = 2
CONV_W = 4
RET_HEADS = 4
RET_DK = 256
RET_DV = 512
ROPE_BASE = 10000.0
PEER_HEADS = 8
PEER_NKEYS = 128
PEER_TOPK = 16
PEER_DQ = 256
CHUNK = 64
EPS = 1e-6

GDN_QK = GDN_HEADS * GDN_DK
GDN_V = GDN_HEADS * GDN_DV
SSM_INNER = SSM_HEADS * SSM_P
SSM_BC = SSM_G * SSM_N
CONV_CH = 2 * GDN_QK + GDN_V + SSM_INNER + 2 * SSM_BC
AB_MAIN = CONV_CH + GDN_V + SSM_INNER
AB_SMALL = 2 * GDN_HEADS + SSM_HEADS
RET_QK = RET_HEADS * RET_DK
RET_V = RET_HEADS * RET_DV
RET_IN = 2 * RET_QK + 2 * RET_V
N_EXPERTS = PEER_NKEYS * PEER_NKEYS

VMEM_LIMIT = 56 * 1024 * 1024
PEER_SB = 256
PEER_EB = 4096

_CAND = [(a, b) for a in range(PEER_TOPK) for b in range(PEER_TOPK) if (a + 1) * (b + 1) <= PEER_TOPK]
_NCAND = len(_CAND)
_NCAND_PAD = -(-_NCAND // 16) * 16


def _params(sem):
    return pltpu.CompilerParams(dimension_semantics=sem, vmem_limit_bytes=VMEM_LIMIT)


def _nt(a, b, **kw):
    return lax.dot_general(a, b, (((1,), (1,)), ((), ())), preferred_element_type=f32, **kw)


def _tn(a, b, **kw):
    return lax.dot_general(a, b, (((0,), (0,)), ((), ())), preferred_element_type=f32, **kw)


def _mm(a, b, **kw):
    return jnp.dot(a, b, preferred_element_type=f32, **kw)


def _bmm(a, b, **kw):
    return jnp.einsum('hik,hkj->hij', a, b, preferred_element_type=f32, **kw)


def _bnt(a, b, **kw):
    return jnp.einsum('hik,hjk->hij', a, b, preferred_element_type=f32, **kw)


def _btn(a, b, **kw):
    return jnp.einsum('hki,hkj->hij', a, b, preferred_element_type=f32, **kw)


def _split(x):
    hi = x.astype(bf16)
    return hi, (x - hi.astype(f32)).astype(bf16)


def _bmm3(a, b):
    ah, al = _split(a)
    bh, bl = _split(b)
    return _bmm(ah, bh) + (_bmm(ah, bl) + _bmm(al, bh))


def _silu(x):
    return x * jax.nn.sigmoid(x)


def _gelu(x):
    return 0.5 * x * (1.0 + lax.erf(x * (2.0 ** -0.5)))


def _rms(x):
    return x * lax.rsqrt(jnp.mean(x * x, axis=-1, keepdims=True) + EPS)


def _mod_kernel(c_ref, w_ref, b_ref, o_ref):
    a = _silu(c_ref[...]).astype(bf16)
    o_ref[0] = _mm(a, w_ref[0].astype(bf16)) + b_ref[0]


def ada_mod(c_all, ada_w, ada_b):
    m = c_all.shape[0]
    tn = 768
    n = ada_w.shape[-1]
    return pl.pallas_call(
        _mod_kernel,
        grid=(DEPTH, n // tn),
        in_specs=[
            pl.BlockSpec((m, D_MODEL), lambda l, j: (0, 0)),
            pl.BlockSpec((1, D_MODEL, tn), lambda l, j: (l, 0, j)),
            pl.BlockSpec((1, 1, tn), lambda l, j: (l, 0, j)),
        ],
        out_specs=pl.BlockSpec((1, m, tn), lambda l, j: (l, 0, j)),
        out_shape=jax.ShapeDtypeStruct((DEPTH, m, n), f32),
        compiler_params=_params(("arbitrary", "arbitrary")),
        name="ada_mod",
    )(c_all, ada_w, ada_b.reshape(DEPTH, 1, n))


def _proj_kernel(x_ref, g_ref, sc_ref, sh_ref, w_ref, *rest, has_small):
    if has_small:
        ws_ref, o_ref, os_ref, hm_ref = rest
    else:
        o_ref, hm_ref = rest
    j = pl.program_id(1)

    @pl.when(j == 0)
    def _():
        y = _rms(x_ref[...]) * g_ref[...]
        hm = (y * (1.0 + sc_ref[0]) + sh_ref[0]).astype(bf16)
        hm_ref[...] = hm
        if has_small:
            os_ref[...] = _nt(ws_ref[...], hm)

    o_ref[...] = _mm(hm_ref[...], w_ref[...])


def norm_proj(x, gamma, sc, sh, mod_map, w, ws_t, tm, tn):
    t = x.shape[0]
    n = w.shape[1]
    r = sc.shape[1]
    has_small = ws_t is not None
    in_specs = [
        pl.BlockSpec((tm, D_MODEL), lambda i, j: (i, 0)),
        pl.BlockSpec((1, D_MODEL), lambda i, j: (0, 0)),
        pl.BlockSpec((1, r, D_MODEL), lambda i, j: mod_map(i)),
        pl.BlockSpec((1, r, D_MODEL), lambda i, j: mod_map(i)),
        pl.BlockSpec((D_MODEL, tn), lambda i, j: (0, j)),
    ]
    out_specs = [pl.BlockSpec((tm, tn), lambda i, j: (i, j))]
    out_shape = [jax.ShapeDtypeStruct((t, n), f32)]
    args = [x, gamma.reshape(1, D_MODEL), sc, sh, w]
    if has_small:
        s = ws_t.shape[0]
        in_specs.append(pl.BlockSpec((s, D_MODEL), lambda i, j: (0, 0)))
        out_specs.append(pl.BlockSpec((s, tm), lambda i, j: (0, i)))
        out_shape.append(jax.ShapeDtypeStruct((s, t), f32))
        args.append(ws_t)
    res = pl.pallas_call(
        functools.partial(_proj_kernel, has_small=has_small),
        grid=(t // tm, n // tn),
        in_specs=in_specs,
        out_specs=out_specs,
        out_shape=out_shape,
        scratch_shapes=[pltpu.VMEM((tm, D_MODEL), bf16)],
        compiler_params=_params(("arbitrary", "arbitrary")),
        name="norm_proj",
    )(*args)
    return res if has_small else res[0]


def _out_kernel(o_ref, w_ref, x_ref, g_ref, y_ref):
    y_ref[...] = x_ref[...] + g_ref[0] * _mm(o_ref[...], w_ref[...])


def out_proj(o, w, x, gate, mod_map, tm):
    t, k = o.shape
    r = gate.shape[1]
    return pl.pallas_call(
        _out_kernel,
        grid=(t // tm,),
        in_specs=[
            pl.BlockSpec((tm, k), lambda i: (i, 0)),
            pl.BlockSpec((k, D_MODEL), lambda i: (0, 0)),
            pl.BlockSpec((tm, D_MODEL), lambda i: (i, 0)),
            pl.BlockSpec((1, r, D_MODEL), lambda i: mod_map(i)),
        ],
        out_specs=pl.BlockSpec((tm, D_MODEL), lambda i: (i, 0)),
        out_shape=jax.ShapeDtypeStruct((t, D_MODEL), f32),
        compiler_params=_params(("arbitrary",)),
        name="out_proj",
    )(o, w, x, gate)


def _unit_lower_inverse(lm, c):
    eye = (lax.broadcasted_iota(jnp.int32, (c, c), 0) == lax.broadcasted_iota(jnp.int32, (c, c), 1)).astype(f32)
    p = eye - lm
    lp = lm
    k = 2
    while k < c:
        lp = _bmm3(lp, lp)
        p = p + _bmm3(p, lp)
        k *= 2
    return p


def _ab_kernel(proj_ref, smt_ref, conv0_ref, gdn0_ref, ssm0_ref, convw_ref, convb_ref, pcol_ref, gng_ref,
               dskip_ref, sng_ref, o_ref, convn_ref, gdnn_ref, ssmn_ref, xbuf, act, sg, ss, ob, *, c, lv):
    l = pl.program_id(1)
    nl = pl.num_programs(1)

    @pl.when(l == 0)
    def _():
        xbuf[...] = jnp.zeros(xbuf.shape, f32)
        xbuf[5:8, :] = conv0_ref[0]
        sg[...] = gdn0_ref[0]
        ss[...] = ssm0_ref[0]

    xbuf[8:8 + lv, :] = proj_ref[0, :, 0:CONV_CH]
    y = convb_ref[...]
    for i in range(CONV_W):
        y = y + convw_ref[i:i + 1, :] * xbuf[5 + i:5 + i + c, :]
    act[...] = _silu(y)
    tail = xbuf[5 + lv:8 + lv, :]
    xbuf[5:8, :] = tail

    if lv == c:
        sm = smt_ref[0, 0]
    else:
        lane = lax.broadcasted_iota(jnp.int32, (AB_SMALL, c), 1)
        sm = jnp.where(lane < lv, jnp.broadcast_to(smt_ref[0, 0], (AB_SMALL, c)), 0.0)
    valid = lax.broadcasted_iota(jnp.int32, (AB_SMALL, c), 1) < lv
    bias_col = pcol_ref[:, 0:1]
    alog_col = pcol_ref[:, 1:2]
    beta_t = jnp.where(valid, jax.nn.sigmoid(sm), 0.0)
    sp_t = jnp.where(valid, jax.nn.softplus(sm + bias_col), 0.0)
    la_t = -jnp.exp(alog_col) * sp_t
    rr = lax.broadcasted_iota(jnp.int32, (c, c), 0)
    qq = lax.broadcasted_iota(jnp.int32, (c, c), 1)
    triu = (rr <= qq).astype(f32)
    eye = (rr == qq).astype(f32)
    cum_t = _mm(la_t, triu, precision=HI)
    last_t = cum_t[:, c - 1:c]
    ecum_t = jnp.exp(cum_t)
    elc_t = jnp.exp(last_t - cum_t)
    elast_t = jnp.exp(last_t)
    g0, g1, s0, s1 = 0, GDN_HEADS, 2 * GDN_HEADS, AB_SMALL
    rows = jnp.concatenate([
        beta_t[g0:g1],
        cum_t[g1:s0],
        ecum_t[g1:s0],
        beta_t[g0:g1] * ecum_t[g1:s0],
        elc_t[g1:s0],
        cum_t[s0:s1],
        ecum_t[s0:s1],
        sp_t[s0:s1] * elc_t[s0:s1],
        jnp.zeros((128 - 88, c), f32),
    ], axis=0)
    cols = _nt(eye, rows, precision=HI)
    incl = rr >= qq
    strict = rr > qq

    def colstack(base, n):
        return jnp.stack([cols[:, base + h:base + h + 1] for h in range(n)])

    def rowstack(x, base, n):
        return jnp.stack([x[base + h:base + h + 1, :] for h in range(n)])

    hs = range(GDN_HEADS)
    q = jnp.stack([act[:, h * GDN_DK:(h + 1) * GDN_DK] for h in hs])
    k = jnp.stack([act[:, GDN_QK + h * GDN_DK:GDN_QK + (h + 1) * GDN_DK] for h in hs])
    v = jnp.stack([act[:, 2 * GDN_QK + h * GDN_DV:2 * GDN_QK + (h + 1) * GDN_DV] for h in hs])
    q = q * lax.rsqrt(jnp.sum(q * q, axis=-1, keepdims=True) + EPS) * (GDN_DK ** -0.5)
    k = k * lax.rsqrt(jnp.sum(k * k, axis=-1, keepdims=True) + EPS)
    beta_c, cum_c, ecum_c, becum_c, elc_c = (colstack(b, GDN_HEADS) for b in (0, 8, 16, 24, 32))
    dec = jnp.exp(jnp.where(incl, cum_c - rowstack(cum_t, g1, GDN_HEADS), -jnp.inf))
    kb = k.astype(bf16)
    qb = q.astype(bf16)
    lm = jnp.where(strict, beta_c * _bnt(kb, kb) * dec, 0.0)
    pinv = _unit_lower_inverse(lm, c)
    rhs = jnp.concatenate([v * beta_c, k * becum_c], axis=2)
    sol = _bmm3(pinv, rhs)
    u0 = sol[:, :, :GDN_DV]
    w = sol[:, :, GDN_DV:]
    qk = _bnt(qb, kb) * dec
    s_old = sg[...]
    sb = s_old.astype(bf16)
    u = u0 - _bmm(w.astype(bf16), sb)
    ub = u.astype(bf16)
    o = _bmm((q * ecum_c).astype(bf16), sb) + _bmm(qk.astype(bf16), ub)
    kd = (k * elc_c).astype(bf16)
    sg[...] = s_old * rowstack(elast_t, g1, GDN_HEADS) + _btn(kd, ub)
    oa = _rms(o) * gng_ref[...]
    for h in hs:
        gate = proj_ref[0, :, CONV_CH + h * GDN_DV:CONV_CH + (h + 1) * GDN_DV]
        o_ref[0, :, h * GDN_DV:(h + 1) * GDN_DV] = (oa[h][0:lv] * _silu(gate)).astype(bf16)

    rep = SSM_HEADS // SSM_G
    xs0 = 2 * GDN_QK + GDN_V
    bm0 = xs0 + SSM_INNER
    cm0 = bm0 + SSM_BC
    hs = range(SSM_HEADS)
    bmb = [act[:, bm0 + g * SSM_N:bm0 + (g + 1) * SSM_N].astype(bf16) for g in range(SSM_G)]
    cmb = [act[:, cm0 + g * SSM_N:cm0 + (g + 1) * SSM_N].astype(bf16) for g in range(SSM_G)]
    cbg = [_nt(cmb[g], bmb[g]) for g in range(SSM_G)]
    cb = jnp.stack([cbg[h // rep] for h in hs])
    bm16 = jnp.stack([bmb[h // rep] for h in hs])
    cm16 = jnp.stack([cmb[h // rep] for h in hs])
    xs = jnp.stack([act[:, xs0 + h * SSM_P:xs0 + (h + 1) * SSM_P] for h in hs])
    dsk = jnp.stack([dskip_ref[:, h * SSM_P:(h + 1) * SSM_P] for h in hs])
    cum_c, ecum_c, dtelc_c = (colstack(b, SSM_HEADS) for b in (40, 56, 72))
    m = cb * jnp.exp(jnp.where(incl, cum_c - rowstack(cum_t, s0, SSM_HEADS), -jnp.inf)) * rowstack(sp_t, s0, SSM_HEADS)
    s_old = ss[...]
    o = _bmm(m.astype(bf16), xs.astype(bf16)) + _bmm(cm16, s_old.astype(bf16)) * ecum_c + dsk * xs
    ss[...] = s_old * rowstack(elast_t, s0, SSM_HEADS) + _btn(bm16, (xs * dtelc_c).astype(bf16))
    for h in hs:
        ob[:, h * SSM_P:(h + 1) * SSM_P] = o[h]
    z = proj_ref[0, :, CONV_CH + GDN_V:CONV_CH + GDN_V + SSM_INNER]
    obv = _rms(ob[0:lv, :] * _silu(z)) * sng_ref[...]
    o_ref[0, :, GDN_V:GDN_V + SSM_INNER] = obv.astype(bf16)

    @pl.when(l == nl - 1)
    def _():
        convn_ref[0] = xbuf[5:8, :]
        gdnn_ref[0] = sg[...]
        ssmn_ref[0] = ss[...]


def mixer_ab(proj, smt, conv0, gdn0, ssm0, conv_w, conv_b, pcol, gdn_norm_g, dskip, ssm_norm_g, c):
    b, l, npad = proj.shape
    tl = min(c, l)
    nl = l // tl
    kern = functools.partial(_ab_kernel, c=c, lv=tl)
    full = lambda shape: pl.BlockSpec(shape, lambda i, j: (0,) * len(shape))
    return pl.pallas_call(
        kern,
        grid=(b, nl),
        in_specs=[
            pl.BlockSpec((1, tl, npad), lambda i, j: (i, j, 0)),
            pl.BlockSpec((1, 1, AB_SMALL, tl), lambda i, j: (i, j, 0, 0)),
            pl.BlockSpec((1, CONV_W - 1, CONV_CH), lambda i, j: (i, 0, 0)),
            pl.BlockSpec((1, GDN_HEADS, GDN_DK, GDN_DV), lambda i, j: (i, 0, 0, 0)),
            pl.BlockSpec((1, SSM_HEADS, SSM_N, SSM_P), lambda i, j: (i, 0, 0, 0)),
            full((CONV_W, CONV_CH)),
            full((1, CONV_CH)),
            full((AB_SMALL, 2)),
            full((1, GDN_DV)),
            full((1, SSM_INNER)),
            full((1, SSM_INNER)),
        ],
        out_specs=[
            pl.BlockSpec((1, tl, GDN_V + SSM_INNER), lambda i, j: (i, j, 0)),
            pl.BlockSpec((1, CONV_W - 1, CONV_CH), lambda i, j: (i, 0, 0)),
            pl.BlockSpec((1, GDN_HEADS, GDN_DK, GDN_DV), lambda i, j: (i, 0, 0, 0)),
            pl.BlockSpec((1, SSM_HEADS, SSM_N, SSM_P), lambda i, j: (i, 0, 0, 0)),
        ],
        out_shape=[
            jax.ShapeDtypeStruct((b, l, GDN_V + SSM_INNER), bf16),
            jax.ShapeDtypeStruct((b, CONV_W - 1, CONV_CH), f32),
            jax.ShapeDtypeStruct((b, GDN_HEADS, GDN_DK, GDN_DV), f32),
            jax.ShapeDtypeStruct((b, SSM_HEADS, SSM_N, SSM_P), f32),
        ],
        scratch_shapes=[
            pltpu.VMEM((8 + c, CONV_CH), f32),
            pltpu.VMEM((c, CONV_CH), f32),
            pltpu.VMEM((GDN_HEADS, GDN_DK, GDN_DV), f32),
            pltpu.VMEM((SSM_HEADS, SSM_N, SSM_P), f32),
            pltpu.VMEM((c, SSM_INNER), f32),
        ],
        compiler_params=_params(("arbitrary", "arbitrary")),
        name="mixer_ab",
    )(proj, smt, conv0, gdn0, ssm0, conv_w, conv_b, pcol, gdn_norm_g, dskip, ssm_norm_g)


def _ret_kernel(proj_ref, cos_ref, sin_ref, ret0_ref, ng_ref, o_ref, retn_ref, st, buf, *, c, lv):
    l = pl.program_id(1)
    nl = pl.num_programs(1)

    @pl.when(l == 0)
    def _():
        st[...] = ret0_ref[0]
        if lv < c:
            buf[...] = jnp.zeros(buf.shape, f32)

    if lv == c:
        src = proj_ref.at[0]
        cos = cos_ref[...]
        sin = sin_ref[...]
    else:
        buf[0:lv, :] = proj_ref[0]
        src = buf
        cos = jnp.broadcast_to(cos_ref[...], (c, RET_DK // 2))
        sin = jnp.broadcast_to(sin_ref[...], (c, RET_DK // 2))

    ri = lax.broadcasted_iota(jnp.int32, (c, c), 0)
    ci = lax.broadcasted_iota(jnp.int32, (c, c), 1)
    incl = ri >= ci
    cnt_r = jnp.minimum(ri + 1, lv).astype(f32)
    cnt_c = jnp.minimum(ci + 1, lv).astype(f32)
    cnt_col = jnp.minimum(lax.broadcasted_iota(jnp.int32, (c, 1), 0) + 1, lv).astype(f32)
    half = RET_DK // 2

    def rope(x):
        x1, x2 = x[:, :half], x[:, half:]
        return jnp.concatenate([x1 * cos - x2 * sin, x1 * sin + x2 * cos], axis=1)

    for h in range(RET_HEADS):
        lg = math.log(1.0 - 2.0 ** (-5.0 - h))
        q = rope(src[:, h * RET_DK:(h + 1) * RET_DK])
        k = rope(src[:, RET_QK + h * RET_DK:RET_QK + (h + 1) * RET_DK]) * (RET_DK ** -0.5)
        v = src[:, 2 * RET_QK + h * RET_DV:2 * RET_QK + (h + 1) * RET_DV].astype(bf16)
        gate = src[:, 2 * RET_QK + RET_V + h * RET_DV:2 * RET_QK + RET_V + (h + 1) * RET_DV]
        dec = jnp.exp(jnp.where(incl, (cnt_r - cnt_c) * lg, -jnp.inf))
        scores = _nt(q.astype(bf16), k.astype(bf16)) * dec
        s_old = st[h]
        o = _mm(scores.astype(bf16), v) + _mm((q * jnp.exp(cnt_col * lg)).astype(bf16), s_old.astype(bf16))
        kd = (k * jnp.exp((lv - cnt_col) * lg)).astype(bf16)
        st[h] = s_old * math.exp(lv * lg) + _tn(kd, v)
        mu = jnp.mean(o, axis=-1, keepdims=True)
        var = jnp.mean(jnp.square(o - mu), axis=-1, keepdims=True)
        o = (o - mu) * lax.rsqrt(var + EPS) * ng_ref[h:h + 1, :]
        o_ref[0, :, h * RET_DV:(h + 1) * RET_DV] = (_silu(gate) * o)[0:lv].astype(bf16)

    @pl.when(l == nl - 1)
    def _():
        retn_ref[0] = st[...]


def mixer_ret(proj, cos, sin, ret0, norm_g, c):
    b, l, n = proj.shape
    tl = min(c, l)
    nl = l // tl
    kern = functools.partial(_ret_kernel, c=c, lv=tl)
    return pl.pallas_call(
        kern,
        grid=(b, nl),
        in_specs=[
            pl.BlockSpec((1, tl, n), lambda i, j: (i, j, 0)),
            pl.BlockSpec((tl, RET_DK // 2), lambda i, j: (j, 0)),
            pl.BlockSpec((tl, RET_DK // 2), lambda i, j: (j, 0)),
            pl.BlockSpec((1, RET_HEADS, RET_DK, RET_DV), lambda i, j: (i, 0, 0, 0)),
            pl.BlockSpec((RET_HEADS, RET_DV), lambda i, j: (0, 0)),
        ],
        out_specs=[
            pl.BlockSpec((1, tl, RET_V), lambda i, j: (i, j, 0)),
            pl.BlockSpec((1, RET_HEADS, RET_DK, RET_DV), lambda i, j: (i, 0, 0, 0)),
        ],
        out_shape=[
            jax.ShapeDtypeStruct((b, l, RET_V), bf16),
            jax.ShapeDtypeStruct((b, RET_HEADS, RET_DK, RET_DV), f32),
        ],
        scratch_shapes=[
            pltpu.VMEM((RET_HEADS, RET_DK, RET_DV), f32),
            pltpu.VMEM((c, n), f32),
        ],
        compiler_params=_params(("arbitrary", "arbitrary")),
        name="mixer_ret",
    )(proj, cos, sin, ret0, norm_g)


def _top16(s, exact):
    g, n, t = s.shape
    rows = lax.broadcasted_iota(jnp.int32, (g, n, t), 1)
    krow = lax.broadcasted_iota(jnp.int32, (g, PEER_TOPK, t), 1)
    rank = jnp.full((g, n, t), float(PEER_TOPK), f32)
    sv = jnp.zeros((g, PEER_TOPK, t), f32)
    work = s
    for k in range(PEER_TOPK):
        m = jnp.max(work, axis=1, keepdims=True)
        if exact:
            idx = jnp.min(jnp.where(work == m, rows, n), axis=1, keepdims=True)
            sel = rows == idx
        else:
            sel = work == m
        rank = jnp.where(sel, float(k), rank)
        work = jnp.where(sel, -jnp.inf, work)
        sv = jnp.where(krow == k, m, sv)
    return sv, rank


def _miscount(rank):
    cnt = jnp.sum(jnp.where(rank < float(PEER_TOPK), 1.0, 0.0), axis=1, keepdims=True)
    return jnp.max(jnp.abs(cnt - float(PEER_TOPK)))


def _topk_kernel(x_ref, g_ref, sc_ref, sh_ref, wqt_ref, keys_ref, a1_ref, a2_ref, a1t_ref,
                 hmt_ref, r2_ref, e2_ref, lc_ref, g1_ref, sv_scr, rk_scr, rc_scr):
    y = _rms(x_ref[...]) * g_ref[...]
    hm = y * (1.0 + sc_ref[0]) + sh_ref[0]
    hmt = hm.T.astype(bf16)
    hmt_ref[...] = hmt
    qt = _mm(wqt_ref[...], hmt)
    tt = qt.shape[1]
    half = PEER_DQ // 2
    hs = range(PEER_HEADS)
    s_all = jnp.stack(
        [_mm(keys_ref[0], qt[h * PEER_DQ:h * PEER_DQ + half, :].astype(bf16)) for h in hs]
        + [_mm(keys_ref[1], qt[h * PEER_DQ + half:(h + 1) * PEER_DQ, :].astype(bf16)) for h in hs])
    crow = lax.broadcasted_iota(jnp.int32, (PEER_HEADS, _NCAND_PAD, tt), 1)

    def candidates(sv):
        cand = jnp.stack([_mm(a1_ref[...], sv[h], precision=HI) + _mm(a2_ref[...], sv[PEER_HEADS + h], precision=HI)
                          for h in hs])
        return jnp.where(crow < _NCAND, cand, -jnp.inf)

    def select(exact):
        sv, rank = _top16(s_all, exact)
        _, rankc = _top16(candidates(sv), exact)
        sv_scr[...] = sv
        rk_scr[...] = rank
        rc_scr[...] = rankc
        return jnp.maximum(_miscount(rank), _miscount(rankc))

    bad = select(False)

    @pl.when(bad > 0.0)
    def _():
        select(True)

    sv = sv_scr[...]
    sv1, sv2 = sv[:PEER_HEADS], sv[PEER_HEADS:]
    rank1 = rk_scr[0:PEER_HEADS]
    sel_all = jnp.where(rc_scr[...] < float(PEER_TOPK), 1.0, 0.0)
    cand = candidates(sv)
    z = jnp.sum(sel_all * jnp.exp(jnp.where(crow < _NCAND, cand - cand[:, 0:1, :], 0.0)), axis=1, keepdims=True)
    selb = sel_all.astype(bf16)
    cnt = jnp.stack([_mm(a1t_ref[...], selb[h]) for h in hs])
    lc = jnp.zeros((PEER_HEADS, PEER_NKEYS, tt), f32)
    for k1 in range(PEER_TOPK):
        lc = jnp.where(rank1 == float(k1), cnt[:, k1:k1 + 1, :], lc)
    r2_ref[...] = rk_scr[PEER_HEADS:2 * PEER_HEADS]
    e2_ref[...] = jnp.exp(s_all[PEER_HEADS:] - sv2[:, 0:1, :])
    lc_ref[...] = lc
    g1_ref[...] = jnp.exp(s_all[:PEER_HEADS] - sv1[:, 0:1, :]) / z


def peer_topk(x, gamma, sc, sh, mod_map, wq_t, keys, tt):
    t = x.shape[0]
    r = sc.shape[1]
    a1 = jnp.zeros((_NCAND_PAD, PEER_TOPK), f32).at[jnp.arange(_NCAND), jnp.array([a for a, _ in _CAND])].set(1.0)
    a2 = jnp.zeros((_NCAND_PAD, PEER_TOPK), f32).at[jnp.arange(_NCAND), jnp.array([b for _, b in _CAND])].set(1.0)
    full = lambda shape: pl.BlockSpec(shape, lambda i: (0,) * len(shape))
    tok = pl.BlockSpec((PEER_HEADS, PEER_NKEYS, tt), lambda i: (0, 0, i))
    tok_shape = jax.ShapeDtypeStruct((PEER_HEADS, PEER_NKEYS, t), f32)
    return pl.pallas_call(
        _topk_kernel,
        grid=(t // tt,),
        in_specs=[
            pl.BlockSpec((tt, D_MODEL), lambda i: (i, 0)),
            full((1, D_MODEL)),
            pl.BlockSpec((1, r, D_MODEL), lambda i: mod_map(i)),
            pl.BlockSpec((1, r, D_MODEL), lambda i: mod_map(i)),
            full((PEER_HEADS * PEER_DQ, D_MODEL)),
            full((2, PEER_NKEYS, PEER_DQ // 2)),
            full((_NCAND_PAD, PEER_TOPK)),
            full((_NCAND_PAD, PEER_TOPK)),
            full((PEER_TOPK, _NCAND_PAD)),
        ],
        out_specs=[pl.BlockSpec((D_MODEL, tt), lambda i: (0, i)), tok, tok, tok, tok],
        out_shape=[jax.ShapeDtypeStruct((D_MODEL, t), bf16), tok_shape, tok_shape, tok_shape, tok_shape],
        scratch_shapes=[
            pltpu.VMEM((2 * PEER_HEADS, PEER_TOPK, tt), f32),
            pltpu.VMEM((2 * PEER_HEADS, PEER_NKEYS, tt), f32),
            pltpu.VMEM((PEER_HEADS, _NCAND_PAD, tt), f32),
        ],
        compiler_params=_params(("arbitrary",)),
        name="peer_topk",
    )(x, gamma.reshape(1, D_MODEL), sc, sh, wq_t, keys, a1, a2, a1.T.astype(bf16))


def _peer_kernel(hmt_ref, u_ref, vt_ref, r2_ref, e2_ref, lc_ref, g1_ref, x_ref, g_ref, fg_ref, o_ref,
                 yt, act_a, act_b, ht_a, ht_b, *, nsub, final):
    j = pl.program_id(1)
    nj = pl.num_programs(1)
    na = PEER_SB // PEER_NKEYS
    tt = hmt_ref.shape[1]

    @pl.when(j == 0)
    def _():
        yt[...] = jnp.zeros(yt.shape, f32)

    def mm1(s, ht):
        ht[...] = _mm(u_ref[s], hmt_ref[...])

    def gate_act(s, ht, act):
        for a in range(na):
            n1 = (j * nsub + s) * na + a
            lrows = [lc_ref[h, pl.ds(n1, 1), :] for h in range(PEER_HEADS)]
            grows = [g1_ref[h, pl.ds(n1, 1), :] for h in range(PEER_HEADS)]
            for lg in range(tt // 128):
                ls = slice(lg * 128, (lg + 1) * 128)
                w = None
                for h in range(PEER_HEADS):
                    wh = jnp.where(r2_ref[h, :, ls] < lrows[h][:, ls], e2_ref[h, :, ls] * grows[h][:, ls], 0.0)
                    w = wh if w is None else w + wh
                hb = ht[a * PEER_NKEYS:(a + 1) * PEER_NKEYS, ls]
                act[a * PEER_NKEYS:(a + 1) * PEER_NKEYS, ls] = (_gelu(hb) * w).astype(bf16)

    def mm2(s, act):
        yt[...] += _mm(vt_ref[s], act[...])

    act_b[...] = jnp.zeros(act_b.shape, bf16)
    mm1(0, ht_a)

    def body(i, carry):
        s = 2 * i
        mm1(s + 1, ht_b)
        gate_act(s, ht_a, act_a)
        mm2(jnp.maximum(s - 1, 0), act_b)
        mm1(jnp.minimum(s + 2, nsub - 1), ht_a)
        gate_act(s + 1, ht_b, act_b)
        mm2(s, act_a)
        return carry

    lax.fori_loop(0, nsub // 2, body, 0)
    mm2(nsub - 1, act_b)

    @pl.when(j == nj - 1)
    def _():
        xn = x_ref[...] + g_ref[0] * yt[...].T
        if final:
            xn = _rms(xn) * fg_ref[...]
        o_ref[...] = xn


def peer_dense(hmt, u3, vt3, r2, e2, lc, g1, x, gate, mod_map, final_g, tt, eb):
    t = x.shape[0]
    r = gate.shape[1]
    nsub = eb // PEER_SB
    final = final_g is not None
    fg = (final_g if final else jnp.ones((D_MODEL,), f32)).reshape(1, D_MODEL)
    tok = pl.BlockSpec((PEER_HEADS, PEER_NKEYS, tt), lambda i, j: (0, 0, i))
    return pl.pallas_call(
        functools.partial(_peer_kernel, nsub=nsub, final=final),
        grid=(t // tt, N_EXPERTS // eb),
        in_specs=[
            pl.BlockSpec((D_MODEL, tt), lambda i, j: (0, i)),
            pl.BlockSpec((nsub, PEER_SB, D_MODEL), lambda i, j: (j, 0, 0)),
            pl.BlockSpec((nsub, D_MODEL, PEER_SB), lambda i, j: (j, 0, 0)),
            tok, tok, tok, tok,
            pl.BlockSpec((tt, D_MODEL), lambda i, j: (i, 0)),
            pl.BlockSpec((1, r, D_MODEL), lambda i, j: mod_map(i)),
            pl.BlockSpec((1, D_MODEL), lambda i, j: (0, 0)),
        ],
        out_specs=pl.BlockSpec((tt, D_MODEL), lambda i, j: (i, 0)),
        out_shape=jax.ShapeDtypeStruct((t, D_MODEL), f32),
        scratch_shapes=[
            pltpu.VMEM((D_MODEL, tt), f32),
            pltpu.VMEM((PEER_SB, tt), bf16), pltpu.VMEM((PEER_SB, tt), bf16),
            pltpu.VMEM((PEER_SB, tt), f32), pltpu.VMEM((PEER_SB, tt), f32),
        ],
        compiler_params=_params(("arbitrary", "arbitrary")),
        name="peer_dense",
    )(hmt, u3, vt3, r2, e2, lc, g1, x, gate, fg)


def _prepare(ab_w_in, ab_conv_w, ab_conv_b, gdn_a_log, gdn_dt_bias, gdn_norm_g, ssm_a_log, ssm_dt_bias, ssm_d,
             ssm_norm_g, ab_w_out, ret_w_in, ret_norm_g, ret_w_out, peer_w_q, peer_keys, peer_u, peer_v):
    zeros8 = jnp.zeros((GDN_HEADS,), f32)
    nsb = N_EXPERTS // PEER_SB
    return dict(
        ab_w_main=ab_w_in[0][:, :AB_MAIN].astype(bf16),
        ab_w_small_t=ab_w_in[0][:, AB_MAIN:].T.astype(bf16),
        conv_w=ab_conv_w[0],
        conv_b=ab_conv_b[0].reshape(1, CONV_CH),
        pcol=jnp.stack([jnp.concatenate([zeros8, gdn_dt_bias[0], ssm_dt_bias[0]]),
                        jnp.concatenate([zeros8, gdn_a_log[0], ssm_a_log[0]])], axis=1),
        gdn_norm_g=gdn_norm_g[0].reshape(1, GDN_DV),
        dskip=jnp.repeat(ssm_d[0], SSM_P).reshape(1, SSM_INNER),
        ssm_norm_g=ssm_norm_g[0].reshape(1, SSM_INNER),
        ab_w_out=ab_w_out[0].astype(bf16),
        ret_w_in=ret_w_in[0].astype(bf16),
        ret_norm_g=ret_norm_g[0],
        ret_w_out=ret_w_out[0].astype(bf16),
        wq_t=[peer_w_q[i].T.astype(bf16) for i in range(DEPTH)],
        keys=[peer_keys[i].astype(bf16) for i in range(DEPTH)],
        u=[peer_u[i].astype(bf16).reshape(nsb, PEER_SB, D_MODEL) for i in range(DEPTH)],
        vt=[peer_v[i].astype(bf16).reshape(nsb, PEER_SB, D_MODEL).transpose(0, 2, 1) for i in range(DEPTH)],
    )


def _rope_tables(pos0, length):
    inv = ROPE_BASE ** (-jnp.arange(0, RET_DK, 2, dtype=f32) / RET_DK)
    ang = (pos0 + jnp.arange(length, dtype=f32))[:, None] * inv[None, :]
    return jnp.cos(ang), jnp.sin(ang)


def _trunk(x, mods, pos0, conv0, gdn0, ssm0, ret0, p, norm1_g, norm2_g, final_g):
    b, l, _ = x.shape
    t = b * l
    c = CHUNK if l >= CHUNK else 8
    xt = x.reshape(t, D_MODEL)
    if l > 1:
        tm = min(512, l)
        tt = min(256, l)
        mod_arr = lambda m: m.reshape(b, 1, D_MODEL)
        mod_map = lambda rows: (lambda i: (i * rows // l, 0, 0))
    else:
        tm = t
        tt = t
        mod_arr = lambda m: m.reshape(1, t, D_MODEL)
        mod_map = lambda rows: (lambda i: (0, i, 0))
    convs = gdns = ssms = rets = None
    for layer in range(DEPTH):
        sh1, sc1, g1, sh2, sc2, g2 = (mod_arr(m) for m in jnp.split(mods[layer], 6, axis=-1))
        if layer == 0:
            proj, smt = norm_proj(xt, norm1_g[layer], sc1, sh1, mod_map(tm), p["ab_w_main"], p["ab_w_small_t"],
                                  tm, AB_MAIN // 4)
            tl = min(c, l)
            smt = smt.reshape(AB_SMALL, b, l // tl, tl).transpose(1, 2, 0, 3)
            o, convs, gdns, ssms = mixer_ab(proj.reshape(b, l, -1), smt, conv0, gdn0, ssm0, p["conv_w"], p["conv_b"],
                                            p["pcol"], p["gdn_norm_g"], p["dskip"], p["ssm_norm_g"], c)
            xt = out_proj(o.reshape(t, -1), p["ab_w_out"], xt, g1, mod_map(tm), tm)
        else:
            proj = norm_proj(xt, norm1_g[layer], sc1, sh1, mod_map(tm), p["ret_w_in"], None, tm, RET_IN // 4)
            cos, sin = _rope_tables(pos0, l)
            o, rets = mixer_ret(proj.reshape(b, l, -1), cos, sin, ret0, p["ret_norm_g"], c)
            xt = out_proj(o.reshape(t, -1), p["ret_w_out"], xt, g1, mod_map(tm), tm)
        hmt, r2, e2, lc, gg = peer_topk(xt, norm2_g[layer], sc2, sh2, mod_map(tt), p["wq_t"][layer], p["keys"][layer], tt)
        xt = peer_dense(hmt, p["u"][layer], p["vt"][layer], r2, e2, lc, gg, xt, g2, mod_map(tt),
                        final_g if layer == DEPTH - 1 else None, tt, PEER_EB)
    return xt.reshape(b, l, D_MODEL), convs[None], gdns[None], ssms[None], rets[None]


def kernel(x_prompt, x_sample, c_prompt, c_sample, state_conv, state_gdn, state_ssm, state_ret, ada_w, ada_b,
           norm1_g, norm2_g, ab_w_in, ab_conv_w, ab_conv_b, gdn_a_log, gdn_dt_bias, gdn_norm_g, ssm_a_log,
           ssm_dt_bias, ssm_d, ssm_norm_g, ab_w_out, ret_w_in, ret_norm_g, ret_w_out, peer_w_q, peer_keys,
           peer_u, peer_v, final_g):
    p = _prepare(ab_w_in, ab_conv_w, ab_conv_b, gdn_a_log, gdn_dt_bias, gdn_norm_g, ssm_a_log, ssm_dt_bias, ssm_d,
                 ssm_norm_g, ab_w_out, ret_w_in, ret_norm_g, ret_w_out, peer_w_q, peer_keys, peer_u, peer_v)
    nb = x_prompt.shape[0]
    mods = ada_mod(jnp.concatenate([c_prompt, c_sample], axis=0), ada_w, ada_b)
    zeros = lambda s: jnp.zeros((nb,) + s.shape[2:], s.dtype)
    y_p, p_conv, p_gdn, p_ssm, p_ret = _trunk(
        x_prompt, mods[:, :nb], 0, zeros(state_conv), zeros(state_gdn), zeros(state_ssm), zeros(state_ret),
        p, norm1_g, norm2_g, final_g)
    y_s, s_conv, s_gdn, s_ssm, s_ret = _trunk(
        x_sample, mods[:, nb:], PAST_LEN, state_conv[0], state_gdn[0], state_ssm[0], state_ret[0],
        p, norm1_g, norm2_g, final_g)
    return (y_p, y_s, p_conv, p_gdn, p_ssm, p_ret, s_conv, s_gdn, s_ssm, s_ret)
```

```python
import functools
import math

import jax
import jax.numpy as jnp
from jax import lax
from jax.experimental import pallas as pl
from jax.experimental.pallas import tpu as pltpu

f32 = jnp.float32
bf16 = jnp.bfloat16
HI = lax.Precision.HIGHEST

D_MODEL = 1024
DEPTH = 2
PAST_LEN = 16384
GDN_HEADS = 8
GDN_DK = 128
GDN_DV = 128
SSM_HEADS = 16
SSM_P = 64
SSM_N = 128
SSM_G = 2
CONV_W = 4
RET_HEADS = 4
RET_DK = 256
RET_DV = 512
ROPE_BASE = 10000.0
PEER_HEADS = 8
PEER_NKEYS = 128
PEER_TOPK = 16
PEER_DQ = 256
CHUNK = 64
EPS = 1e-6

GDN_QK = GDN_HEADS * GDN_DK
GDN_V = GDN_HEADS * GDN_DV
SSM_INNER = SSM_HEADS * SSM_P
SSM_BC = SSM_G * SSM_N
CONV_CH = 2 * GDN_QK + GDN_V + SSM_INNER + 2 * SSM_BC
AB_MAIN = CONV_CH + GDN_V + SSM_INNER
AB_SMALL = 2 * GDN_HEADS + SSM_HEADS
RET_QK = RET_HEADS * RET_DK
RET_V = RET_HEADS * RET_DV
RET_IN = 2 * RET_QK + 2 * RET_V
N_EXPERTS = PEER_NKEYS * PEER_NKEYS

VMEM_LIMIT = 56 * 1024 * 1024
PEER_SB = 256
PEER_EB = 2048
PEER_TH = 256

_CAND = [(a, b) for a in range(PEER_TOPK) for b in range(PEER_TOPK) if (a + 1) * (b + 1) <= PEER_TOPK]
_NCAND = len(_CAND)
_NCAND_PAD = -(-_NCAND // 16) * 16


def _params(sem):
    return pltpu.CompilerParams(dimension_semantics=sem, vmem_limit_bytes=VMEM_LIMIT)


def _nt(a, b, **kw):
    return lax.dot_general(a, b, (((1,), (1,)), ((), ())), preferred_element_type=f32, **kw)


def _tn(a, b, **kw):
    return lax.dot_general(a, b, (((0,), (0,)), ((), ())), preferred_element_type=f32, **kw)


def _mm(a, b, **kw):
    return jnp.dot(a, b, preferred_element_type=f32, **kw)


def _bmm(a, b, **kw):
    return jnp.einsum('hik,hkj->hij', a, b, preferred_element_type=f32, **kw)


def _bnt(a, b, **kw):
    return jnp.einsum('hik,hjk->hij', a, b, preferred_element_type=f32, **kw)


def _btn(a, b, **kw):
    return jnp.einsum('hki,hkj->hij', a, b, preferred_element_type=f32, **kw)


def _split(x):
    hi = x.astype(bf16)
    return hi, (x - hi.astype(f32)).astype(bf16)


def _bmm3(a, b):
    ah, al = _split(a)
    bh, bl = _split(b)
    return _bmm(ah, bh) + (_bmm(ah, bl) + _bmm(al, bh))


def _silu(x):
    return x * jax.nn.sigmoid(x)


def _gelu(x):
    return 0.5 * x * (1.0 + lax.erf(x * (2.0 ** -0.5)))


def _rms(x):
    return x * lax.rsqrt(jnp.mean(x * x, axis=-1, keepdims=True) + EPS)


def _mod_kernel(c_ref, w_ref, b_ref, o_ref):
    a = _silu(c_ref[...]).astype(bf16)
    o_ref[0] = _mm(a, w_ref[0].astype(bf16)) + b_ref[0]


def ada_mod(c_all, ada_w, ada_b):
    m = c_all.shape[0]
    tn = 768
    n = ada_w.shape[-1]
    return pl.pallas_call(
        _mod_kernel,
        grid=(DEPTH, n // tn),
        in_specs=[
            pl.BlockSpec((m, D_MODEL), lambda l, j: (0, 0)),
            pl.BlockSpec((1, D_MODEL, tn), lambda l, j: (l, 0, j)),
            pl.BlockSpec((1, 1, tn), lambda l, j: (l, 0, j)),
        ],
        out_specs=pl.BlockSpec((1, m, tn), lambda l, j: (l, 0, j)),
        out_shape=jax.ShapeDtypeStruct((DEPTH, m, n), f32),
        compiler_params=_params(("arbitrary", "arbitrary")),
        name="ada_mod",
    )(c_all, ada_w, ada_b.reshape(DEPTH, 1, n))


def _proj_kernel(x_ref, g_ref, sc_ref, sh_ref, w_ref, *rest, has_small):
    if has_small:
        ws_ref, o_ref, os_ref, hm_ref = rest
    else:
        o_ref, hm_ref = rest
    j = pl.program_id(1)

    @pl.when(j == 0)
    def _():
        y = _rms(x_ref[...]) * g_ref[...]
        hm = (y * (1.0 + sc_ref[0]) + sh_ref[0]).astype(bf16)
        hm_ref[...] = hm
        if has_small:
            os_ref[...] = _nt(ws_ref[...], hm)

    o_ref[...] = _mm(hm_ref[...], w_ref[...])


def norm_proj(x, gamma, sc, sh, mod_map, w, ws_t, tm, tn):
    t = x.shape[0]
    n = w.shape[1]
    r = sc.shape[1]
    has_small = ws_t is not None
    in_specs = [
        pl.BlockSpec((tm, D_MODEL), lambda i, j: (i, 0)),
        pl.BlockSpec((1, D_MODEL), lambda i, j: (0, 0)),
        pl.BlockSpec((1, r, D_MODEL), lambda i, j: mod_map(i)),
        pl.BlockSpec((1, r, D_MODEL), lambda i, j: mod_map(i)),
        pl.BlockSpec((D_MODEL, tn), lambda i, j: (0, j)),
    ]
    out_specs = [pl.BlockSpec((tm, tn), lambda i, j: (i, j))]
    out_shape = [jax.ShapeDtypeStruct((t, n), f32)]
    args = [x, gamma.reshape(1, D_MODEL), sc, sh, w]
    if has_small:
        s = ws_t.shape[0]
        in_specs.append(pl.BlockSpec((s, D_MODEL), lambda i, j: (0, 0)))
        out_specs.append(pl.BlockSpec((s, tm), lambda i, j: (0, i)))
        out_shape.append(jax.ShapeDtypeStruct((s, t), f32))
        args.append(ws_t)
    res = pl.pallas_call(
        functools.partial(_proj_kernel, has_small=has_small),
        grid=(t // tm, n // tn),
        in_specs=in_specs,
        out_specs=out_specs,
        out_shape=out_shape,
        scratch_shapes=[pltpu.VMEM((tm, D_MODEL), bf16)],
        compiler_params=_params(("arbitrary", "arbitrary")),
        name="norm_proj",
    )(*args)
    return res if has_small else res[0]


def _out_kernel(o_ref, w_ref, x_ref, g_ref, y_ref):
    y_ref[...] = x_ref[...] + g_ref[0] * _mm(o_ref[...], w_ref[...])


def out_proj(o, w, x, gate, mod_map, tm):
    t, k = o.shape
    r = gate.shape[1]
    return pl.pallas_call(
        _out_kernel,
        grid=(t // tm,),
        in_specs=[
            pl.BlockSpec((tm, k), lambda i: (i, 0)),
            pl.BlockSpec((k, D_MODEL), lambda i: (0, 0)),
            pl.BlockSpec((tm, D_MODEL), lambda i: (i, 0)),
            pl.BlockSpec((1, r, D_MODEL), lambda i: mod_map(i)),
        ],
        out_specs=pl.BlockSpec((tm, D_MODEL), lambda i: (i, 0)),
        out_shape=jax.ShapeDtypeStruct((t, D_MODEL), f32),
        compiler_params=_params(("arbitrary",)),
        name="out_proj",
    )(o, w, x, gate)


def _unit_lower_inverse(lm, c):
    eye = (lax.broadcasted_iota(jnp.int32, (c, c), 0) == lax.broadcasted_iota(jnp.int32, (c, c), 1)).astype(f32)
    p = eye - lm
    lp = lm
    k = 2
    while k < c:
        lp = _bmm3(lp, lp)
        p = p + _bmm3(p, lp)
        k *= 2
    return p


def _ab_kernel(proj_ref, smt_ref, conv0_ref, gdn0_ref, ssm0_ref, convw_ref, convb_ref, pcol_ref, gng_ref,
               dskip_ref, sng_ref, o_ref, convn_ref, gdnn_ref, ssmn_ref, xbuf, act, sg, ss, ob, *, c, lv):
    l = pl.program_id(1)
    nl = pl.num_programs(1)

    @pl.when(l == 0)
    def _():
        xbuf[...] = jnp.zeros(xbuf.shape, f32)
        xbuf[5:8, :] = conv0_ref[0]
        sg[...] = gdn0_ref[0]
        ss[...] = ssm0_ref[0]

    xbuf[8:8 + lv, :] = proj_ref[0, :, 0:CONV_CH]
    y = convb_ref[...]
    for i in range(CONV_W):
        y = y + convw_ref[i:i + 1, :] * xbuf[5 + i:5 + i + c, :]
    act[...] = _silu(y)
    tail = xbuf[5 + lv:8 + lv, :]
    xbuf[5:8, :] = tail

    if lv == c:
        sm = smt_ref[0, 0]
    else:
        lane = lax.broadcasted_iota(jnp.int32, (AB_SMALL, c), 1)
        sm = jnp.where(lane < lv, jnp.broadcast_to(smt_ref[0, 0], (AB_SMALL, c)), 0.0)
    valid = lax.broadcasted_iota(jnp.int32, (AB_SMALL, c), 1) < lv
    bias_col = pcol_ref[:, 0:1]
    alog_col = pcol_ref[:, 1:2]
    beta_t = jnp.where(valid, jax.nn.sigmoid(sm), 0.0)
    sp_t = jnp.where(valid, jax.nn.softplus(sm + bias_col), 0.0)
    la_t = -jnp.exp(alog_col) * sp_t
    rr = lax.broadcasted_iota(jnp.int32, (c, c), 0)
    qq = lax.broadcasted_iota(jnp.int32, (c, c), 1)
    triu = (rr <= qq).astype(f32)
    eye = (rr == qq).astype(f32)
    cum_t = _mm(la_t, triu, precision=HI)
    last_t = cum_t[:, c - 1:c]
    ecum_t = jnp.exp(cum_t)
    elc_t = jnp.exp(last_t - cum_t)
    elast_t = jnp.exp(last_t)
    g0, g1, s0, s1 = 0, GDN_HEADS, 2 * GDN_HEADS, AB_SMALL
    rows = jnp.concatenate([
        beta_t[g0:g1],
        cum_t[g1:s0],
        ecum_t[g1:s0],
        beta_t[g0:g1] * ecum_t[g1:s0],
        elc_t[g1:s0],
        cum_t[s0:s1],
        ecum_t[s0:s1],
        sp_t[s0:s1] * elc_t[s0:s1],
        jnp.zeros((128 - 88, c), f32),
    ], axis=0)
    cols = _nt(eye, rows, precision=HI)
    incl = rr >= qq
    strict = rr > qq

    def colstack(base, n):
        return jnp.stack([cols[:, base + h:base + h + 1] for h in range(n)])

    def rowstack(x, base, n):
        return jnp.stack([x[base + h:base + h + 1, :] for h in range(n)])

    hs = range(GDN_HEADS)
    q = jnp.stack([act[:, h * GDN_DK:(h + 1) * GDN_DK] for h in hs])
    k = jnp.stack([act[:, GDN_QK + h * GDN_DK:GDN_QK + (h + 1) * GDN_DK] for h in hs])
    v = jnp.stack([act[:, 2 * GDN_QK + h * GDN_DV:2 * GDN_QK + (h + 1) * GDN_DV] for h in hs])
    q = q * lax.rsqrt(jnp.sum(q * q, axis=-1, keepdims=True) + EPS) * (GDN_DK ** -0.5)
    k = k * lax.rsqrt(jnp.sum(k * k, axis=-1, keepdims=True) + EPS)
    beta_c, cum_c, ecum_c, becum_c, elc_c = (colstack(b, GDN_HEADS) for b in (0, 8, 16, 24, 32))
    dec = jnp.exp(jnp.where(incl, cum_c - rowstack(cum_t, g1, GDN_HEADS), -jnp.inf))
    kb = k.astype(bf16)
    qb = q.astype(bf16)
    lm = jnp.where(strict, beta_c * _bnt(kb, kb) * dec, 0.0)
    pinv = _unit_lower_inverse(lm, c)
    rhs = jnp.concatenate([v * beta_c, k * becum_c], axis=2)
    sol = _bmm3(pinv, rhs)
    u0 = sol[:, :, :GDN_DV]
    w = sol[:, :, GDN_DV:]
    qk = _bnt(qb, kb) * dec
    s_old = sg[...]
    sb = s_old.astype(bf16)
    u = u0 - _bmm(w.astype(bf16), sb)
    ub = u.astype(bf16)
    o = _bmm((q * ecum_c).astype(bf16), sb) + _bmm(qk.astype(bf16), ub)
    kd = (k * elc_c).astype(bf16)
    sg[...] = s_old * rowstack(elast_t, g1, GDN_HEADS) + _btn(kd, ub)
    oa = _rms(o) * gng_ref[...]
    for h in hs:
        gate = proj_ref[0, :, CONV_CH + h * GDN_DV:CONV_CH + (h + 1) * GDN_DV]
        o_ref[0, :, h * GDN_DV:(h + 1) * GDN_DV] = (oa[h][0:lv] * _silu(gate)).astype(bf16)

    rep = SSM_HEADS // SSM_G
    xs0 = 2 * GDN_QK + GDN_V
    bm0 = xs0 + SSM_INNER
    cm0 = bm0 + SSM_BC
    hs = range(SSM_HEADS)
    bmb = [act[:, bm0 + g * SSM_N:bm0 + (g + 1) * SSM_N].astype(bf16) for g in range(SSM_G)]
    cmb = [act[:, cm0 + g * SSM_N:cm0 + (g + 1) * SSM_N].astype(bf16) for g in range(SSM_G)]
    cbg = [_nt(cmb[g], bmb[g]) for g in range(SSM_G)]
    cb = jnp.stack([cbg[h // rep] for h in hs])
    bm16 = jnp.stack([bmb[h // rep] for h in hs])
    cm16 = jnp.stack([cmb[h // rep] for h in hs])
    xs = jnp.stack([act[:, xs0 + h * SSM_P:xs0 + (h + 1) * SSM_P] for h in hs])
    dsk = jnp.stack([dskip_ref[:, h * SSM_P:(h + 1) * SSM_P] for h in hs])
    cum_c, ecum_c, dtelc_c = (colstack(b, SSM_HEADS) for b in (40, 56, 72))
    m = cb * jnp.exp(jnp.where(incl, cum_c - rowstack(cum_t, s0, SSM_HEADS), -jnp.inf)) * rowstack(sp_t, s0, SSM_HEADS)
    s_old = ss[...]
    o = _bmm(m.astype(bf16), xs.astype(bf16)) + _bmm(cm16, s_old.astype(bf16)) * ecum_c + dsk * xs
    ss[...] = s_old * rowstack(elast_t, s0, SSM_HEADS) + _btn(bm16, (xs * dtelc_c).astype(bf16))
    for h in hs:
        ob[:, h * SSM_P:(h + 1) * SSM_P] = o[h]
    z = proj_ref[0, :, CONV_CH + GDN_V:CONV_CH + GDN_V + SSM_INNER]
    obv = _rms(ob[0:lv, :] * _silu(z)) * sng_ref[...]
    o_ref[0, :, GDN_V:GDN_V + SSM_INNER] = obv.astype(bf16)

    @pl.when(l == nl - 1)
    def _():
        convn_ref[0] = xbuf[5:8, :]
        gdnn_ref[0] = sg[...]
        ssmn_ref[0] = ss[...]


def mixer_ab(proj, smt, conv0, gdn0, ssm0, conv_w, conv_b, pcol, gdn_norm_g, dskip, ssm_norm_g, c):
    b, l, npad = proj.shape
    tl = min(c, l)
    nl = l // tl
    kern = functools.partial(_ab_kernel, c=c, lv=tl)
    full = lambda shape: pl.BlockSpec(shape, lambda i, j: (0,) * len(shape))
    return pl.pallas_call(
        kern,
        grid=(b, nl),
        in_specs=[
            pl.BlockSpec((1, tl, npad), lambda i, j: (i, j, 0)),
            pl.BlockSpec((1, 1, AB_SMALL, tl), lambda i, j: (i, j, 0, 0)),
            pl.BlockSpec((1, CONV_W - 1, CONV_CH), lambda i, j: (i, 0, 0)),
            pl.BlockSpec((1, GDN_HEADS, GDN_DK, GDN_DV), lambda i, j: (i, 0, 0, 0)),
            pl.BlockSpec((1, SSM_HEADS, SSM_N, SSM_P), lambda i, j: (i, 0, 0, 0)),
            full((CONV_W, CONV_CH)),
            full((1, CONV_CH)),
            full((AB_SMALL, 2)),
            full((1, GDN_DV)),
            full((1, SSM_INNER)),
            full((1, SSM_INNER)),
        ],
        out_specs=[
            pl.BlockSpec((1, tl, GDN_V + SSM_INNER), lambda i, j: (i, j, 0)),
            pl.BlockSpec((1, CONV_W - 1, CONV_CH), lambda i, j: (i, 0, 0)),
            pl.BlockSpec((1, GDN_HEADS, GDN_DK, GDN_DV), lambda i, j: (i, 0, 0, 0)),
            pl.BlockSpec((1, SSM_HEADS, SSM_N, SSM_P), lambda i, j: (i, 0, 0, 0)),
        ],
        out_shape=[
            jax.ShapeDtypeStruct((b, l, GDN_V + SSM_INNER), bf16),
            jax.ShapeDtypeStruct((b, CONV_W - 1, CONV_CH), f32),
            jax.ShapeDtypeStruct((b, GDN_HEADS, GDN_DK, GDN_DV), f32),
            jax.ShapeDtypeStruct((b, SSM_HEADS, SSM_N, SSM_P), f32),
        ],
        scratch_shapes=[
            pltpu.VMEM((8 + c, CONV_CH), f32),
            pltpu.VMEM((c, CONV_CH), f32),
            pltpu.VMEM((GDN_HEADS, GDN_DK, GDN_DV), f32),
            pltpu.VMEM((SSM_HEADS, SSM_N, SSM_P), f32),
            pltpu.VMEM((c, SSM_INNER), f32),
        ],
        compiler_params=_params(("arbitrary", "arbitrary")),
        name="mixer_ab",
    )(proj, smt, conv0, gdn0, ssm0, conv_w, conv_b, pcol, gdn_norm_g, dskip, ssm_norm_g)


def _ret_kernel(proj_ref, cos_ref, sin_ref, ret0_ref, ng_ref, o_ref, retn_ref, st, buf, *, c, lv):
    l = pl.program_id(1)
    nl = pl.num_programs(1)

    @pl.when(l == 0)
    def _():
        st[...] = ret0_ref[0]
        if lv < c:
            buf[...] = jnp.zeros(buf.shape, f32)

    if lv == c:
        src = proj_ref.at[0]
        cos = cos_ref[...]
        sin = sin_ref[...]
    else:
        buf[0:lv, :] = proj_ref[0]
        src = buf
        cos = jnp.broadcast_to(cos_ref[...], (c, RET_DK // 2))
        sin = jnp.broadcast_to(sin_ref[...], (c, RET_DK // 2))

    ri = lax.broadcasted_iota(jnp.int32, (c, c), 0)
    ci = lax.broadcasted_iota(jnp.int32, (c, c), 1)
    incl = ri >= ci
    cnt_r = jnp.minimum(ri + 1, lv).astype(f32)
    cnt_c = jnp.minimum(ci + 1, lv).astype(f32)
    cnt_col = jnp.minimum(lax.broadcasted_iota(jnp.int32, (c, 1), 0) + 1, lv).astype(f32)
    half = RET_DK // 2

    def rope(x):
        x1, x2 = x[:, :half], x[:, half:]
        return jnp.concatenate([x1 * cos - x2 * sin, x1 * sin + x2 * cos], axis=1)

    for h in range(RET_HEADS):
        lg = math.log(1.0 - 2.0 ** (-5.0 - h))
        q = rope(src[:, h * RET_DK:(h + 1) * RET_DK])
        k = rope(src[:, RET_QK + h * RET_DK:RET_QK + (h + 1) * RET_DK]) * (RET_DK ** -0.5)
        v = src[:, 2 * RET_QK + h * RET_DV:2 * RET_QK + (h + 1) * RET_DV].astype(bf16)
        gate = src[:, 2 * RET_QK + RET_V + h * RET_DV:2 * RET_QK + RET_V + (h + 1) * RET_DV]
        dec = jnp.exp(jnp.where(incl, (cnt_r - cnt_c) * lg, -jnp.inf))
        scores = _nt(q.astype(bf16), k.astype(bf16)) * dec
        s_old = st[h]
        o = _mm(scores.astype(bf16), v) + _mm((q * jnp.exp(cnt_col * lg)).astype(bf16), s_old.astype(bf16))
        kd = (k * jnp.exp((lv - cnt_col) * lg)).astype(bf16)
        st[h] = s_old * math.exp(lv * lg) + _tn(kd, v)
        mu = jnp.mean(o, axis=-1, keepdims=True)
        var = jnp.mean(jnp.square(o - mu), axis=-1, keepdims=True)
        o = (o - mu) * lax.rsqrt(var + EPS) * ng_ref[h:h + 1, :]
        o_ref[0, :, h * RET_DV:(h + 1) * RET_DV] = (_silu(gate) * o)[0:lv].astype(bf16)

    @pl.when(l == nl - 1)
    def _():
        retn_ref[0] = st[...]


def mixer_ret(proj, cos, sin, ret0, norm_g, c):
    b, l, n = proj.shape
    tl = min(c, l)
    nl = l // tl
    kern = functools.partial(_ret_kernel, c=c, lv=tl)
    return pl.pallas_call(
        kern,
        grid=(b, nl),
        in_specs=[
            pl.BlockSpec((1, tl, n), lambda i, j: (i, j, 0)),
            pl.BlockSpec((tl, RET_DK // 2), lambda i, j: (j, 0)),
            pl.BlockSpec((tl, RET_DK // 2), lambda i, j: (j, 0)),
            pl.BlockSpec((1, RET_HEADS, RET_DK, RET_DV), lambda i, j: (i, 0, 0, 0)),
            pl.BlockSpec((RET_HEADS, RET_DV), lambda i, j: (0, 0)),
        ],
        out_specs=[
            pl.BlockSpec((1, tl, RET_V), lambda i, j: (i, j, 0)),
            pl.BlockSpec((1, RET_HEADS, RET_DK, RET_DV), lambda i, j: (i, 0, 0, 0)),
        ],
        out_shape=[
            jax.ShapeDtypeStruct((b, l, RET_V), bf16),
            jax.ShapeDtypeStruct((b, RET_HEADS, RET_DK, RET_DV), f32),
        ],
        scratch_shapes=[
            pltpu.VMEM((RET_HEADS, RET_DK, RET_DV), f32),
            pltpu.VMEM((c, n), f32),
        ],
        compiler_params=_params(("arbitrary", "arbitrary")),
        name="mixer_ret",
    )(proj, cos, sin, ret0, norm_g)


def _top16(s, exact):
    g, n, t = s.shape
    rows = lax.broadcasted_iota(jnp.int32, (g, n, t), 1)
    krow = lax.broadcasted_iota(jnp.int32, (g, PEER_TOPK, t), 1)
    rank = jnp.full((g, n, t), float(PEER_TOPK), f32)
    sv = jnp.zeros((g, PEER_TOPK, t), f32)
    work = s
    for k in range(PEER_TOPK):
        m = jnp.max(work, axis=1, keepdims=True)
        if exact:
            idx = jnp.min(jnp.where(work == m, rows, n), axis=1, keepdims=True)
            sel = rows == idx
        else:
            sel = work == m
        rank = jnp.where(sel, float(k), rank)
        work = jnp.where(sel, -jnp.inf, work)
        sv = jnp.where(krow == k, m, sv)
    return sv, rank


def _top16_fast(s, bad):
    n, t = s.shape
    krow = lax.broadcasted_iota(jnp.int32, (PEER_TOPK, t), 0)
    m = jnp.max(s, axis=0, keepdims=True)
    tops = [m]
    for _ in range(1, PEER_TOPK):
        m = jnp.max(jnp.where(s < m, s, -jnp.inf), axis=0, keepdims=True)
        tops.append(m)
    rank = jnp.zeros((n, t), f32)
    sv = jnp.zeros((PEER_TOPK, t), f32)
    for k in range(PEER_TOPK):
        rank = rank + jnp.where(s < tops[k], 1.0, 0.0)
        sv = jnp.where(krow == k, tops[k], sv)
    cnt = jnp.sum(jnp.where(rank < float(PEER_TOPK), 1.0, 0.0), axis=0, keepdims=True)
    return sv, rank, jnp.maximum(bad, jnp.abs(cnt - float(PEER_TOPK)))


def _topk_kernel(x_ref, g_ref, sc_ref, sh_ref, wqt_ref, keys_ref, a1_ref, a2_ref, a1t_ref,
                 hmt_ref, r2_ref, e2_ref, lc_ref, g1_ref, s_scr, sv_scr, rk_scr, cd_scr, rc_scr):
    y = _rms(x_ref[...]) * g_ref[...]
    hm = y * (1.0 + sc_ref[0]) + sh_ref[0]
    hmt = hm.T.astype(bf16)
    hmt_ref[0] = hmt
    qt = _mm(wqt_ref[...], hmt)
    tt = qt.shape[1]
    half = PEER_DQ // 2
    nh = PEER_HEADS
    for h in range(nh):
        s_scr[h] = _mm(keys_ref[0], qt[h * PEER_DQ:h * PEER_DQ + half, :].astype(bf16))
        s_scr[nh + h] = _mm(keys_ref[1], qt[h * PEER_DQ + half:(h + 1) * PEER_DQ, :].astype(bf16))
    lane_groups = [slice(i * 128, (i + 1) * 128) for i in range(tt // 128)]
    crow = lax.broadcasted_iota(jnp.int32, (_NCAND_PAD, tt), 0)

    def candidates(sv1, sv2):
        cand = _mm(a1_ref[...], sv1, precision=HI) + _mm(a2_ref[...], sv2, precision=HI)
        return jnp.where(crow < _NCAND, cand, -jnp.inf)

    def stage1(g, bad):
        for ls in lane_groups:
            sv, rank, bad = _top16_fast(s_scr[g, :, ls], bad)
            sv_scr[g, :, ls] = sv
            rk_scr[g, :, ls] = rank
        return bad

    def stage2(i, bad):
        for h in (2 * i, 2 * i + 1):
            cand = candidates(sv_scr[h], sv_scr[nh + h])
            cd_scr[h] = cand
            for ls in lane_groups:
                _, rankc, bad = _top16_fast(cand[:, ls], bad)
                rc_scr[h, :, ls] = rankc
        return bad

    bad = lax.fori_loop(0, 2 * nh, stage1, jnp.zeros((1, 128), f32))
    bad = lax.fori_loop(0, nh // 2, stage2, bad)

    @pl.when(jnp.max(bad) > 0.0)
    def _():
        sv, rank = _top16(s_scr[...], True)
        cand = jnp.stack([candidates(sv[h], sv[nh + h]) for h in range(nh)])
        _, rankc = _top16(cand, True)
        sv_scr[...] = sv
        rk_scr[...] = rank
        cd_scr[...] = cand
        rc_scr[...] = rankc

    def finish(h, carry):
        sel = jnp.where(rc_scr[h] < float(PEER_TOPK), 1.0, 0.0)
        cand = cd_scr[h]
        z = jnp.sum(sel * jnp.exp(jnp.where(crow < _NCAND, cand - cand[0:1, :], 0.0)), axis=0, keepdims=True)
        cnt = _mm(a1t_ref[...], sel.astype(bf16))
        for ls in lane_groups:
            rank1 = rk_scr[h, :, ls]
            lc = jnp.zeros(rank1.shape, f32)
            for k1 in range(PEER_TOPK):
                lc = jnp.where(rank1 == float(k1), cnt[k1:k1 + 1, ls], lc)
            lc_ref[h, 0, :, ls] = lc
        r2_ref[h, 0] = rk_scr[nh + h]
        e2_ref[h, 0] = jnp.exp(s_scr[nh + h] - sv_scr[nh + h, 0:1, :])
        g1_ref[h, 0] = jnp.exp(s_scr[h] - sv_scr[h, 0:1, :]) / z
        return carry

    lax.fori_loop(0, nh, finish, 0)


def peer_topk(x, gamma, sc, sh, mod_map, wq_t, keys, tt):
    t = x.shape[0]
    r = sc.shape[1]
    a1 = jnp.zeros((_NCAND_PAD, PEER_TOPK), f32).at[jnp.arange(_NCAND), jnp.array([a for a, _ in _CAND])].set(1.0)
    a2 = jnp.zeros((_NCAND_PAD, PEER_TOPK), f32).at[jnp.arange(_NCAND), jnp.array([b for _, b in _CAND])].set(1.0)
    full = lambda shape: pl.BlockSpec(shape, lambda i: (0,) * len(shape))
    tok = pl.BlockSpec((PEER_HEADS, 1, PEER_NKEYS, tt), lambda i: (0, i, 0, 0))
    tok_shape = jax.ShapeDtypeStruct((PEER_HEADS, t // tt, PEER_NKEYS, tt), f32)
    return pl.pallas_call(
        _topk_kernel,
        grid=(t // tt,),
        in_specs=[
            pl.BlockSpec((tt, D_MODEL), lambda i: (i, 0)),
            full((1, D_MODEL)),
            pl.BlockSpec((1, r, D_MODEL), lambda i: mod_map(i)),
            pl.BlockSpec((1, r, D_MODEL), lambda i: mod_map(i)),
            full((PEER_HEADS * PEER_DQ, D_MODEL)),
            full((2, PEER_NKEYS, PEER_DQ // 2)),
            full((_NCAND_PAD, PEER_TOPK)),
            full((_NCAND_PAD, PEER_TOPK)),
            full((PEER_TOPK, _NCAND_PAD)),
        ],
        out_specs=[pl.BlockSpec((1, D_MODEL, tt), lambda i: (i, 0, 0)), tok, tok, tok, tok],
        out_shape=[jax.ShapeDtypeStruct((t // tt, D_MODEL, tt), bf16), tok_shape, tok_shape, tok_shape, tok_shape],
        scratch_shapes=[
            pltpu.VMEM((2 * PEER_HEADS, PEER_NKEYS, tt), f32),
            pltpu.VMEM((2 * PEER_HEADS, PEER_TOPK, tt), f32),
            pltpu.VMEM((2 * PEER_HEADS, PEER_NKEYS, tt), f32),
            pltpu.VMEM((PEER_HEADS, _NCAND_PAD, tt), f32),
            pltpu.VMEM((PEER_HEADS, _NCAND_PAD, tt), f32),
        ],
        compiler_params=_params(("arbitrary",)),
        name="peer_topk",
    )(x, gamma.reshape(1, D_MODEL), sc, sh, wq_t, keys, a1, a2, a1.T.astype(bf16))


def _peer_kernel(hmt_ref, u_ref, vt_ref, r2_ref, e2_ref, lc_ref, g1_ref, x_ref, g_ref, fg_ref, o_ref,
                 yt, act_a, act_b, ht_a, ht_b, *, nsub, final):
    j = pl.program_id(1)
    nj = pl.num_programs(1)
    na = PEER_SB // PEER_NKEYS
    nhalf, _, th = hmt_ref.shape

    @pl.when(j == 0)
    def _():
        yt[...] = jnp.zeros(yt.shape, f32)

    for hf in range(nhalf):
        def mm1(s, ht):
            ht[...] = _mm(u_ref[s], hmt_ref[hf])

        def gate_act(s, ht, act):
            for a in range(na):
                n1 = (j * nsub + s) * na + a
                lrows = [lc_ref[h, hf, pl.ds(n1, 1), :] for h in range(PEER_HEADS)]
                grows = [g1_ref[h, hf, pl.ds(n1, 1), :] for h in range(PEER_HEADS)]
                for lg in range(th // 128):
                    ls = slice(lg * 128, (lg + 1) * 128)
                    w = None
                    for h in range(PEER_HEADS):
                        wh = jnp.where(r2_ref[h, hf, :, ls] < lrows[h][:, ls], e2_ref[h, hf, :, ls] * grows[h][:, ls], 0.0)
                        w = wh if w is None else w + wh
                    hb = ht[a * PEER_NKEYS:(a + 1) * PEER_NKEYS, ls]
                    act[a * PEER_NKEYS:(a + 1) * PEER_NKEYS, ls] = (_gelu(hb) * w).astype(bf16)

        def mm2(s, act):
            yt[hf] += _mm(vt_ref[s], act[...])

        act_b[...] = jnp.zeros(act_b.shape, bf16)
        mm1(0, ht_a)

        def body(i, carry):
            s = 2 * i
            mm1(s + 1, ht_b)
            gate_act(s, ht_a, act_a)
            mm2(jnp.maximum(s - 1, 0), act_b)
            mm1(jnp.minimum(s + 2, nsub - 1), ht_a)
            gate_act(s + 1, ht_b, act_b)
            mm2(s, act_a)
            return carry

        lax.fori_loop(0, nsub // 2, body, 0)
        mm2(nsub - 1, act_b)

    @pl.when(j == nj - 1)
    def _():
        for hf in range(nhalf):
            rows = slice(hf * th, (hf + 1) * th)
            gate = g_ref[0] if g_ref.shape[1] == 1 else g_ref[0, rows, :]
            xn = x_ref[rows, :] + gate * yt[hf].T
            if final:
                xn = _rms(xn) * fg_ref[...]
            o_ref[rows, :] = xn


def peer_dense(hmt, u3, vt3, r2, e2, lc, g1, x, gate, mod_map, final_g, tt, eb):
    t = x.shape[0]
    r = gate.shape[1]
    nsub = eb // PEER_SB
    th = min(tt, PEER_TH)
    final = final_g is not None
    fg = (final_g if final else jnp.ones((D_MODEL,), f32)).reshape(1, D_MODEL)
    nhalf = tt // th
    tok = pl.BlockSpec((PEER_HEADS, nhalf, PEER_NKEYS, th), lambda i, j: (0, i, 0, 0))
    return pl.pallas_call(
        functools.partial(_peer_kernel, nsub=nsub, final=final),
        grid=(t // tt, N_EXPERTS // eb),
        in_specs=[
            pl.BlockSpec((nhalf, D_MODEL, th), lambda i, j: (i, 0, 0)),
            pl.BlockSpec((nsub, PEER_SB, D_MODEL), lambda i, j: (j, 0, 0)),
            pl.BlockSpec((nsub, D_MODEL, PEER_SB), lambda i, j: (j, 0, 0)),
            tok, tok, tok, tok,
            pl.BlockSpec((tt, D_MODEL), lambda i, j: (i, 0)),
            pl.BlockSpec((1, r, D_MODEL), lambda i, j: mod_map(i)),
            pl.BlockSpec((1, D_MODEL), lambda i, j: (0, 0)),
        ],
        out_specs=pl.BlockSpec((tt, D_MODEL), lambda i, j: (i, 0)),
        out_shape=jax.ShapeDtypeStruct((t, D_MODEL), f32),
        scratch_shapes=[
            pltpu.VMEM((nhalf, D_MODEL, th), f32),
            pltpu.VMEM((PEER_SB, th), bf16), pltpu.VMEM((PEER_SB, th), bf16),
            pltpu.VMEM((PEER_SB, th), f32), pltpu.VMEM((PEER_SB, th), f32),
        ],
        compiler_params=_params(("arbitrary", "arbitrary")),
        name="peer_dense",
    )(hmt, u3, vt3, r2, e2, lc, g1, x, gate, fg)


def _prepare(ab_w_in, ab_conv_w, ab_conv_b, gdn_a_log, gdn_dt_bias, gdn_norm_g, ssm_a_log, ssm_dt_bias, ssm_d,
             ssm_norm_g, ab_w_out, ret_w_in, ret_norm_g, ret_w_out, peer_w_q, peer_keys, peer_u, peer_v):
    zeros8 = jnp.zeros((GDN_HEADS,), f32)
    nsb = N_EXPERTS // PEER_SB
    return dict(
        ab_w_main=ab_w_in[0][:, :AB_MAIN].astype(bf16),
        ab_w_small_t=ab_w_in[0][:, AB_MAIN:].T.astype(bf16),
        conv_w=ab_conv_w[0],
        conv_b=ab_conv_b[0].reshape(1, CONV_CH),
        pcol=jnp.stack([jnp.concatenate([zeros8, gdn_dt_bias[0], ssm_dt_bias[0]]),
                        jnp.concatenate([zeros8, gdn_a_log[0], ssm_a_log[0]])], axis=1),
        gdn_norm_g=gdn_norm_g[0].reshape(1, GDN_DV),
        dskip=jnp.repeat(ssm_d[0], SSM_P).reshape(1, SSM_INNER),
        ssm_norm_g=ssm_norm_g[0].reshape(1, SSM_INNER),
        ab_w_out=ab_w_out[0].astype(bf16),
        ret_w_in=ret_w_in[0].astype(bf16),
        ret_norm_g=ret_norm_g[0],
        ret_w_out=ret_w_out[0].astype(bf16),
        wq_t=[peer_w_q[i].T.astype(bf16) for i in range(DEPTH)],
        keys=[peer_keys[i].astype(bf16) for i in range(DEPTH)],
        u=[peer_u[i].astype(bf16).reshape(nsb, PEER_SB, D_MODEL) for i in range(DEPTH)],
        vt=[peer_v[i].astype(bf16).reshape(nsb, PEER_SB, D_MODEL).transpose(0, 2, 1) for i in range(DEPTH)],
    )


def _rope_tables(pos0, length):
    inv = ROPE_BASE ** (-jnp.arange(0, RET_DK, 2, dtype=f32) / RET_DK)
    ang = (pos0 + jnp.arange(length, dtype=f32))[:, None] * inv[None, :]
    return jnp.cos(ang), jnp.sin(ang)


def _trunk(x, mods, pos0, conv0, gdn0, ssm0, ret0, p, norm1_g, norm2_g, final_g):
    b, l, _ = x.shape
    t = b * l
    c = CHUNK if l >= CHUNK else 8
    xt = x.reshape(t, D_MODEL)
    if l > 1:
        tm = min(512, l)
        tt = min(512, l)
        mod_arr = lambda m: m.reshape(b, 1, D_MODEL)
        mod_map = lambda rows: (lambda i: (i * rows // l, 0, 0))
    else:
        tm = t
        tt = t
        mod_arr = lambda m: m.reshape(1, t, D_MODEL)
        mod_map = lambda rows: (lambda i: (0, i, 0))
    convs = gdns = ssms = rets = None
    for layer in range(DEPTH):
        sh1, sc1, g1, sh2, sc2, g2 = (mod_arr(m) for m in jnp.split(mods[layer], 6, axis=-1))
        if layer == 0:
            proj, smt = norm_proj(xt, norm1_g[layer], sc1, sh1, mod_map(tm), p["ab_w_main"], p["ab_w_small_t"],
                                  tm, AB_MAIN // 4)
            tl = min(c, l)
            smt = smt.reshape(AB_SMALL, b, l // tl, tl).transpose(1, 2, 0, 3)
            o, convs, gdns, ssms = mixer_ab(proj.reshape(b, l, -1), smt, conv0, gdn0, ssm0, p["conv_w"], p["conv_b"],
                                            p["pcol"], p["gdn_norm_g"], p["dskip"], p["ssm_norm_g"], c)
            xt = out_proj(o.reshape(t, -1), p["ab_w_out"], xt, g1, mod_map(tm), tm)
        else:
            proj = norm_proj(xt, norm1_g[layer], sc1, sh1, mod_map(tm), p["ret_w_in"], None, tm, RET_IN // 4)
            cos, sin = _rope_tables(pos0, l)
            o, rets = mixer_ret(proj.reshape(b, l, -1), cos, sin, ret0, p["ret_norm_g"], c)
            xt = out_proj(o.reshape(t, -1), p["ret_w_out"], xt, g1, mod_map(tm), tm)
        th = min(tt, PEER_TH)
        hmt, r2, e2, lc, gg = peer_topk(xt, norm2_g[layer], sc2, sh2, mod_map(th), p["wq_t"][layer], p["keys"][layer], th)
        xt = peer_dense(hmt, p["u"][layer], p["vt"][layer], r2, e2, lc, gg, xt, g2, mod_map(tt),
                        final_g if layer == DEPTH - 1 else None, tt, PEER_EB)
    return xt.reshape(b, l, D_MODEL), convs[None], gdns[None], ssms[None], rets[None]


def kernel(x_prompt, x_sample, c_prompt, c_sample, state_conv, state_gdn, state_ssm, state_ret, ada_w, ada_b,
           norm1_g, norm2_g, ab_w_in, ab_conv_w, ab_conv_b, gdn_a_log, gdn_dt_bias, gdn_norm_g, ssm_a_log,
           ssm_dt_bias, ssm_d, ssm_norm_g, ab_w_out, ret_w_in, ret_norm_g, ret_w_out, peer_w_q, peer_keys,
           peer_u, peer_v, final_g):
    p = _prepare(ab_w_in, ab_conv_w, ab_conv_b, gdn_a_log, gdn_dt_bias, gdn_norm_g, ssm_a_log, ssm_dt_bias, ssm_d,
                 ssm_norm_g, ab_w_out, ret_w_in, ret_norm_g, ret_w_out, peer_w_q, peer_keys, peer_u, peer_v)
    nb = x_prompt.shape[0]
    mods = ada_mod(jnp.concatenate([c_prompt, c_sample], axis=0), ada_w, ada_b)
    zeros = lambda s: jnp.zeros((nb,) + s.shape[2:], s.dtype)
    y_p, p_conv, p_gdn, p_ssm, p_ret = _trunk(
        x_prompt, mods[:, :nb], 0, zeros(state_conv), zeros(state_gdn), zeros(state_ssm), zeros(state_ret),
        p, norm1_g, norm2_g, final_g)
    y_s, s_conv, s_gdn, s_ssm, s_ret = _trunk(
        x_sample, mods[:, nb:], PAST_LEN, state_conv[0], state_gdn[0], state_ssm[0], state_ret[0],
        p, norm1_g, norm2_g, final_g)
    return (y_p, y_s, p_conv, p_gdn, p_ssm, p_ret, s_conv, s_gdn, s_ssm, s_ret)
```

```python
import functools
import math

import jax
import jax.numpy as jnp
from jax import lax
from jax.experimental import pallas as pl
from jax.experimental.pallas import tpu as pltpu

f32 = jnp.float32
bf16 = jnp.bfloat16
HI = lax.Precision.HIGHEST

D_MODEL = 1024
DEPTH = 2
PAST_LEN = 16384
GDN_HEADS = 8
GDN_DK = 128
GDN_DV = 128
SSM_HEADS = 16
SSM_P = 64
SSM_N = 128
SSM_G = 2
CONV_W = 4
RET_HEADS = 4
RET_DK = 256
RET_DV = 512
ROPE_BASE = 10000.0
PEER_HEADS = 8
PEER_NKEYS = 128
PEER_TOPK = 16
PEER_DQ = 256
CHUNK = 64
EPS = 1e-6

GDN_QK = GDN_HEADS * GDN_DK
GDN_V = GDN_HEADS * GDN_DV
SSM_INNER = SSM_HEADS * SSM_P
SSM_BC = SSM_G * SSM_N
CONV_CH = 2 * GDN_QK + GDN_V + SSM_INNER + 2 * SSM_BC
AB_MAIN = CONV_CH + GDN_V + SSM_INNER
AB_SMALL = 2 * GDN_HEADS + SSM_HEADS
RET_QK = RET_HEADS * RET_DK
RET_V = RET_HEADS * RET_DV
RET_IN = 2 * RET_QK + 2 * RET_V
N_EXPERTS = PEER_NKEYS * PEER_NKEYS

VMEM_LIMIT = 56 * 1024 * 1024
PEER_SB = 256
PEER_EB = 4096
PEER_NKP = PEER_NKEYS + 8

_CAND = [(a, b) for a in range(PEER_TOPK) for b in range(PEER_TOPK) if (a + 1) * (b + 1) <= PEER_TOPK]
_NCAND = len(_CAND)
_NCAND_PAD = -(-_NCAND // 16) * 16


def _params(sem):
    return pltpu.CompilerParams(dimension_semantics=sem, vmem_limit_bytes=VMEM_LIMIT)


def _nt(a, b, **kw):
    return lax.dot_general(a, b, (((1,), (1,)), ((), ())), preferred_element_type=f32, **kw)


def _tn(a, b, **kw):
    return lax.dot_general(a, b, (((0,), (0,)), ((), ())), preferred_element_type=f32, **kw)


def _mm(a, b, **kw):
    return jnp.dot(a, b, preferred_element_type=f32, **kw)


def _bmm(a, b, **kw):
    return jnp.einsum('hik,hkj->hij', a, b, preferred_element_type=f32, **kw)


def _bnt(a, b, **kw):
    return jnp.einsum('hik,hjk->hij', a, b, preferred_element_type=f32, **kw)


def _btn(a, b, **kw):
    return jnp.einsum('hki,hkj->hij', a, b, preferred_element_type=f32, **kw)


def _split(x):
    hi = x.astype(bf16)
    return hi, (x - hi.astype(f32)).astype(bf16)


def _bmm3(a, b):
    ah, al = _split(a)
    bh, bl = _split(b)
    return _bmm(ah, bh) + (_bmm(ah, bl) + _bmm(al, bh))


def _silu(x):
    return x * jax.nn.sigmoid(x)


def _gelu(x):
    return 0.5 * x * (1.0 + lax.erf(x * (2.0 ** -0.5)))


def _rms(x):
    return x * lax.rsqrt(jnp.mean(x * x, axis=-1, keepdims=True) + EPS)


def _mod_kernel(c_ref, w_ref, b_ref, o_ref):
    a = _silu(c_ref[...]).astype(bf16)
    o_ref[0] = _mm(a, w_ref[0].astype(bf16)) + b_ref[0]


def ada_mod(c_all, ada_w, ada_b):
    m = c_all.shape[0]
    tn = 768
    n = ada_w.shape[-1]
    return pl.pallas_call(
        _mod_kernel,
        grid=(DEPTH, n // tn),
        in_specs=[
            pl.BlockSpec((m, D_MODEL), lambda l, j: (0, 0)),
            pl.BlockSpec((1, D_MODEL, tn), lambda l, j: (l, 0, j)),
            pl.BlockSpec((1, 1, tn), lambda l, j: (l, 0, j)),
        ],
        out_specs=pl.BlockSpec((1, m, tn), lambda l, j: (l, 0, j)),
        out_shape=jax.ShapeDtypeStruct((DEPTH, m, n), f32),
        compiler_params=_params(("arbitrary", "arbitrary")),
        name="ada_mod",
    )(c_all, ada_w, ada_b.reshape(DEPTH, 1, n))


def _proj_kernel(x_ref, g_ref, sc_ref, sh_ref, w_ref, *rest, has_small):
    if has_small:
        ws_ref, o_ref, os_ref, hm_ref = rest
    else:
        o_ref, hm_ref = rest
    j = pl.program_id(1)

    @pl.when(j == 0)
    def _():
        y = _rms(x_ref[...]) * g_ref[...]
        hm = (y * (1.0 + sc_ref[0]) + sh_ref[0]).astype(bf16)
        hm_ref[...] = hm
        if has_small:
            os_ref[...] = _nt(ws_ref[...], hm)

    o_ref[...] = _mm(hm_ref[...], w_ref[...])


def norm_proj(x, gamma, sc, sh, mod_map, w, ws_t, tm, tn):
    t = x.shape[0]
    n = w.shape[1]
    r = sc.shape[1]
    has_small = ws_t is not None
    in_specs = [
        pl.BlockSpec((tm, D_MODEL), lambda i, j: (i, 0)),
        pl.BlockSpec((1, D_MODEL), lambda i, j: (0, 0)),
        pl.BlockSpec((1, r, D_MODEL), lambda i, j: mod_map(i)),
        pl.BlockSpec((1, r, D_MODEL), lambda i, j: mod_map(i)),
        pl.BlockSpec((D_MODEL, tn), lambda i, j: (0, j)),
    ]
    out_specs = [pl.BlockSpec((tm, tn), lambda i, j: (i, j))]
    out_shape = [jax.ShapeDtypeStruct((t, n), f32)]
    args = [x, gamma.reshape(1, D_MODEL), sc, sh, w]
    if has_small:
        s = ws_t.shape[0]
        in_specs.append(pl.BlockSpec((s, D_MODEL), lambda i, j: (0, 0)))
        out_specs.append(pl.BlockSpec((s, tm), lambda i, j: (0, i)))
        out_shape.append(jax.ShapeDtypeStruct((s, t), f32))
        args.append(ws_t)
    res = pl.pallas_call(
        functools.partial(_proj_kernel, has_small=has_small),
        grid=(t // tm, n // tn),
        in_specs=in_specs,
        out_specs=out_specs,
        out_shape=out_shape,
        scratch_shapes=[pltpu.VMEM((tm, D_MODEL), bf16)],
        compiler_params=_params(("arbitrary", "arbitrary")),
        name="norm_proj",
    )(*args)
    return res if has_small else res[0]


def _out_kernel(o_ref, w_ref, x_ref, g_ref, y_ref):
    y_ref[...] = x_ref[...] + g_ref[0] * _mm(o_ref[...], w_ref[...])


def out_proj(o, w, x, gate, mod_map, tm):
    t, k = o.shape
    r = gate.shape[1]
    return pl.pallas_call(
        _out_kernel,
        grid=(t // tm,),
        in_specs=[
            pl.BlockSpec((tm, k), lambda i: (i, 0)),
            pl.BlockSpec((k, D_MODEL), lambda i: (0, 0)),
            pl.BlockSpec((tm, D_MODEL), lambda i: (i, 0)),
            pl.BlockSpec((1, r, D_MODEL), lambda i: mod_map(i)),
        ],
        out_specs=pl.BlockSpec((tm, D_MODEL), lambda i: (i, 0)),
        out_shape=jax.ShapeDtypeStruct((t, D_MODEL), f32),
        compiler_params=_params(("arbitrary",)),
        name="out_proj",
    )(o, w, x, gate)


def _unit_lower_inverse(lm, c):
    eye = (lax.broadcasted_iota(jnp.int32, (c, c), 0) == lax.broadcasted_iota(jnp.int32, (c, c), 1)).astype(f32)
    p = eye - lm
    lp = lm
    k = 2
    while k < c:
        lp = _bmm3(lp, lp)
        p = p + _bmm3(p, lp)
        k *= 2
    return p


def _ab_kernel(proj_ref, smt_ref, conv0_ref, gdn0_ref, ssm0_ref, convw_ref, convb_ref, pcol_ref, gng_ref,
               dskip_ref, sng_ref, o_ref, convn_ref, gdnn_ref, ssmn_ref, xbuf, act, sg, ss, ob, *, c, lv):
    l = pl.program_id(1)
    nl = pl.num_programs(1)

    @pl.when(l == 0)
    def _():
        xbuf[...] = jnp.zeros(xbuf.shape, f32)
        xbuf[5:8, :] = conv0_ref[0]
        sg[...] = gdn0_ref[0]
        ss[...] = ssm0_ref[0]

    xbuf[8:8 + lv, :] = proj_ref[0, :, 0:CONV_CH]
    y = convb_ref[...]
    for i in range(CONV_W):
        y = y + convw_ref[i:i + 1, :] * xbuf[5 + i:5 + i + c, :]
    act[...] = _silu(y)
    tail = xbuf[5 + lv:8 + lv, :]
    xbuf[5:8, :] = tail

    if lv == c:
        sm = smt_ref[0, 0]
    else:
        lane = lax.broadcasted_iota(jnp.int32, (AB_SMALL, c), 1)
        sm = jnp.where(lane < lv, jnp.broadcast_to(smt_ref[0, 0], (AB_SMALL, c)), 0.0)
    valid = lax.broadcasted_iota(jnp.int32, (AB_SMALL, c), 1) < lv
    bias_col = pcol_ref[:, 0:1]
    alog_col = pcol_ref[:, 1:2]
    beta_t = jnp.where(valid, jax.nn.sigmoid(sm), 0.0)
    sp_t = jnp.where(valid, jax.nn.softplus(sm + bias_col), 0.0)
    la_t = -jnp.exp(alog_col) * sp_t
    rr = lax.broadcasted_iota(jnp.int32, (c, c), 0)
    qq = lax.broadcasted_iota(jnp.int32, (c, c), 1)
    triu = (rr <= qq).astype(f32)
    eye = (rr == qq).astype(f32)
    cum_t = _mm(la_t, triu, precision=HI)
    last_t = cum_t[:, c - 1:c]
    ecum_t = jnp.exp(cum_t)
    elc_t = jnp.exp(last_t - cum_t)
    elast_t = jnp.exp(last_t)
    g0, g1, s0, s1 = 0, GDN_HEADS, 2 * GDN_HEADS, AB_SMALL
    rows = jnp.concatenate([
        beta_t[g0:g1],
        cum_t[g1:s0],
        ecum_t[g1:s0],
        beta_t[g0:g1] * ecum_t[g1:s0],
        elc_t[g1:s0],
        cum_t[s0:s1],
        ecum_t[s0:s1],
        sp_t[s0:s1] * elc_t[s0:s1],
        jnp.zeros((128 - 88, c), f32),
    ], axis=0)
    cols = _nt(eye, rows, precision=HI)
    incl = rr >= qq
    strict = rr > qq

    def colstack(base, n):
        return jnp.stack([cols[:, base + h:base + h + 1] for h in range(n)])

    def rowstack(x, base, n):
        return jnp.stack([x[base + h:base + h + 1, :] for h in range(n)])

    hs = range(GDN_HEADS)
    q = jnp.stack([act[:, h * GDN_DK:(h + 1) * GDN_DK] for h in hs])
    k = jnp.stack([act[:, GDN_QK + h * GDN_DK:GDN_QK + (h + 1) * GDN_DK] for h in hs])
    v = jnp.stack([act[:, 2 * GDN_QK + h * GDN_DV:2 * GDN_QK + (h + 1) * GDN_DV] for h in hs])
    q = q * lax.rsqrt(jnp.sum(q * q, axis=-1, keepdims=True) + EPS) * (GDN_DK ** -0.5)
    k = k * lax.rsqrt(jnp.sum(k * k, axis=-1, keepdims=True) + EPS)
    beta_c, cum_c, ecum_c, becum_c, elc_c = (colstack(b, GDN_HEADS) for b in (0, 8, 16, 24, 32))
    dec = jnp.exp(jnp.where(incl, cum_c - rowstack(cum_t, g1, GDN_HEADS), -jnp.inf))
    kb = k.astype(bf16)
    qb = q.astype(bf16)
    lm = jnp.where(strict, beta_c * _bnt(kb, kb) * dec, 0.0)
    pinv = _unit_lower_inverse(lm, c)
    rhs = jnp.concatenate([v * beta_c, k * becum_c], axis=2)
    sol = _bmm3(pinv, rhs)
    u0 = sol[:, :, :GDN_DV]
    w = sol[:, :, GDN_DV:]
    qk = _bnt(qb, kb) * dec
    s_old = sg[...]
    sb = s_old.astype(bf16)
    u = u0 - _bmm(w.astype(bf16), sb)
    ub = u.astype(bf16)
    o = _bmm((q * ecum_c).astype(bf16), sb) + _bmm(qk.astype(bf16), ub)
    kd = (k * elc_c).astype(bf16)
    sg[...] = s_old * rowstack(elast_t, g1, GDN_HEADS) + _btn(kd, ub)
    oa = _rms(o) * gng_ref[...]
    for h in hs:
        gate = proj_ref[0, :, CONV_CH + h * GDN_DV:CONV_CH + (h + 1) * GDN_DV]
        o_ref[0, :, h * GDN_DV:(h + 1) * GDN_DV] = (oa[h][0:lv] * _silu(gate)).astype(bf16)

    rep = SSM_HEADS // SSM_G
    xs0 = 2 * GDN_QK + GDN_V
    bm0 = xs0 + SSM_INNER
    cm0 = bm0 + SSM_BC
    hs = range(SSM_HEADS)
    bmb = [act[:, bm0 + g * SSM_N:bm0 + (g + 1) * SSM_N].astype(bf16) for g in range(SSM_G)]
    cmb = [act[:, cm0 + g * SSM_N:cm0 + (g + 1) * SSM_N].astype(bf16) for g in range(SSM_G)]
    cbg = [_nt(cmb[g], bmb[g]) for g in range(SSM_G)]
    cb = jnp.stack([cbg[h // rep] for h in hs])
    bm16 = jnp.stack([bmb[h // rep] for h in hs])
    cm16 = jnp.stack([cmb[h // rep] for h in hs])
    xs = jnp.stack([act[:, xs0 + h * SSM_P:xs0 + (h + 1) * SSM_P] for h in hs])
    dsk = jnp.stack([dskip_ref[:, h * SSM_P:(h + 1) * SSM_P] for h in hs])
    cum_c, ecum_c, dtelc_c = (colstack(b, SSM_HEADS) for b in (40, 56, 72))
    m = cb * jnp.exp(jnp.where(incl, cum_c - rowstack(cum_t, s0, SSM_HEADS), -jnp.inf)) * rowstack(sp_t, s0, SSM_HEADS)
    s_old = ss[...]
    o = _bmm(m.astype(bf16), xs.astype(bf16)) + _bmm(cm16, s_old.astype(bf16)) * ecum_c + dsk * xs
    ss[...] = s_old * rowstack(elast_t, s0, SSM_HEADS) + _btn(bm16, (xs * dtelc_c).astype(bf16))
    for h in hs:
        ob[:, h * SSM_P:(h + 1) * SSM_P] = o[h]
    z = proj_ref[0, :, CONV_CH + GDN_V:CONV_CH + GDN_V + SSM_INNER]
    obv = _rms(ob[0:lv, :] * _silu(z)) * sng_ref[...]
    o_ref[0, :, GDN_V:GDN_V + SSM_INNER] = obv.astype(bf16)

    @pl.when(l == nl - 1)
    def _():
        convn_ref[0] = xbuf[5:8, :]
        gdnn_ref[0] = sg[...]
        ssmn_ref[0] = ss[...]


def mixer_ab(proj, smt, conv0, gdn0, ssm0, conv_w, conv_b, pcol, gdn_norm_g, dskip, ssm_norm_g, c):
    b, l, npad = proj.shape
    tl = min(c, l)
    nl = l // tl
    kern = functools.partial(_ab_kernel, c=c, lv=tl)
    full = lambda shape: pl.BlockSpec(shape, lambda i, j: (0,) * len(shape))
    return pl.pallas_call(
        kern,
        grid=(b, nl),
        in_specs=[
            pl.BlockSpec((1, tl, npad), lambda i, j: (i, j, 0)),
            pl.BlockSpec((1, 1, AB_SMALL, tl), lambda i, j: (i, j, 0, 0)),
            pl.BlockSpec((1, CONV_W - 1, CONV_CH), lambda i, j: (i, 0, 0)),
            pl.BlockSpec((1, GDN_HEADS, GDN_DK, GDN_DV), lambda i, j: (i, 0, 0, 0)),
            pl.BlockSpec((1, SSM_HEADS, SSM_N, SSM_P), lambda i, j: (i, 0, 0, 0)),
            full((CONV_W, CONV_CH)),
            full((1, CONV_CH)),
            full((AB_SMALL, 2)),
            full((1, GDN_DV)),
            full((1, SSM_INNER)),
            full((1, SSM_INNER)),
        ],
        out_specs=[
            pl.BlockSpec((1, tl, GDN_V + SSM_INNER), lambda i, j: (i, j, 0)),
            pl.BlockSpec((1, CONV_W - 1, CONV_CH), lambda i, j: (i, 0, 0)),
            pl.BlockSpec((1, GDN_HEADS, GDN_DK, GDN_DV), lambda i, j: (i, 0, 0, 0)),
            pl.BlockSpec((1, SSM_HEADS, SSM_N, SSM_P), lambda i, j: (i, 0, 0, 0)),
        ],
        out_shape=[
            jax.ShapeDtypeStruct((b, l, GDN_V + SSM_INNER), bf16),
            jax.ShapeDtypeStruct((b, CONV_W - 1, CONV_CH), f32),
            jax.ShapeDtypeStruct((b, GDN_HEADS, GDN_DK, GDN_DV), f32),
            jax.ShapeDtypeStruct((b, SSM_HEADS, SSM_N, SSM_P), f32),
        ],
        scratch_shapes=[
            pltpu.VMEM((8 + c, CONV_CH), f32),
            pltpu.VMEM((c, CONV_CH), f32),
            pltpu.VMEM((GDN_HEADS, GDN_DK, GDN_DV), f32),
            pltpu.VMEM((SSM_HEADS, SSM_N, SSM_P), f32),
            pltpu.VMEM((c, SSM_INNER), f32),
        ],
        compiler_params=_params(("arbitrary", "arbitrary")),
        name="mixer_ab",
    )(proj, smt, conv0, gdn0, ssm0, conv_w, conv_b, pcol, gdn_norm_g, dskip, ssm_norm_g)


def _ret_kernel(proj_ref, cos_ref, sin_ref, ret0_ref, ng_ref, o_ref, retn_ref, st, buf, *, c, lv):
    l = pl.program_id(1)
    nl = pl.num_programs(1)

    @pl.when(l == 0)
    def _():
        st[...] = ret0_ref[0]
        if lv < c:
            buf[...] = jnp.zeros(buf.shape, f32)

    if lv == c:
        src = proj_ref.at[0]
        cos = cos_ref[...]
        sin = sin_ref[...]
    else:
        buf[0:lv, :] = proj_ref[0]
        src = buf
        cos = jnp.broadcast_to(cos_ref[...], (c, RET_DK // 2))
        sin = jnp.broadcast_to(sin_ref[...], (c, RET_DK // 2))

    ri = lax.broadcasted_iota(jnp.int32, (c, c), 0)
    ci = lax.broadcasted_iota(jnp.int32, (c, c), 1)
    incl = ri >= ci
    cnt_r = jnp.minimum(ri + 1, lv).astype(f32)
    cnt_c = jnp.minimum(ci + 1, lv).astype(f32)
    cnt_col = jnp.minimum(lax.broadcasted_iota(jnp.int32, (c, 1), 0) + 1, lv).astype(f32)
    half = RET_DK // 2

    def rope(x):
        x1, x2 = x[:, :half], x[:, half:]
        return jnp.concatenate([x1 * cos - x2 * sin, x1 * sin + x2 * cos], axis=1)

    for h in range(RET_HEADS):
        lg = math.log(1.0 - 2.0 ** (-5.0 - h))
        q = rope(src[:, h * RET_DK:(h + 1) * RET_DK])
        k = rope(src[:, RET_QK + h * RET_DK:RET_QK + (h + 1) * RET_DK]) * (RET_DK ** -0.5)
        v = src[:, 2 * RET_QK + h * RET_DV:2 * RET_QK + (h + 1) * RET_DV].astype(bf16)
        gate = src[:, 2 * RET_QK + RET_V + h * RET_DV:2 * RET_QK + RET_V + (h + 1) * RET_DV]
        dec = jnp.exp(jnp.where(incl, (cnt_r - cnt_c) * lg, -jnp.inf))
        scores = _nt(q.astype(bf16), k.astype(bf16)) * dec
        s_old = st[h]
        o = _mm(scores.astype(bf16), v) + _mm((q * jnp.exp(cnt_col * lg)).astype(bf16), s_old.astype(bf16))
        kd = (k * jnp.exp((lv - cnt_col) * lg)).astype(bf16)
        st[h] = s_old * math.exp(lv * lg) + _tn(kd, v)
        mu = jnp.mean(o, axis=-1, keepdims=True)
        var = jnp.mean(jnp.square(o - mu), axis=-1, keepdims=True)
        o = (o - mu) * lax.rsqrt(var + EPS) * ng_ref[h:h + 1, :]
        o_ref[0, :, h * RET_DV:(h + 1) * RET_DV] = (_silu(gate) * o)[0:lv].astype(bf16)

    @pl.when(l == nl - 1)
    def _():
        retn_ref[0] = st[...]


def mixer_ret(proj, cos, sin, ret0, norm_g, c):
    b, l, n = proj.shape
    tl = min(c, l)
    nl = l // tl
    kern = functools.partial(_ret_kernel, c=c, lv=tl)
    return pl.pallas_call(
        kern,
        grid=(b, nl),
        in_specs=[
            pl.BlockSpec((1, tl, n), lambda i, j: (i, j, 0)),
            pl.BlockSpec((tl, RET_DK // 2), lambda i, j: (j, 0)),
            pl.BlockSpec((tl, RET_DK // 2), lambda i, j: (j, 0)),
            pl.BlockSpec((1, RET_HEADS, RET_DK, RET_DV), lambda i, j: (i, 0, 0, 0)),
            pl.BlockSpec((RET_HEADS, RET_DV), lambda i, j: (0, 0)),
        ],
        out_specs=[
            pl.BlockSpec((1, tl, RET_V), lambda i, j: (i, j, 0)),
            pl.BlockSpec((1, RET_HEADS, RET_DK, RET_DV), lambda i, j: (i, 0, 0, 0)),
        ],
        out_shape=[
            jax.ShapeDtypeStruct((b, l, RET_V), bf16),
            jax.ShapeDtypeStruct((b, RET_HEADS, RET_DK, RET_DV), f32),
        ],
        scratch_shapes=[
            pltpu.VMEM((RET_HEADS, RET_DK, RET_DV), f32),
            pltpu.VMEM((c, n), f32),
        ],
        compiler_params=_params(("arbitrary", "arbitrary")),
        name="mixer_ret",
    )(proj, cos, sin, ret0, norm_g)


def _top16(s, exact):
    g, n, t = s.shape
    rows = lax.broadcasted_iota(jnp.int32, (g, n, t), 1)
    krow = lax.broadcasted_iota(jnp.int32, (g, PEER_TOPK, t), 1)
    rank = jnp.full((g, n, t), float(PEER_TOPK), f32)
    sv = jnp.zeros((g, PEER_TOPK, t), f32)
    work = s
    for k in range(PEER_TOPK):
        m = jnp.max(work, axis=1, keepdims=True)
        if exact:
            idx = jnp.min(jnp.where(work == m, rows, n), axis=1, keepdims=True)
            sel = rows == idx
        else:
            sel = work == m
        rank = jnp.where(sel, float(k), rank)
        work = jnp.where(sel, -jnp.inf, work)
        sv = jnp.where(krow == k, m, sv)
    return sv, rank


def _miscount(rank):
    cnt = jnp.sum(jnp.where(rank < float(PEER_TOPK), 1.0, 0.0), axis=1, keepdims=True)
    return jnp.max(jnp.abs(cnt - float(PEER_TOPK)))


def _topk_kernel(x_ref, g_ref, sc_ref, sh_ref, wqt_ref, keys_ref, a1_ref, a2_ref, a1t_ref,
                 hmt_ref, r2_ref, e2_ref, lc_ref, g1_ref, sv_scr, rk_scr, rc_scr):
    y = _rms(x_ref[...]) * g_ref[...]
    hm = y * (1.0 + sc_ref[0]) + sh_ref[0]
    hmt = hm.T.astype(bf16)
    hmt_ref[...] = hmt
    qt = _mm(wqt_ref[...], hmt)
    tt = qt.shape[1]
    half = PEER_DQ // 2
    hs = range(PEER_HEADS)
    s_all = jnp.stack(
        [_mm(keys_ref[0], qt[h * PEER_DQ:h * PEER_DQ + half, :].astype(bf16)) for h in hs]
        + [_mm(keys_ref[1], qt[h * PEER_DQ + half:(h + 1) * PEER_DQ, :].astype(bf16)) for h in hs])
    keys_rows = slice(0, PEER_NKEYS)
    pad = jnp.zeros((PEER_HEADS, PEER_NKP - PEER_NKEYS, 128), f32)
    crow = lax.broadcasted_iota(jnp.int32, (PEER_HEADS, _NCAND_PAD, 128), 1)

    def candidates(sv):
        cand = jnp.stack([_mm(a1_ref[...], sv[h], precision=HI) + _mm(a2_ref[...], sv[PEER_HEADS + h], precision=HI)
                          for h in hs])
        return jnp.where(crow < _NCAND, cand, -jnp.inf)

    for lg in range(tt // 128):
        ls = slice(lg * 128, (lg + 1) * 128)
        s_blk = s_all[:, :, ls]

        def select(exact):
            sv, rank = _top16(s_blk, exact)
            _, rankc = _top16(candidates(sv), exact)
            sv_scr[:, :, ls] = sv
            rk_scr[:, :, ls] = rank
            rc_scr[:, :, ls] = rankc
            return jnp.maximum(_miscount(rank), _miscount(rankc))

        bad = select(False)

        @pl.when(bad > 0.0)
        def _():
            select(True)

        sv = sv_scr[:, :, ls]
        sv1, sv2 = sv[:PEER_HEADS], sv[PEER_HEADS:]
        rank1 = rk_scr[0:PEER_HEADS, :, ls]
        sel_all = jnp.where(rc_scr[:, :, ls] < float(PEER_TOPK), 1.0, 0.0)
        cand = candidates(sv)
        z = jnp.sum(sel_all * jnp.exp(jnp.where(crow < _NCAND, cand - cand[:, 0:1, :], 0.0)), axis=1, keepdims=True)
        selb = sel_all.astype(bf16)
        cnt = jnp.stack([_mm(a1t_ref[...], selb[h]) for h in hs])
        lc = jnp.zeros((PEER_HEADS, PEER_NKEYS, 128), f32)
        for k1 in range(PEER_TOPK):
            lc = jnp.where(rank1 == float(k1), cnt[:, k1:k1 + 1, :], lc)
        r2_ref[:, keys_rows, ls] = rk_scr[PEER_HEADS:2 * PEER_HEADS, :, ls]
        e2_ref[:, keys_rows, ls] = jnp.exp(s_blk[PEER_HEADS:] - sv2[:, 0:1, :])
        lc_ref[:, keys_rows, ls] = lc
        g1_ref[:, keys_rows, ls] = jnp.exp(s_blk[:PEER_HEADS] - sv1[:, 0:1, :]) / z
        for ref in (r2_ref, e2_ref, lc_ref, g1_ref):
            ref[:, PEER_NKEYS:, ls] = pad


def peer_topk(x, gamma, sc, sh, mod_map, wq_t, keys, tt):
    t = x.shape[0]
    r = sc.shape[1]
    a1 = jnp.zeros((_NCAND_PAD, PEER_TOPK), f32).at[jnp.arange(_NCAND), jnp.array([a for a, _ in _CAND])].set(1.0)
    a2 = jnp.zeros((_NCAND_PAD, PEER_TOPK), f32).at[jnp.arange(_NCAND), jnp.array([b for _, b in _CAND])].set(1.0)
    full = lambda shape: pl.BlockSpec(shape, lambda i: (0,) * len(shape))
    tok = pl.BlockSpec((PEER_HEADS, PEER_NKP, tt), lambda i: (0, 0, i))
    tok_shape = jax.ShapeDtypeStruct((PEER_HEADS, PEER_NKP, t), f32)
    return pl.pallas_call(
        _topk_kernel,
        grid=(t // tt,),
        in_specs=[
            pl.BlockSpec((tt, D_MODEL), lambda i: (i, 0)),
            full((1, D_MODEL)),
            pl.BlockSpec((1, r, D_MODEL), lambda i: mod_map(i)),
            pl.BlockSpec((1, r, D_MODEL), lambda i: mod_map(i)),
            full((PEER_HEADS * PEER_DQ, D_MODEL)),
            full((2, PEER_NKEYS, PEER_DQ // 2)),
            full((_NCAND_PAD, PEER_TOPK)),
            full((_NCAND_PAD, PEER_TOPK)),
            full((PEER_TOPK, _NCAND_PAD)),
        ],
        out_specs=[pl.BlockSpec((D_MODEL, tt), lambda i: (0, i)), tok, tok, tok, tok],
        out_shape=[jax.ShapeDtypeStruct((D_MODEL, t), bf16), tok_shape, tok_shape, tok_shape, tok_shape],
        scratch_shapes=[
            pltpu.VMEM((2 * PEER_HEADS, PEER_TOPK, tt), f32),
            pltpu.VMEM((2 * PEER_HEADS, PEER_NKEYS, tt), f32),
            pltpu.VMEM((PEER_HEADS, _NCAND_PAD, tt), f32),
        ],
        compiler_params=_params(("arbitrary",)),
        name="peer_topk",
    )(x, gamma.reshape(1, D_MODEL), sc, sh, wq_t, keys, a1, a2, a1.T.astype(bf16))


def _peer_kernel(hmt_ref, u_ref, vt_ref, r2_ref, e2_ref, lc_ref, g1_ref, x_ref, g_ref, fg_ref, o_ref,
                 yt, act_a, act_b, ht_a, ht_b, *, nsub, final):
    j = pl.program_id(1)
    nj = pl.num_programs(1)
    na = PEER_SB // PEER_NKEYS
    tt = hmt_ref.shape[1]

    @pl.when(j == 0)
    def _():
        yt[...] = jnp.zeros(yt.shape, f32)

    def mm1(s, ht):
        ht[...] = _mm(u_ref[s], hmt_ref[...])

    def gate_act(s, ht, act):
        for a in range(na):
            n1 = (j * nsub + s) * na + a
            lrows = [lc_ref[h, pl.ds(n1, 1), :] for h in range(PEER_HEADS)]
            grows = [g1_ref[h, pl.ds(n1, 1), :] for h in range(PEER_HEADS)]
            for lg in range(tt // 128):
                ls = slice(lg * 128, (lg + 1) * 128)
                w = None
                for h in range(PEER_HEADS):
                    wh = jnp.where(r2_ref[h, 0:PEER_NKEYS, ls] < lrows[h][:, ls], e2_ref[h, 0:PEER_NKEYS, ls] * grows[h][:, ls], 0.0)
                    w = wh if w is None else w + wh
                hb = ht[a * PEER_NKEYS:(a + 1) * PEER_NKEYS, ls]
                act[a * PEER_NKEYS:(a + 1) * PEER_NKEYS, ls] = (_gelu(hb) * w).astype(bf16)

    def mm2(s, act):
        yt[...] += _mm(vt_ref[s], act[...])

    act_b[...] = jnp.zeros(act_b.shape, bf16)
    mm1(0, ht_a)

    def body(i, carry):
        s = 2 * i
        mm1(s + 1, ht_b)
        gate_act(s, ht_a, act_a)
        mm2(jnp.maximum(s - 1, 0), act_b)
        mm1(jnp.minimum(s + 2, nsub - 1), ht_a)
        gate_act(s + 1, ht_b, act_b)
        mm2(s, act_a)
        return carry

    lax.fori_loop(0, nsub // 2, body, 0)
    mm2(nsub - 1, act_b)

    @pl.when(j == nj - 1)
    def _():
        xn = x_ref[...] + g_ref[0] * yt[...].T
        if final:
            xn = _rms(xn) * fg_ref[...]
        o_ref[...] = xn


def peer_dense(hmt, u3, vt3, r2, e2, lc, g1, x, gate, mod_map, final_g, tt, eb):
    t = x.shape[0]
    r = gate.shape[1]
    nsub = eb // PEER_SB
    final = final_g is not None
    fg = (final_g if final else jnp.ones((D_MODEL,), f32)).reshape(1, D_MODEL)
    tok = pl.BlockSpec((PEER_HEADS, PEER_NKP, tt), lambda i, j: (0, 0, i))
    return pl.pallas_call(
        functools.partial(_peer_kernel, nsub=nsub, final=final),
        grid=(t // tt, N_EXPERTS // eb),
        in_specs=[
            pl.BlockSpec((D_MODEL, tt), lambda i, j: (0, i)),
            pl.BlockSpec((nsub, PEER_SB, D_MODEL), lambda i, j: (j, 0, 0)),
            pl.BlockSpec((nsub, D_MODEL, PEER_SB), lambda i, j: (j, 0, 0)),
            tok, tok, tok, tok,
            pl.BlockSpec((tt, D_MODEL), lambda i, j: (i, 0)),
            pl.BlockSpec((1, r, D_MODEL), lambda i, j: mod_map(i)),
            pl.BlockSpec((1, D_MODEL), lambda i, j: (0, 0)),
        ],
        out_specs=pl.BlockSpec((tt, D_MODEL), lambda i, j: (i, 0)),
        out_shape=jax.ShapeDtypeStruct((t, D_MODEL), f32),
        scratch_shapes=[
            pltpu.VMEM((D_MODEL, tt), f32),
            pltpu.VMEM((PEER_SB, tt), bf16), pltpu.VMEM((PEER_SB, tt), bf16),
            pltpu.VMEM((PEER_SB, tt), f32), pltpu.VMEM((PEER_SB, tt), f32),
        ],
        compiler_params=_params(("arbitrary", "arbitrary")),
        name="peer_dense",
    )(hmt, u3, vt3, r2, e2, lc, g1, x, gate, fg)


def _prepare(ab_w_in, ab_conv_w, ab_conv_b, gdn_a_log, gdn_dt_bias, gdn_norm_g, ssm_a_log, ssm_dt_bias, ssm_d,
             ssm_norm_g, ab_w_out, ret_w_in, ret_norm_g, ret_w_out, peer_w_q, peer_keys, peer_u, peer_v):
    zeros8 = jnp.zeros((GDN_HEADS,), f32)
    nsb = N_EXPERTS // PEER_SB
    return dict(
        ab_w_main=ab_w_in[0][:, :AB_MAIN].astype(bf16),
        ab_w_small_t=ab_w_in[0][:, AB_MAIN:].T.astype(bf16),
        conv_w=ab_conv_w[0],
        conv_b=ab_conv_b[0].reshape(1, CONV_CH),
        pcol=jnp.stack([jnp.concatenate([zeros8, gdn_dt_bias[0], ssm_dt_bias[0]]),
                        jnp.concatenate([zeros8, gdn_a_log[0], ssm_a_log[0]])], axis=1),
        gdn_norm_g=gdn_norm_g[0].reshape(1, GDN_DV),
        dskip=jnp.repeat(ssm_d[0], SSM_P).reshape(1, SSM_INNER),
        ssm_norm_g=ssm_norm_g[0].reshape(1, SSM_INNER),
        ab_w_out=ab_w_out[0].astype(bf16),
        ret_w_in=ret_w_in[0].astype(bf16),
        ret_norm_g=ret_norm_g[0],
        ret_w_out=ret_w_out[0].astype(bf16),
        wq_t=[peer_w_q[i].T.astype(bf16) for i in range(DEPTH)],
        keys=[peer_keys[i].astype(bf16) for i in range(DEPTH)],
        u=[peer_u[i].astype(bf16).reshape(nsb, PEER_SB, D_MODEL) for i in range(DEPTH)],
        vt=[peer_v[i].astype(bf16).reshape(nsb, PEER_SB, D_MODEL).transpose(0, 2, 1) for i in range(DEPTH)],
    )


def _rope_tables(pos0, length):
    inv = ROPE_BASE ** (-jnp.arange(0, RET_DK, 2, dtype=f32) / RET_DK)
    ang = (pos0 + jnp.arange(length, dtype=f32))[:, None] * inv[None, :]
    return jnp.cos(ang), jnp.sin(ang)


def _trunk(x, mods, pos0, conv0, gdn0, ssm0, ret0, p, norm1_g, norm2_g, final_g):
    b, l, _ = x.shape
    t = b * l
    c = CHUNK if l >= CHUNK else 8
    xt = x.reshape(t, D_MODEL)
    if l > 1:
        tm = min(512, l)
        tt = min(256, l)
        mod_arr = lambda m: m.reshape(b, 1, D_MODEL)
        mod_map = lambda rows: (lambda i: (i * rows // l, 0, 0))
    else:
        tm = t
        tt = t
        mod_arr = lambda m: m.reshape(1, t, D_MODEL)
        mod_map = lambda rows: (lambda i: (0, i, 0))
    convs = gdns = ssms = rets = None
    for layer in range(DEPTH):
        sh1, sc1, g1, sh2, sc2, g2 = (mod_arr(m) for m in jnp.split(mods[layer], 6, axis=-1))
        if layer == 0:
            proj, smt = norm_proj(xt, norm1_g[layer], sc1, sh1, mod_map(tm), p["ab_w_main"], p["ab_w_small_t"],
                                  tm, AB_MAIN // 4)
            tl = min(c, l)
            smt = smt.reshape(AB_SMALL, b, l // tl, tl).transpose(1, 2, 0, 3)
            o, convs, gdns, ssms = mixer_ab(proj.reshape(b, l, -1), smt, conv0, gdn0, ssm0, p["conv_w"], p["conv_b"],
                                            p["pcol"], p["gdn_norm_g"], p["dskip"], p["ssm_norm_g"], c)
            xt = out_proj(o.reshape(t, -1), p["ab_w_out"], xt, g1, mod_map(tm), tm)
        else:
            proj = norm_proj(xt, norm1_g[layer], sc1, sh1, mod_map(tm), p["ret_w_in"], None, tm, RET_IN // 4)
            cos, sin = _rope_tables(pos0, l)
            o, rets = mixer_ret(proj.reshape(b, l, -1), cos, sin, ret0, p["ret_norm_g"], c)
            xt = out_proj(o.reshape(t, -1), p["ret_w_out"], xt, g1, mod_map(tm), tm)
        hmt, r2, e2, lc, gg = peer_topk(xt, norm2_g[layer], sc2, sh2, mod_map(tt), p["wq_t"][layer], p["keys"][layer], tt)
        xt = peer_dense(hmt, p["u"][layer], p["vt"][layer], r2, e2, lc, gg, xt, g2, mod_map(tt),
                        final_g if layer == DEPTH - 1 else None, tt, PEER_EB)
    return xt.reshape(b, l, D_MODEL), convs[None], gdns[None], ssms[None], rets[None]


def kernel(x_prompt, x_sample, c_prompt, c_sample, state_conv, state_gdn, state_ssm, state_ret, ada_w, ada_b,
           norm1_g, norm2_g, ab_w_in, ab_conv_w, ab_conv_b, gdn_a_log, gdn_dt_bias, gdn_norm_g, ssm_a_log,
           ssm_dt_bias, ssm_d, ssm_norm_g, ab_w_out, ret_w_in, ret_norm_g, ret_w_out, peer_w_q, peer_keys,
           peer_u, peer_v, final_g):
    p = _prepare(ab_w_in, ab_conv_w, ab_conv_b, gdn_a_log, gdn_dt_bias, gdn_norm_g, ssm_a_log, ssm_dt_bias, ssm_d,
                 ssm_norm_g, ab_w_out, ret_w_in, ret_norm_g, ret_w_out, peer_w_q, peer_keys, peer_u, peer_v)
    nb = x_prompt.shape[0]
    mods = ada_mod(jnp.concatenate([c_prompt, c_sample], axis=0), ada_w, ada_b)
    zeros = lambda s: jnp.zeros((nb,) + s.shape[2:], s.dtype)
    y_p, p_conv, p_gdn, p_ssm, p_ret = _trunk(
        x_prompt, mods[:, :nb], 0, zeros(state_conv), zeros(state_gdn), zeros(state_ssm), zeros(state_ret),
        p, norm1_g, norm2_g, final_g)
    y_s, s_conv, s_gdn, s_ssm, s_ret = _trunk(
        x_sample, mods[:, nb:], PAST_LEN, state_conv[0], state_gdn[0], state_ssm[0], state_ret[0],
        p, norm1_g, norm2_g, final_g)
    return (y_p, y_s, p_conv, p_gdn, p_ssm, p_ret, s_conv, s_gdn, s_ssm, s_ret)
```

```python
import functools
import math

import jax
import jax.numpy as jnp
from jax import lax
from jax.experimental import pallas as pl
from jax.experimental.pallas import tpu as pltpu

f32 = jnp.float32
bf16 = jnp.bfloat16
HI = lax.Precision.HIGHEST

D_MODEL = 1024
DEPTH = 2
PAST_LEN = 16384
GDN_HEADS = 8
GDN_DK = 128
GDN_DV = 128
SSM_HEADS = 16
SSM_P = 64
SSM_N = 128
SSM_G = 2
CONV_W = 4
RET_HEADS = 4
RET_DK = 256
RET_DV = 512
ROPE_BASE = 10000.0
PEER_HEADS = 8
PEER_NKEYS = 128
PEER_TOPK = 16
PEER_DQ = 256
CHUNK = 64
EPS = 1e-6

GDN_QK = GDN_HEADS * GDN_DK
GDN_V = GDN_HEADS * GDN_DV
SSM_INNER = SSM_HEADS * SSM_P
SSM_BC = SSM_G * SSM_N
CONV_CH = 2 * GDN_QK + GDN_V + SSM_INNER + 2 * SSM_BC
AB_MAIN = CONV_CH + GDN_V + SSM_INNER
AB_SMALL = 2 * GDN_HEADS + SSM_HEADS
RET_QK = RET_HEADS * RET_DK
RET_V = RET_HEADS * RET_DV
RET_IN = 2 * RET_QK + 2 * RET_V
N_EXPERTS = PEER_NKEYS * PEER_NKEYS

VMEM_LIMIT = 56 * 1024 * 1024
PEER_SB = 256
PEER_EB = 4096
PEER_NKP = PEER_NKEYS + 8

_CAND = [(a, b) for a in range(PEER_TOPK) for b in range(PEER_TOPK) if (a + 1) * (b + 1) <= PEER_TOPK]
_NCAND = len(_CAND)
_NCAND_PAD = -(-_NCAND // 16) * 16


def _params(sem):
    return pltpu.CompilerParams(dimension_semantics=sem, vmem_limit_bytes=VMEM_LIMIT)


def _nt(a, b, **kw):
    return lax.dot_general(a, b, (((1,), (1,)), ((), ())), preferred_element_type=f32, **kw)


def _tn(a, b, **kw):
    return lax.dot_general(a, b, (((0,), (0,)), ((), ())), preferred_element_type=f32, **kw)


def _mm(a, b, **kw):
    return jnp.dot(a, b, preferred_element_type=f32, **kw)


def _bmm(a, b, **kw):
    return jnp.einsum('hik,hkj->hij', a, b, preferred_element_type=f32, **kw)


def _bnt(a, b, **kw):
    return jnp.einsum('hik,hjk->hij', a, b, preferred_element_type=f32, **kw)


def _btn(a, b, **kw):
    return jnp.einsum('hki,hkj->hij', a, b, preferred_element_type=f32, **kw)


def _split(x):
    hi = x.astype(bf16)
    return hi, (x - hi.astype(f32)).astype(bf16)


def _bmm3(a, b):
    ah, al = _split(a)
    bh, bl = _split(b)
    return _bmm(ah, bh) + (_bmm(ah, bl) + _bmm(al, bh))


def _silu(x):
    return x * jax.nn.sigmoid(x)


def _gelu(x):
    return 0.5 * x * (1.0 + lax.erf(x * (2.0 ** -0.5)))


def _rms(x):
    return x * lax.rsqrt(jnp.mean(x * x, axis=-1, keepdims=True) + EPS)


def _mod_kernel(c_ref, w_ref, b_ref, o_ref):
    a = _silu(c_ref[...]).astype(bf16)
    o_ref[0] = _mm(a, w_ref[0].astype(bf16)) + b_ref[0]


def ada_mod(c_all, ada_w, ada_b):
    m = c_all.shape[0]
    tn = 768
    n = ada_w.shape[-1]
    return pl.pallas_call(
        _mod_kernel,
        grid=(DEPTH, n // tn),
        in_specs=[
            pl.BlockSpec((m, D_MODEL), lambda l, j: (0, 0)),
            pl.BlockSpec((1, D_MODEL, tn), lambda l, j: (l, 0, j)),
            pl.BlockSpec((1, 1, tn), lambda l, j: (l, 0, j)),
        ],
        out_specs=pl.BlockSpec((1, m, tn), lambda l, j: (l, 0, j)),
        out_shape=jax.ShapeDtypeStruct((DEPTH, m, n), f32),
        compiler_params=_params(("arbitrary", "arbitrary")),
        name="ada_mod",
    )(c_all, ada_w, ada_b.reshape(DEPTH, 1, n))


def _proj_kernel(x_ref, g_ref, sc_ref, sh_ref, w_ref, *rest, has_small):
    if has_small:
        ws_ref, o_ref, os_ref, hm_ref = rest
    else:
        o_ref, hm_ref = rest
    j = pl.program_id(1)

    @pl.when(j == 0)
    def _():
        y = _rms(x_ref[...]) * g_ref[...]
        hm = (y * (1.0 + sc_ref[0]) + sh_ref[0]).astype(bf16)
        hm_ref[...] = hm
        if has_small:
            os_ref[...] = _nt(ws_ref[...], hm)

    o_ref[...] = _mm(hm_ref[...], w_ref[...]).astype(o_ref.dtype)


def norm_proj(x, gamma, sc, sh, mod_map, w, ws_t, tm, tn):
    t = x.shape[0]
    n = w.shape[1]
    r = sc.shape[1]
    has_small = ws_t is not None
    in_specs = [
        pl.BlockSpec((tm, D_MODEL), lambda i, j: (i, 0)),
        pl.BlockSpec((1, D_MODEL), lambda i, j: (0, 0)),
        pl.BlockSpec((1, r, D_MODEL), lambda i, j: mod_map(i)),
        pl.BlockSpec((1, r, D_MODEL), lambda i, j: mod_map(i)),
        pl.BlockSpec((D_MODEL, tn), lambda i, j: (0, j)),
    ]
    out_specs = [pl.BlockSpec((tm, tn), lambda i, j: (i, j))]
    out_shape = [jax.ShapeDtypeStruct((t, n), bf16)]
    args = [x, gamma.reshape(1, D_MODEL), sc, sh, w]
    if has_small:
        s = ws_t.shape[0]
        in_specs.append(pl.BlockSpec((s, D_MODEL), lambda i, j: (0, 0)))
        out_specs.append(pl.BlockSpec((s, tm), lambda i, j: (0, i)))
        out_shape.append(jax.ShapeDtypeStruct((s, t), f32))
        args.append(ws_t)
    res = pl.pallas_call(
        functools.partial(_proj_kernel, has_small=has_small),
        grid=(t // tm, n // tn),
        in_specs=in_specs,
        out_specs=out_specs,
        out_shape=out_shape,
        scratch_shapes=[pltpu.VMEM((tm, D_MODEL), bf16)],
        compiler_params=_params(("arbitrary", "arbitrary")),
        name="norm_proj",
    )(*args)
    return res if has_small else res[0]


def _out_kernel(o_ref, w_ref, x_ref, g_ref, y_ref):
    y_ref[...] = x_ref[...] + g_ref[0] * _mm(o_ref[...], w_ref[...])


def out_proj(o, w, x, gate, mod_map, tm):
    t, k = o.shape
    r = gate.shape[1]
    return pl.pallas_call(
        _out_kernel,
        grid=(t // tm,),
        in_specs=[
            pl.BlockSpec((tm, k), lambda i: (i, 0)),
            pl.BlockSpec((k, D_MODEL), lambda i: (0, 0)),
            pl.BlockSpec((tm, D_MODEL), lambda i: (i, 0)),
            pl.BlockSpec((1, r, D_MODEL), lambda i: mod_map(i)),
        ],
        out_specs=pl.BlockSpec((tm, D_MODEL), lambda i: (i, 0)),
        out_shape=jax.ShapeDtypeStruct((t, D_MODEL), f32),
        compiler_params=_params(("arbitrary",)),
        name="out_proj",
    )(o, w, x, gate)


def _unit_lower_inverse(lm, c):
    eye = (lax.broadcasted_iota(jnp.int32, (c, c), 0) == lax.broadcasted_iota(jnp.int32, (c, c), 1)).astype(f32)
    p = eye - lm
    lp = lm
    k = 2
    while k < c:
        lp = _bmm3(lp, lp)
        p = p + _bmm3(p, lp)
        k *= 2
    return p


def _ab_kernel(proj_ref, smt_ref, conv0_ref, gdn0_ref, ssm0_ref, convw_ref, convb_ref, pcol_ref, gng_ref,
               dskip_ref, sng_ref, o_ref, convn_ref, gdnn_ref, ssmn_ref, xbuf, act, sg, ss, ob, *, c, lv):
    l = pl.program_id(1)
    nl = pl.num_programs(1)

    @pl.when(l == 0)
    def _():
        xbuf[...] = jnp.zeros(xbuf.shape, f32)
        xbuf[5:8, :] = conv0_ref[0]
        sg[...] = gdn0_ref[0]
        ss[...] = ssm0_ref[0]

    xbuf[8:8 + lv, :] = proj_ref[0, :, 0:CONV_CH].astype(f32)
    y = convb_ref[...]
    for i in range(CONV_W):
        y = y + convw_ref[i:i + 1, :] * xbuf[5 + i:5 + i + c, :]
    act[...] = _silu(y)
    tail = xbuf[5 + lv:8 + lv, :]
    xbuf[5:8, :] = tail

    if lv == c:
        sm = smt_ref[0, 0]
    else:
        lane = lax.broadcasted_iota(jnp.int32, (AB_SMALL, c), 1)
        sm = jnp.where(lane < lv, jnp.broadcast_to(smt_ref[0, 0], (AB_SMALL, c)), 0.0)
    valid = lax.broadcasted_iota(jnp.int32, (AB_SMALL, c), 1) < lv
    bias_col = pcol_ref[:, 0:1]
    alog_col = pcol_ref[:, 1:2]
    beta_t = jnp.where(valid, jax.nn.sigmoid(sm), 0.0)
    sp_t = jnp.where(valid, jax.nn.softplus(sm + bias_col), 0.0)
    la_t = -jnp.exp(alog_col) * sp_t
    rr = lax.broadcasted_iota(jnp.int32, (c, c), 0)
    qq = lax.broadcasted_iota(jnp.int32, (c, c), 1)
    triu = (rr <= qq).astype(f32)
    eye = (rr == qq).astype(f32)
    cum_t = _mm(la_t, triu, precision=HI)
    last_t = cum_t[:, c - 1:c]
    ecum_t = jnp.exp(cum_t)
    elc_t = jnp.exp(last_t - cum_t)
    elast_t = jnp.exp(last_t)
    g0, g1, s0, s1 = 0, GDN_HEADS, 2 * GDN_HEADS, AB_SMALL
    rows = jnp.concatenate([
        beta_t[g0:g1],
        cum_t[g1:s0],
        ecum_t[g1:s0],
        beta_t[g0:g1] * ecum_t[g1:s0],
        elc_t[g1:s0],
        cum_t[s0:s1],
        ecum_t[s0:s1],
        sp_t[s0:s1] * elc_t[s0:s1],
        jnp.zeros((128 - 88, c), f32),
    ], axis=0)
    cols = _nt(eye, rows, precision=HI)
    incl = rr >= qq
    strict = rr > qq

    def colstack(base, n):
        return jnp.stack([cols[:, base + h:base + h + 1] for h in range(n)])

    def rowstack(x, base, n):
        return jnp.stack([x[base + h:base + h + 1, :] for h in range(n)])

    hs = range(GDN_HEADS)
    q = jnp.stack([act[:, h * GDN_DK:(h + 1) * GDN_DK] for h in hs])
    k = jnp.stack([act[:, GDN_QK + h * GDN_DK:GDN_QK + (h + 1) * GDN_DK] for h in hs])
    v = jnp.stack([act[:, 2 * GDN_QK + h * GDN_DV:2 * GDN_QK + (h + 1) * GDN_DV] for h in hs])
    q = q * lax.rsqrt(jnp.sum(q * q, axis=-1, keepdims=True) + EPS) * (GDN_DK ** -0.5)
    k = k * lax.rsqrt(jnp.sum(k * k, axis=-1, keepdims=True) + EPS)
    beta_c, cum_c, ecum_c, becum_c, elc_c = (colstack(b, GDN_HEADS) for b in (0, 8, 16, 24, 32))
    dec = jnp.exp(jnp.where(incl, cum_c - rowstack(cum_t, g1, GDN_HEADS), -jnp.inf))
    kb = k.astype(bf16)
    qb = q.astype(bf16)
    lm = jnp.where(strict, beta_c * _bnt(kb, kb) * dec, 0.0)
    pinv = _unit_lower_inverse(lm, c)
    rhs = jnp.concatenate([v * beta_c, k * becum_c], axis=2)
    sol = _bmm3(pinv, rhs)
    u0 = sol[:, :, :GDN_DV]
    w = sol[:, :, GDN_DV:]
    qk = _bnt(qb, kb) * dec
    s_old = sg[...]
    sb = s_old.astype(bf16)
    u = u0 - _bmm(w.astype(bf16), sb)
    ub = u.astype(bf16)
    o = _bmm((q * ecum_c).astype(bf16), sb) + _bmm(qk.astype(bf16), ub)
    kd = (k * elc_c).astype(bf16)
    sg[...] = s_old * rowstack(elast_t, g1, GDN_HEADS) + _btn(kd, ub)
    oa = _rms(o) * gng_ref[...]
    for h in hs:
        gate = proj_ref[0, :, CONV_CH + h * GDN_DV:CONV_CH + (h + 1) * GDN_DV].astype(f32)
        o_ref[0, :, h * GDN_DV:(h + 1) * GDN_DV] = (oa[h][0:lv] * _silu(gate)).astype(bf16)

    rep = SSM_HEADS // SSM_G
    xs0 = 2 * GDN_QK + GDN_V
    bm0 = xs0 + SSM_INNER
    cm0 = bm0 + SSM_BC
    hs = range(SSM_HEADS)
    bmb = [act[:, bm0 + g * SSM_N:bm0 + (g + 1) * SSM_N].astype(bf16) for g in range(SSM_G)]
    cmb = [act[:, cm0 + g * SSM_N:cm0 + (g + 1) * SSM_N].astype(bf16) for g in range(SSM_G)]
    cbg = [_nt(cmb[g], bmb[g]) for g in range(SSM_G)]
    cb = jnp.stack([cbg[h // rep] for h in hs])
    bm16 = jnp.stack([bmb[h // rep] for h in hs])
    cm16 = jnp.stack([cmb[h // rep] for h in hs])
    xs = jnp.stack([act[:, xs0 + h * SSM_P:xs0 + (h + 1) * SSM_P] for h in hs])
    dsk = jnp.stack([dskip_ref[:, h * SSM_P:(h + 1) * SSM_P] for h in hs])
    cum_c, ecum_c, dtelc_c = (colstack(b, SSM_HEADS) for b in (40, 56, 72))
    m = cb * jnp.exp(jnp.where(incl, cum_c - rowstack(cum_t, s0, SSM_HEADS), -jnp.inf)) * rowstack(sp_t, s0, SSM_HEADS)
    s_old = ss[...]
    o = _bmm(m.astype(bf16), xs.astype(bf16)) + _bmm(cm16, s_old.astype(bf16)) * ecum_c + dsk * xs
    ss[...] = s_old * rowstack(elast_t, s0, SSM_HEADS) + _btn(bm16, (xs * dtelc_c).astype(bf16))
    for h in hs:
        ob[:, h * SSM_P:(h + 1) * SSM_P] = o[h]
    z = proj_ref[0, :, CONV_CH + GDN_V:CONV_CH + GDN_V + SSM_INNER].astype(f32)
    obv = _rms(ob[0:lv, :] * _silu(z)) * sng_ref[...]
    o_ref[0, :, GDN_V:GDN_V + SSM_INNER] = obv.astype(bf16)

    @pl.when(l == nl - 1)
    def _():
        convn_ref[0] = xbuf[5:8, :]
        gdnn_ref[0] = sg[...]
        ssmn_ref[0] = ss[...]


def mixer_ab(proj, smt, conv0, gdn0, ssm0, conv_w, conv_b, pcol, gdn_norm_g, dskip, ssm_norm_g, c):
    b, l, npad = proj.shape
    tl = min(c, l)
    nl = l // tl
    kern = functools.partial(_ab_kernel, c=c, lv=tl)
    full = lambda shape: pl.BlockSpec(shape, lambda i, j: (0,) * len(shape))
    return pl.pallas_call(
        kern,
        grid=(b, nl),
        in_specs=[
            pl.BlockSpec((1, tl, npad), lambda i, j: (i, j, 0)),
            pl.BlockSpec((1, 1, AB_SMALL, tl), lambda i, j: (i, j, 0, 0)),
            pl.BlockSpec((1, CONV_W - 1, CONV_CH), lambda i, j: (i, 0, 0)),
            pl.BlockSpec((1, GDN_HEADS, GDN_DK, GDN_DV), lambda i, j: (i, 0, 0, 0)),
            pl.BlockSpec((1, SSM_HEADS, SSM_N, SSM_P), lambda i, j: (i, 0, 0, 0)),
            full((CONV_W, CONV_CH)),
            full((1, CONV_CH)),
            full((AB_SMALL, 2)),
            full((1, GDN_DV)),
            full((1, SSM_INNER)),
            full((1, SSM_INNER)),
        ],
        out_specs=[
            pl.BlockSpec((1, tl, GDN_V + SSM_INNER), lambda i, j: (i, j, 0)),
            pl.BlockSpec((1, CONV_W - 1, CONV_CH), lambda i, j: (i, 0, 0)),
            pl.BlockSpec((1, GDN_HEADS, GDN_DK, GDN_DV), lambda i, j: (i, 0, 0, 0)),
            pl.BlockSpec((1, SSM_HEADS, SSM_N, SSM_P), lambda i, j: (i, 0, 0, 0)),
        ],
        out_shape=[
            jax.ShapeDtypeStruct((b, l, GDN_V + SSM_INNER), bf16),
            jax.ShapeDtypeStruct((b, CONV_W - 1, CONV_CH), f32),
            jax.ShapeDtypeStruct((b, GDN_HEADS, GDN_DK, GDN_DV), f32),
            jax.ShapeDtypeStruct((b, SSM_HEADS, SSM_N, SSM_P), f32),
        ],
        scratch_shapes=[
            pltpu.VMEM((8 + c, CONV_CH), f32),
            pltpu.VMEM((c, CONV_CH), f32),
            pltpu.VMEM((GDN_HEADS, GDN_DK, GDN_DV), f32),
            pltpu.VMEM((SSM_HEADS, SSM_N, SSM_P), f32),
            pltpu.VMEM((c, SSM_INNER), f32),
        ],
        compiler_params=_params(("arbitrary", "arbitrary")),
        name="mixer_ab",
    )(proj, smt, conv0, gdn0, ssm0, conv_w, conv_b, pcol, gdn_norm_g, dskip, ssm_norm_g)


def _ret_kernel(proj_ref, cos_ref, sin_ref, ret0_ref, ng_ref, o_ref, retn_ref, st, buf, *, c, lv):
    l = pl.program_id(1)
    nl = pl.num_programs(1)

    @pl.when(l == 0)
    def _():
        st[...] = ret0_ref[0]
        if lv < c:
            buf[...] = jnp.zeros(buf.shape, f32)

    if lv == c:
        src = proj_ref.at[0]
        cos = cos_ref[...]
        sin = sin_ref[...]
    else:
        buf[0:lv, :] = proj_ref[0].astype(f32)
        src = buf
        cos = jnp.broadcast_to(cos_ref[...], (c, RET_DK // 2))
        sin = jnp.broadcast_to(sin_ref[...], (c, RET_DK // 2))

    ri = lax.broadcasted_iota(jnp.int32, (c, c), 0)
    ci = lax.broadcasted_iota(jnp.int32, (c, c), 1)
    incl = ri >= ci
    cnt_r = jnp.minimum(ri + 1, lv).astype(f32)
    cnt_c = jnp.minimum(ci + 1, lv).astype(f32)
    cnt_col = jnp.minimum(lax.broadcasted_iota(jnp.int32, (c, 1), 0) + 1, lv).astype(f32)
    half = RET_DK // 2

    def rope(x):
        x1, x2 = x[:, :half], x[:, half:]
        return jnp.concatenate([x1 * cos - x2 * sin, x1 * sin + x2 * cos], axis=1)

    for h in range(RET_HEADS):
        lg = math.log(1.0 - 2.0 ** (-5.0 - h))
        q = rope(src[:, h * RET_DK:(h + 1) * RET_DK].astype(f32))
        k = rope(src[:, RET_QK + h * RET_DK:RET_QK + (h + 1) * RET_DK].astype(f32)) * (RET_DK ** -0.5)
        v = src[:, 2 * RET_QK + h * RET_DV:2 * RET_QK + (h + 1) * RET_DV].astype(bf16)
        gate = src[:, 2 * RET_QK + RET_V + h * RET_DV:2 * RET_QK + RET_V + (h + 1) * RET_DV].astype(f32)
        dec = jnp.exp(jnp.where(incl, (cnt_r - cnt_c) * lg, -jnp.inf))
        scores = _nt(q.astype(bf16), k.astype(bf16)) * dec
        s_old = st[h]
        o = _mm(scores.astype(bf16), v) + _mm((q * jnp.exp(cnt_col * lg)).astype(bf16), s_old.astype(bf16))
        kd = (k * jnp.exp((lv - cnt_col) * lg)).astype(bf16)
        st[h] = s_old * math.exp(lv * lg) + _tn(kd, v)
        mu = jnp.mean(o, axis=-1, keepdims=True)
        var = jnp.mean(jnp.square(o - mu), axis=-1, keepdims=True)
        o = (o - mu) * lax.rsqrt(var + EPS) * ng_ref[h:h + 1, :]
        o_ref[0, :, h * RET_DV:(h + 1) * RET_DV] = (_silu(gate) * o)[0:lv].astype(bf16)

    @pl.when(l == nl - 1)
    def _():
        retn_ref[0] = st[...]


def mixer_ret(proj, cos, sin, ret0, norm_g, c):
    b, l, n = proj.shape
    tl = min(c, l)
    nl = l // tl
    kern = functools.partial(_ret_kernel, c=c, lv=tl)
    return pl.pallas_call(
        kern,
        grid=(b, nl),
        in_specs=[
            pl.BlockSpec((1, tl, n), lambda i, j: (i, j, 0)),
            pl.BlockSpec((tl, RET_DK // 2), lambda i, j: (j, 0)),
            pl.BlockSpec((tl, RET_DK // 2), lambda i, j: (j, 0)),
            pl.BlockSpec((1, RET_HEADS, RET_DK, RET_DV), lambda i, j: (i, 0, 0, 0)),
            pl.BlockSpec((RET_HEADS, RET_DV), lambda i, j: (0, 0)),
        ],
        out_specs=[
            pl.BlockSpec((1, tl, RET_V), lambda i, j: (i, j, 0)),
            pl.BlockSpec((1, RET_HEADS, RET_DK, RET_DV), lambda i, j: (i, 0, 0, 0)),
        ],
        out_shape=[
            jax.ShapeDtypeStruct((b, l, RET_V), bf16),
            jax.ShapeDtypeStruct((b, RET_HEADS, RET_DK, RET_DV), f32),
        ],
        scratch_shapes=[
            pltpu.VMEM((RET_HEADS, RET_DK, RET_DV), f32),
            pltpu.VMEM((c, n), f32),
        ],
        compiler_params=_params(("arbitrary", "arbitrary")),
        name="mixer_ret",
    )(proj, cos, sin, ret0, norm_g)


def _top16(s, exact):
    g, n, t = s.shape
    rows = lax.broadcasted_iota(jnp.int32, (g, n, t), 1)
    krow = lax.broadcasted_iota(jnp.int32, (g, PEER_TOPK, t), 1)
    rank = jnp.full((g, n, t), float(PEER_TOPK), f32)
    sv = jnp.zeros((g, PEER_TOPK, t), f32)
    work = s
    for k in range(PEER_TOPK):
        m = jnp.max(work, axis=1, keepdims=True)
        if exact:
            idx = jnp.min(jnp.where(work == m, rows, n), axis=1, keepdims=True)
            sel = rows == idx
        else:
            sel = work == m
        rank = jnp.where(sel, float(k), rank)
        work = jnp.where(sel, -jnp.inf, work)
        sv = jnp.where(krow == k, m, sv)
    return sv, rank


def _miscount(rank):
    cnt = jnp.sum(jnp.where(rank < float(PEER_TOPK), 1.0, 0.0), axis=1, keepdims=True)
    return jnp.max(jnp.abs(cnt - float(PEER_TOPK)))


def _topk_kernel(x_ref, g_ref, sc_ref, sh_ref, wqt_ref, keys_ref, a1_ref, a2_ref, a1t_ref,
                 hmt_ref, r2_ref, e2_ref, lc_ref, g1_ref, sv_scr, rk_scr, rc_scr):
    y = _rms(x_ref[...]) * g_ref[...]
    hm = y * (1.0 + sc_ref[0]) + sh_ref[0]
    hmt = hm.T.astype(bf16)
    hmt_ref[...] = hmt
    qt = _mm(wqt_ref[...], hmt)
    tt = qt.shape[1]
    half = PEER_DQ // 2
    hs = range(PEER_HEADS)
    s_all = jnp.stack(
        [_mm(keys_ref[0], qt[h * PEER_DQ:h * PEER_DQ + half, :].astype(bf16)) for h in hs]
        + [_mm(keys_ref[1], qt[h * PEER_DQ + half:(h + 1) * PEER_DQ, :].astype(bf16)) for h in hs])
    keys_rows = slice(0, PEER_NKEYS)
    pad = jnp.zeros((PEER_HEADS, PEER_NKP - PEER_NKEYS, 128), f32)
    crow = lax.broadcasted_iota(jnp.int32, (PEER_HEADS, _NCAND_PAD, 128), 1)

    def candidates(sv):
        cand = jnp.stack([_mm(a1_ref[...], sv[h], precision=HI) + _mm(a2_ref[...], sv[PEER_HEADS + h], precision=HI)
                          for h in hs])
        return jnp.where(crow < _NCAND, cand, -jnp.inf)

    for lg in range(tt // 128):
        ls = slice(lg * 128, (lg + 1) * 128)
        s_blk = s_all[:, :, ls]

        def stage1(rows, exact):
            sv, rank = _top16(s_blk[rows], exact)
            sv_scr[rows, :, ls] = sv
            rk_scr[rows, :, ls] = rank
            return _miscount(rank)

        def stage2(exact):
            _, rankc = _top16(candidates(sv_scr[:, :, ls]), exact)
            rc_scr[:, :, ls] = rankc
            return _miscount(rankc)

        for rows in (slice(0, PEER_HEADS), slice(PEER_HEADS, 2 * PEER_HEADS)):
            @pl.when(stage1(rows, False) > 0.0)
            def _():
                stage1(rows, True)

        @pl.when(stage2(False) > 0.0)
        def _():
            stage2(True)

        sv = sv_scr[:, :, ls]
        sv1, sv2 = sv[:PEER_HEADS], sv[PEER_HEADS:]
        rank1 = rk_scr[0:PEER_HEADS, :, ls]
        sel_all = jnp.where(rc_scr[:, :, ls] < float(PEER_TOPK), 1.0, 0.0)
        cand = candidates(sv)
        z = jnp.sum(sel_all * jnp.exp(jnp.where(crow < _NCAND, cand - cand[:, 0:1, :], 0.0)), axis=1, keepdims=True)
        selb = sel_all.astype(bf16)
        cnt = jnp.stack([_mm(a1t_ref[...], selb[h]) for h in hs])
        lc = jnp.zeros((PEER_HEADS, PEER_NKEYS, 128), f32)
        for k1 in range(PEER_TOPK):
            lc = jnp.where(rank1 == float(k1), cnt[:, k1:k1 + 1, :], lc)
        r2_ref[:, keys_rows, ls] = rk_scr[PEER_HEADS:2 * PEER_HEADS, :, ls]
        e2_ref[:, keys_rows, ls] = jnp.exp(s_blk[PEER_HEADS:] - sv2[:, 0:1, :])
        lc_ref[:, keys_rows, ls] = lc
        g1_ref[:, keys_rows, ls] = jnp.exp(s_blk[:PEER_HEADS] - sv1[:, 0:1, :]) / z
        for ref in (r2_ref, e2_ref, lc_ref, g1_ref):
            ref[:, PEER_NKEYS:, ls] = pad


def peer_topk(x, gamma, sc, sh, mod_map, wq_t, keys, tt):
    t = x.shape[0]
    r = sc.shape[1]
    a1 = jnp.zeros((_NCAND_PAD, PEER_TOPK), f32).at[jnp.arange(_NCAND), jnp.array([a for a, _ in _CAND])].set(1.0)
    a2 = jnp.zeros((_NCAND_PAD, PEER_TOPK), f32).at[jnp.arange(_NCAND), jnp.array([b for _, b in _CAND])].set(1.0)
    full = lambda shape: pl.BlockSpec(shape, lambda i: (0,) * len(shape))
    tok = pl.BlockSpec((PEER_HEADS, PEER_NKP, tt), lambda i: (0, 0, i))
    tok_shape = jax.ShapeDtypeStruct((PEER_HEADS, PEER_NKP, t), f32)
    return pl.pallas_call(
        _topk_kernel,
        grid=(t // tt,),
        in_specs=[
            pl.BlockSpec((tt, D_MODEL), lambda i: (i, 0)),
            full((1, D_MODEL)),
            pl.BlockSpec((1, r, D_MODEL), lambda i: mod_map(i)),
            pl.BlockSpec((1, r, D_MODEL), lambda i: mod_map(i)),
            full((PEER_HEADS * PEER_DQ, D_MODEL)),
            full((2, PEER_NKEYS, PEER_DQ // 2)),
            full((_NCAND_PAD, PEER_TOPK)),
            full((_NCAND_PAD, PEER_TOPK)),
            full((PEER_TOPK, _NCAND_PAD)),
        ],
        out_specs=[pl.BlockSpec((D_MODEL, tt), lambda i: (0, i)), tok, tok, tok, tok],
        out_shape=[jax.ShapeDtypeStruct((D_MODEL, t), bf16), tok_shape, tok_shape, tok_shape, tok_shape],
        scratch_shapes=[
            pltpu.VMEM((2 * PEER_HEADS, PEER_TOPK, tt), f32),
            pltpu.VMEM((2 * PEER_HEADS, PEER_NKEYS, tt), f32),
            pltpu.VMEM((PEER_HEADS, _NCAND_PAD, tt), f32),
        ],
        compiler_params=_params(("arbitrary",)),
        name="peer_topk",
    )(x, gamma.reshape(1, D_MODEL), sc, sh, wq_t, keys, a1, a2, a1.T.astype(bf16))


def _peer_kernel(hmt_ref, u_ref, vt_ref, r2_ref, e2_ref, lc_ref, g1_ref, x_ref, g_ref, fg_ref, o_ref,
                 yt, act_a, act_b, ht_a, ht_b, *, nsub, final):
    j = pl.program_id(1)
    nj = pl.num_programs(1)
    na = PEER_SB // PEER_NKEYS
    tt = hmt_ref.shape[1]

    @pl.when(j == 0)
    def _():
        yt[...] = jnp.zeros(yt.shape, f32)

    def mm1(s, ht):
        ht[...] = _mm(u_ref[s], hmt_ref[...])

    def gate_act(s, ht, act):
        for a in range(na):
            n1 = (j * nsub + s) * na + a
            lrows = [lc_ref[h, pl.ds(n1, 1), :] for h in range(PEER_HEADS)]
            grows = [g1_ref[h, pl.ds(n1, 1), :] for h in range(PEER_HEADS)]
            for lg in range(tt // 128):
                ls = slice(lg * 128, (lg + 1) * 128)
                w = None
                for h in range(PEER_HEADS):
                    wh = jnp.where(r2_ref[h, 0:PEER_NKEYS, ls] < lrows[h][:, ls], e2_ref[h, 0:PEER_NKEYS, ls] * grows[h][:, ls], 0.0)
                    w = wh if w is None else w + wh
                hb = ht[a * PEER_NKEYS:(a + 1) * PEER_NKEYS, ls]
                act[a * PEER_NKEYS:(a + 1) * PEER_NKEYS, ls] = (_gelu(hb) * w).astype(bf16)

    def mm2(s, act):
        yt[...] += _mm(vt_ref[s], act[...])

    act_b[...] = jnp.zeros(act_b.shape, bf16)
    mm1(0, ht_a)

    def body(i, carry):
        s = 2 * i
        mm1(s + 1, ht_b)
        gate_act(s, ht_a, act_a)
        mm2(jnp.maximum(s - 1, 0), act_b)
        mm1(jnp.minimum(s + 2, nsub - 1), ht_a)
        gate_act(s + 1, ht_b, act_b)
        mm2(s, act_a)
        return carry

    lax.fori_loop(0, nsub // 2, body, 0)
    mm2(nsub - 1, act_b)

    @pl.when(j == nj - 1)
    def _():
        xn = x_ref[...] + g_ref[0] * yt[...].T
        if final:
            xn = _rms(xn) * fg_ref[...]
        o_ref[...] = xn


def peer_dense(hmt, u3, vt3, r2, e2, lc, g1, x, gate, mod_map, final_g, tt, eb):
    t = x.shape[0]
    r = gate.shape[1]
    nsub = eb // PEER_SB
    final = final_g is not None
    fg = (final_g if final else jnp.ones((D_MODEL,), f32)).reshape(1, D_MODEL)
    tok = pl.BlockSpec((PEER_HEADS, PEER_NKP, tt), lambda i, j: (0, 0, i))
    return pl.pallas_call(
        functools.partial(_peer_kernel, nsub=nsub, final=final),
        grid=(t // tt, N_EXPERTS // eb),
        in_specs=[
            pl.BlockSpec((D_MODEL, tt), lambda i, j: (0, i)),
            pl.BlockSpec((nsub, PEER_SB, D_MODEL), lambda i, j: (j, 0, 0)),
            pl.BlockSpec((nsub, D_MODEL, PEER_SB), lambda i, j: (j, 0, 0)),
            tok, tok, tok, tok,
            pl.BlockSpec((tt, D_MODEL), lambda i, j: (i, 0)),
            pl.BlockSpec((1, r, D_MODEL), lambda i, j: mod_map(i)),
            pl.BlockSpec((1, D_MODEL), lambda i, j: (0, 0)),
        ],
        out_specs=pl.BlockSpec((tt, D_MODEL), lambda i, j: (i, 0)),
        out_shape=jax.ShapeDtypeStruct((t, D_MODEL), f32),
        scratch_shapes=[
            pltpu.VMEM((D_MODEL, tt), f32),
            pltpu.VMEM((PEER_SB, tt), bf16), pltpu.VMEM((PEER_SB, tt), bf16),
            pltpu.VMEM((PEER_SB, tt), f32), pltpu.VMEM((PEER_SB, tt), f32),
        ],
        compiler_params=_params(("arbitrary", "arbitrary")),
        name="peer_dense",
    )(hmt, u3, vt3, r2, e2, lc, g1, x, gate, fg)


def _prepare(ab_w_in, ab_conv_w, ab_conv_b, gdn_a_log, gdn_dt_bias, gdn_norm_g, ssm_a_log, ssm_dt_bias, ssm_d,
             ssm_norm_g, ab_w_out, ret_w_in, ret_norm_g, ret_w_out, peer_w_q, peer_keys, peer_u, peer_v):
    zeros8 = jnp.zeros((GDN_HEADS,), f32)
    nsb = N_EXPERTS // PEER_SB
    return dict(
        ab_w_main=ab_w_in[0][:, :AB_MAIN].astype(bf16),
        ab_w_small_t=ab_w_in[0][:, AB_MAIN:].T.astype(bf16),
        conv_w=ab_conv_w[0],
        conv_b=ab_conv_b[0].reshape(1, CONV_CH),
        pcol=jnp.stack([jnp.concatenate([zeros8, gdn_dt_bias[0], ssm_dt_bias[0]]),
                        jnp.concatenate([zeros8, gdn_a_log[0], ssm_a_log[0]])], axis=1),
        gdn_norm_g=gdn_norm_g[0].reshape(1, GDN_DV),
        dskip=jnp.repeat(ssm_d[0], SSM_P).reshape(1, SSM_INNER),
        ssm_norm_g=ssm_norm_g[0].reshape(1, SSM_INNER),
        ab_w_out=ab_w_out[0].astype(bf16),
        ret_w_in=ret_w_in[0].astype(bf16),
        ret_norm_g=ret_norm_g[0],
        ret_w_out=ret_w_out[0].astype(bf16),
        wq_t=[peer_w_q[i].T.astype(bf16) for i in range(DEPTH)],
        keys=[peer_keys[i].astype(bf16) for i in range(DEPTH)],
        u=[peer_u[i].astype(bf16).reshape(nsb, PEER_SB, D_MODEL) for i in range(DEPTH)],
        vt=[peer_v[i].astype(bf16).reshape(nsb, PEER_SB, D_MODEL).transpose(0, 2, 1) for i in range(DEPTH)],
    )


def _rope_tables(pos0, length):
    inv = ROPE_BASE ** (-jnp.arange(0, RET_DK, 2, dtype=f32) / RET_DK)
    ang = (pos0 + jnp.arange(length, dtype=f32))[:, None] * inv[None, :]
    return jnp.cos(ang), jnp.sin(ang)


def _trunk(x, mods, pos0, conv0, gdn0, ssm0, ret0, p, norm1_g, norm2_g, final_g):
    b, l, _ = x.shape
    t = b * l
    c = CHUNK if l >= CHUNK else 8
    xt = x.reshape(t, D_MODEL)
    if l > 1:
        tm = min(512, l)
        tt = min(256, l)
        mod_arr = lambda m: m.reshape(b, 1, D_MODEL)
        mod_map = lambda rows: (lambda i: (i * rows // l, 0, 0))
    else:
        tm = t
        tt = t
        mod_arr = lambda m: m.reshape(1, t, D_MODEL)
        mod_map = lambda rows: (lambda i: (0, i, 0))
    convs = gdns = ssms = rets = None
    for layer in range(DEPTH):
        sh1, sc1, g1, sh2, sc2, g2 = (mod_arr(m) for m in jnp.split(mods[layer], 6, axis=-1))
        if layer == 0:
            proj, smt = norm_proj(xt, norm1_g[layer], sc1, sh1, mod_map(tm), p["ab_w_main"], p["ab_w_small_t"],
                                  tm, AB_MAIN // 4)
            tl = min(c, l)
            smt = smt.reshape(AB_SMALL, b, l // tl, tl).transpose(1, 2, 0, 3)
            o, convs, gdns, ssms = mixer_ab(proj.reshape(b, l, -1), smt, conv0, gdn0, ssm0, p["conv_w"], p["conv_b"],
                                            p["pcol"], p["gdn_norm_g"], p["dskip"], p["ssm_norm_g"], c)
            xt = out_proj(o.reshape(t, -1), p["ab_w_out"], xt, g1, mod_map(tm), tm)
        else:
            proj = norm_proj(xt, norm1_g[layer], sc1, sh1, mod_map(tm), p["ret_w_in"], None, tm, RET_IN // 4)
            cos, sin = _rope_tables(pos0, l)
            o, rets = mixer_ret(proj.reshape(b, l, -1), cos, sin, ret0, p["ret_norm_g"], c)
            xt = out_proj(o.reshape(t, -1), p["ret_w_out"], xt, g1, mod_map(tm), tm)
        hmt, r2, e2, lc, gg = peer_topk(xt, norm2_g[layer], sc2, sh2, mod_map(tt), p["wq_t"][layer], p["keys"][layer], tt)
        xt = peer_dense(hmt, p["u"][layer], p["vt"][layer], r2, e2, lc, gg, xt, g2, mod_map(tt),
                        final_g if layer == DEPTH - 1 else None, tt, PEER_EB)
    return xt.reshape(b, l, D_MODEL), convs[None], gdns[None], ssms[None], rets[None]


def kernel(x_prompt, x_sample, c_prompt, c_sample, state_conv, state_gdn, state_ssm, state_ret, ada_w, ada_b,
           norm1_g, norm2_g, ab_w_in, ab_conv_w, ab_conv_b, gdn_a_log, gdn_dt_bias, gdn_norm_g, ssm_a_log,
           ssm_dt_bias, ssm_d, ssm_norm_g, ab_w_out, ret_w_in, ret_norm_g, ret_w_out, peer_w_q, peer_keys,
           peer_u, peer_v, final_g):
    p = _prepare(ab_w_in, ab_conv_w, ab_conv_b, gdn_a_log, gdn_dt_bias, gdn_norm_g, ssm_a_log, ssm_dt_bias, ssm_d,
                 ssm_norm_g, ab_w_out, ret_w_in, ret_norm_g, ret_w_out, peer_w_q, peer_keys, peer_u, peer_v)
    nb = x_prompt.shape[0]
    mods = ada_mod(jnp.concatenate([c_prompt, c_sample], axis=0), ada_w, ada_b)
    zeros = lambda s: jnp.zeros((nb,) + s.shape[2:], s.dtype)
    y_p, p_conv, p_gdn, p_ssm, p_ret = _trunk(
        x_prompt, mods[:, :nb], 0, zeros(state_conv), zeros(state_gdn), zeros(state_ssm), zeros(state_ret),
        p, norm1_g, norm2_g, final_g)
    y_s, s_conv, s_gdn, s_ssm, s_ret = _trunk(
        x_sample, mods[:, nb:], PAST_LEN, state_conv[0], state_gdn[0], state_ssm[0], state_ret[0],
        p, norm1_g, norm2_g, final_g)
    return (y_p, y_s, p_conv, p_gdn, p_ssm, p_ret, s_conv, s_gdn, s_ssm, s_ret)
```

```python
import functools
import math

import jax
import jax.numpy as jnp
from jax import lax
from jax.experimental import pallas as pl
from jax.experimental.pallas import tpu as pltpu

f32 = jnp.float32
bf16 = jnp.bfloat16
HI = lax.Precision.HIGHEST

D_MODEL = 1024
DEPTH = 2
PAST_LEN = 16384
GDN_HEADS = 8
GDN_DK = 128
GDN_DV = 128
SSM_HEADS = 16
SSM_P = 64
SSM_N = 128
SSM_G = 2
CONV_W = 4
RET_HEADS = 4
RET_DK = 256
RET_DV = 512
ROPE_BASE = 10000.0
PEER_HEADS = 8
PEER_NKEYS = 128
PEER_TOPK = 16
PEER_DQ = 256
CHUNK = 64
EPS = 1e-6

GDN_QK = GDN_HEADS * GDN_DK
GDN_V = GDN_HEADS * GDN_DV
SSM_INNER = SSM_HEADS * SSM_P
SSM_BC = SSM_G * SSM_N
CONV_CH = 2 * GDN_QK + GDN_V + SSM_INNER + 2 * SSM_BC
AB_MAIN = CONV_CH + GDN_V + SSM_INNER
AB_SMALL = 2 * GDN_HEADS + SSM_HEADS
RET_QK = RET_HEADS * RET_DK
RET_V = RET_HEADS * RET_DV
RET_IN = 2 * RET_QK + 2 * RET_V
N_EXPERTS = PEER_NKEYS * PEER_NKEYS

VMEM_LIMIT = 56 * 1024 * 1024
PEER_SB = 256
PEER_EB = 4096
PEER_NKP = PEER_NKEYS + 8

_CAND = [(a, b) for a in range(PEER_TOPK) for b in range(PEER_TOPK) if (a + 1) * (b + 1) <= PEER_TOPK]
_NCAND = len(_CAND)
_NCAND_PAD = -(-_NCAND // 16) * 16


def _params(sem):
    return pltpu.CompilerParams(dimension_semantics=sem, vmem_limit_bytes=VMEM_LIMIT)


def _nt(a, b, **kw):
    return lax.dot_general(a, b, (((1,), (1,)), ((), ())), preferred_element_type=f32, **kw)


def _tn(a, b, **kw):
    return lax.dot_general(a, b, (((0,), (0,)), ((), ())), preferred_element_type=f32, **kw)


def _mm(a, b, **kw):
    return jnp.dot(a, b, preferred_element_type=f32, **kw)


def _bmm(a, b, **kw):
    return jnp.einsum('hik,hkj->hij', a, b, preferred_element_type=f32, **kw)


def _bnt(a, b, **kw):
    return jnp.einsum('hik,hjk->hij', a, b, preferred_element_type=f32, **kw)


def _btn(a, b, **kw):
    return jnp.einsum('hki,hkj->hij', a, b, preferred_element_type=f32, **kw)


def _split(x):
    hi = x.astype(bf16)
    return hi, (x - hi.astype(f32)).astype(bf16)


def _bmm3(a, b):
    ah, al = _split(a)
    bh, bl = _split(b)
    return _bmm(ah, bh) + (_bmm(ah, bl) + _bmm(al, bh))


def _silu(x):
    return x * jax.nn.sigmoid(x)


def _gelu(x):
    return 0.5 * x * (1.0 + lax.erf(x * (2.0 ** -0.5)))


def _rms(x):
    return x * lax.rsqrt(jnp.mean(x * x, axis=-1, keepdims=True) + EPS)


def _mod_kernel(c_ref, w_ref, b_ref, o_ref):
    a = _silu(c_ref[...]).astype(bf16)
    o_ref[0] = _mm(a, w_ref[0].astype(bf16)) + b_ref[0]


def ada_mod(c_all, ada_w, ada_b):
    m = c_all.shape[0]
    tn = 768
    n = ada_w.shape[-1]
    return pl.pallas_call(
        _mod_kernel,
        grid=(DEPTH, n // tn),
        in_specs=[
            pl.BlockSpec((m, D_MODEL), lambda l, j: (0, 0)),
            pl.BlockSpec((1, D_MODEL, tn), lambda l, j: (l, 0, j)),
            pl.BlockSpec((1, 1, tn), lambda l, j: (l, 0, j)),
        ],
        out_specs=pl.BlockSpec((1, m, tn), lambda l, j: (l, 0, j)),
        out_shape=jax.ShapeDtypeStruct((DEPTH, m, n), f32),
        compiler_params=_params(("arbitrary", "arbitrary")),
        name="ada_mod",
    )(c_all, ada_w, ada_b.reshape(DEPTH, 1, n))


def _proj_kernel(x_ref, g_ref, sc_ref, sh_ref, w_ref, *rest, has_small):
    if has_small:
        ws_ref, o_ref, os_ref, hm_ref = rest
    else:
        o_ref, hm_ref = rest
    j = pl.program_id(1)

    @pl.when(j == 0)
    def _():
        y = _rms(x_ref[...]) * g_ref[...]
        hm = (y * (1.0 + sc_ref[0]) + sh_ref[0]).astype(bf16)
        hm_ref[...] = hm
        if has_small:
            os_ref[...] = _nt(ws_ref[...], hm)

    o_ref[...] = _mm(hm_ref[...], w_ref[...]).astype(o_ref.dtype)


def norm_proj(x, gamma, sc, sh, mod_map, w, ws_t, tm, tn):
    t = x.shape[0]
    n = w.shape[1]
    r = sc.shape[1]
    has_small = ws_t is not None
    in_specs = [
        pl.BlockSpec((tm, D_MODEL), lambda i, j: (i, 0)),
        pl.BlockSpec((1, D_MODEL), lambda i, j: (0, 0)),
        pl.BlockSpec((1, r, D_MODEL), lambda i, j: mod_map(i)),
        pl.BlockSpec((1, r, D_MODEL), lambda i, j: mod_map(i)),
        pl.BlockSpec((D_MODEL, tn), lambda i, j: (0, j)),
    ]
    out_specs = [pl.BlockSpec((tm, tn), lambda i, j: (i, j))]
    out_shape = [jax.ShapeDtypeStruct((t, n), bf16)]
    args = [x, gamma.reshape(1, D_MODEL), sc, sh, w]
    if has_small:
        s = ws_t.shape[0]
        in_specs.append(pl.BlockSpec((s, D_MODEL), lambda i, j: (0, 0)))
        out_specs.append(pl.BlockSpec((s, tm), lambda i, j: (0, i)))
        out_shape.append(jax.ShapeDtypeStruct((s, t), f32))
        args.append(ws_t)
    res = pl.pallas_call(
        functools.partial(_proj_kernel, has_small=has_small),
        grid=(t // tm, n // tn),
        in_specs=in_specs,
        out_specs=out_specs,
        out_shape=out_shape,
        scratch_shapes=[pltpu.VMEM((tm, D_MODEL), bf16)],
        compiler_params=_params(("arbitrary", "arbitrary")),
        name="norm_proj",
    )(*args)
    return res if has_small else res[0]


def _out_kernel(o_ref, w_ref, x_ref, g_ref, y_ref):
    y_ref[...] = x_ref[...] + g_ref[0] * _mm(o_ref[...], w_ref[...])


def out_proj(o, w, x, gate, mod_map, tm):
    t, k = o.shape
    r = gate.shape[1]
    return pl.pallas_call(
        _out_kernel,
        grid=(t // tm,),
        in_specs=[
            pl.BlockSpec((tm, k), lambda i: (i, 0)),
            pl.BlockSpec((k, D_MODEL), lambda i: (0, 0)),
            pl.BlockSpec((tm, D_MODEL), lambda i: (i, 0)),
            pl.BlockSpec((1, r, D_MODEL), lambda i: mod_map(i)),
        ],
        out_specs=pl.BlockSpec((tm, D_MODEL), lambda i: (i, 0)),
        out_shape=jax.ShapeDtypeStruct((t, D_MODEL), f32),
        compiler_params=_params(("arbitrary",)),
        name="out_proj",
    )(o, w, x, gate)


def _unit_lower_inverse(lm, c):
    eye = (lax.broadcasted_iota(jnp.int32, (c, c), 0) == lax.broadcasted_iota(jnp.int32, (c, c), 1)).astype(f32)
    p = eye - lm
    lp = lm
    k = 2
    while k < c:
        lp = _bmm3(lp, lp)
        p = p + _bmm3(p, lp)
        k *= 2
    return p


def _ab_kernel(proj_ref, smt_ref, conv0_ref, gdn0_ref, ssm0_ref, convw_ref, convb_ref, pcol_ref, gng_ref,
               dskip_ref, sng_ref, o_ref, convn_ref, gdnn_ref, ssmn_ref, xbuf, act, sg, ss, ob, *, c, lv):
    l = pl.program_id(1)
    nl = pl.num_programs(1)

    @pl.when(l == 0)
    def _():
        xbuf[...] = jnp.zeros(xbuf.shape, f32)
        xbuf[5:8, :] = conv0_ref[0]
        sg[...] = gdn0_ref[0]
        for h in range(SSM_HEADS):
            ss[h] = ssm0_ref[0, h].T

    xbuf[8:8 + lv, :] = proj_ref[0, :, 0:CONV_CH].astype(f32)
    y = convb_ref[...]
    for i in range(CONV_W):
        y = y + convw_ref[i:i + 1, :] * xbuf[5 + i:5 + i + c, :]
    act[...] = _silu(y)
    tail = xbuf[5 + lv:8 + lv, :]
    xbuf[5:8, :] = tail

    if lv == c:
        sm = smt_ref[0, 0]
    else:
        lane = lax.broadcasted_iota(jnp.int32, (AB_SMALL, c), 1)
        sm = jnp.where(lane < lv, jnp.broadcast_to(smt_ref[0, 0], (AB_SMALL, c)), 0.0)
    valid = lax.broadcasted_iota(jnp.int32, (AB_SMALL, c), 1) < lv
    bias_col = pcol_ref[:, 0:1]
    alog_col = pcol_ref[:, 1:2]
    beta_t = jnp.where(valid, jax.nn.sigmoid(sm), 0.0)
    sp_t = jnp.where(valid, jax.nn.softplus(sm + bias_col), 0.0)
    la_t = -jnp.exp(alog_col) * sp_t
    rr = lax.broadcasted_iota(jnp.int32, (c, c), 0)
    qq = lax.broadcasted_iota(jnp.int32, (c, c), 1)
    triu = (rr <= qq).astype(f32)
    eye = (rr == qq).astype(f32)
    cum_t = _mm(la_t, triu, precision=HI)
    last_t = cum_t[:, c - 1:c]
    ecum_t = jnp.exp(cum_t)
    elc_t = jnp.exp(last_t - cum_t)
    elast_t = jnp.exp(last_t)
    g0, g1, s0, s1 = 0, GDN_HEADS, 2 * GDN_HEADS, AB_SMALL
    rows = jnp.concatenate([
        beta_t[g0:g1],
        cum_t[g1:s0],
        ecum_t[g1:s0],
        beta_t[g0:g1] * ecum_t[g1:s0],
        elc_t[g1:s0],
        cum_t[s0:s1],
        ecum_t[s0:s1],
        sp_t[s0:s1] * elc_t[s0:s1],
        jnp.zeros((128 - 88, c), f32),
    ], axis=0)
    cols = _nt(eye, rows, precision=HI)
    incl = rr >= qq
    strict = rr > qq

    def colstack(base, n):
        return jnp.stack([cols[:, base + h:base + h + 1] for h in range(n)])

    def rowstack(x, base, n):
        return jnp.stack([x[base + h:base + h + 1, :] for h in range(n)])

    hs = range(GDN_HEADS)
    q = jnp.stack([act[:, h * GDN_DK:(h + 1) * GDN_DK] for h in hs])
    k = jnp.stack([act[:, GDN_QK + h * GDN_DK:GDN_QK + (h + 1) * GDN_DK] for h in hs])
    v = jnp.stack([act[:, 2 * GDN_QK + h * GDN_DV:2 * GDN_QK + (h + 1) * GDN_DV] for h in hs])
    q = q * lax.rsqrt(jnp.sum(q * q, axis=-1, keepdims=True) + EPS) * (GDN_DK ** -0.5)
    k = k * lax.rsqrt(jnp.sum(k * k, axis=-1, keepdims=True) + EPS)
    beta_c, cum_c, ecum_c, becum_c, elc_c = (colstack(b, GDN_HEADS) for b in (0, 8, 16, 24, 32))
    dec = jnp.exp(jnp.where(incl, cum_c - rowstack(cum_t, g1, GDN_HEADS), -jnp.inf))
    kb = k.astype(bf16)
    qb = q.astype(bf16)
    lm = jnp.where(strict, beta_c * _bnt(kb, kb) * dec, 0.0)
    pinv = _unit_lower_inverse(lm, c)
    rhs = jnp.concatenate([v * beta_c, k * becum_c], axis=2)
    sol = _bmm3(pinv, rhs)
    u0 = sol[:, :, :GDN_DV]
    w = sol[:, :, GDN_DV:]
    qk = _bnt(qb, kb) * dec
    s_old = sg[...]
    sb = s_old.astype(bf16)
    u = u0 - _bmm(w.astype(bf16), sb)
    ub = u.astype(bf16)
    o = _bmm((q * ecum_c).astype(bf16), sb) + _bmm(qk.astype(bf16), ub)
    kd = (k * elc_c).astype(bf16)
    sg[...] = s_old * rowstack(elast_t, g1, GDN_HEADS) + _btn(kd, ub)
    oa = _rms(o) * gng_ref[...]
    for h in hs:
        gate = proj_ref[0, :, CONV_CH + h * GDN_DV:CONV_CH + (h + 1) * GDN_DV].astype(f32)
        o_ref[0, :, h * GDN_DV:(h + 1) * GDN_DV] = (oa[h][0:lv] * _silu(gate)).astype(bf16)

    rep = SSM_HEADS // SSM_G
    xs0 = 2 * GDN_QK + GDN_V
    bm0 = xs0 + SSM_INNER
    cm0 = bm0 + SSM_BC
    hs = range(SSM_HEADS)
    bmb = [act[:, bm0 + g * SSM_N:bm0 + (g + 1) * SSM_N].astype(bf16) for g in range(SSM_G)]
    cmb = [act[:, cm0 + g * SSM_N:cm0 + (g + 1) * SSM_N].astype(bf16) for g in range(SSM_G)]
    cbg = [_nt(cmb[g], bmb[g]) for g in range(SSM_G)]
    cb = jnp.stack([cbg[h // rep] for h in hs])
    bm16 = jnp.stack([bmb[h // rep] for h in hs])
    cm16 = jnp.stack([cmb[h // rep] for h in hs])
    xs = jnp.stack([act[:, xs0 + h * SSM_P:xs0 + (h + 1) * SSM_P] for h in hs])
    dsk = jnp.stack([dskip_ref[:, h * SSM_P:(h + 1) * SSM_P] for h in hs])
    cum_c, ecum_c, dtelc_c = (colstack(b, SSM_HEADS) for b in (40, 56, 72))
    m = cb * jnp.exp(jnp.where(incl, cum_c - rowstack(cum_t, s0, SSM_HEADS), -jnp.inf)) * rowstack(sp_t, s0, SSM_HEADS)
    s_old = ss[...]
    o = _bmm(m.astype(bf16), xs.astype(bf16)) + _bmm(cm16, s_old.astype(bf16)) * ecum_c + dsk * xs
    ss[...] = s_old * rowstack(elast_t, s0, SSM_HEADS) + _btn(bm16, (xs * dtelc_c).astype(bf16))
    for h in hs:
        ob[:, h * SSM_P:(h + 1) * SSM_P] = o[h]
    z = proj_ref[0, :, CONV_CH + GDN_V:CONV_CH + GDN_V + SSM_INNER].astype(f32)
    obv = _rms(ob[0:lv, :] * _silu(z)) * sng_ref[...]
    o_ref[0, :, GDN_V:GDN_V + SSM_INNER] = obv.astype(bf16)

    @pl.when(l == nl - 1)
    def _():
        convn_ref[0] = xbuf[5:8, :]
        gdnn_ref[0] = sg[...]
        ssmn_ref[0] = ss[...]


def mixer_ab(proj, smt, conv0, gdn0, ssm0, conv_w, conv_b, pcol, gdn_norm_g, dskip, ssm_norm_g, c):
    b, l, npad = proj.shape
    tl = min(c, l)
    nl = l // tl
    kern = functools.partial(_ab_kernel, c=c, lv=tl)
    full = lambda shape: pl.BlockSpec(shape, lambda i, j: (0,) * len(shape))
    return pl.pallas_call(
        kern,
        grid=(b, nl),
        in_specs=[
            pl.BlockSpec((1, tl, npad), lambda i, j: (i, j, 0)),
            pl.BlockSpec((1, 1, AB_SMALL, tl), lambda i, j: (i, j, 0, 0)),
            pl.BlockSpec((1, CONV_W - 1, CONV_CH), lambda i, j: (i, 0, 0)),
            pl.BlockSpec((1, GDN_HEADS, GDN_DK, GDN_DV), lambda i, j: (i, 0, 0, 0)),
            pl.BlockSpec((1, SSM_HEADS, SSM_P, SSM_N), lambda i, j: (i, 0, 0, 0)),
            full((CONV_W, CONV_CH)),
            full((1, CONV_CH)),
            full((AB_SMALL, 2)),
            full((1, GDN_DV)),
            full((1, SSM_INNER)),
            full((1, SSM_INNER)),
        ],
        out_specs=[
            pl.BlockSpec((1, tl, GDN_V + SSM_INNER), lambda i, j: (i, j, 0)),
            pl.BlockSpec((1, CONV_W - 1, CONV_CH), lambda i, j: (i, 0, 0)),
            pl.BlockSpec((1, GDN_HEADS, GDN_DK, GDN_DV), lambda i, j: (i, 0, 0, 0)),
            pl.BlockSpec((1, SSM_HEADS, SSM_N, SSM_P), lambda i, j: (i, 0, 0, 0)),
        ],
        out_shape=[
            jax.ShapeDtypeStruct((b, l, GDN_V + SSM_INNER), bf16),
            jax.ShapeDtypeStruct((b, CONV_W - 1, CONV_CH), f32),
            jax.ShapeDtypeStruct((b, GDN_HEADS, GDN_DK, GDN_DV), f32),
            jax.ShapeDtypeStruct((b, SSM_HEADS, SSM_N, SSM_P), f32),
        ],
        scratch_shapes=[
            pltpu.VMEM((8 + c, CONV_CH), f32),
            pltpu.VMEM((c, CONV_CH), f32),
            pltpu.VMEM((GDN_HEADS, GDN_DK, GDN_DV), f32),
            pltpu.VMEM((SSM_HEADS, SSM_N, SSM_P), f32),
            pltpu.VMEM((c, SSM_INNER), f32),
        ],
        compiler_params=_params(("arbitrary", "arbitrary")),
        name="mixer_ab",
    )(proj, smt, conv0, gdn0, ssm0, conv_w, conv_b, pcol, gdn_norm_g, dskip, ssm_norm_g)


def _ret_kernel(proj_ref, cos_ref, sin_ref, ret0_ref, ng_ref, o_ref, retn_ref, st, buf, *, c, lv):
    l = pl.program_id(1)
    nl = pl.num_programs(1)

    @pl.when(l == 0)
    def _():
        st[...] = ret0_ref[0]
        if lv < c:
            buf[...] = jnp.zeros(buf.shape, f32)

    if lv == c:
        src = proj_ref.at[0]
        cos = cos_ref[...]
        sin = sin_ref[...]
    else:
        buf[0:lv, :] = proj_ref[0].astype(f32)
        src = buf
        cos = jnp.broadcast_to(cos_ref[...], (c, RET_DK // 2))
        sin = jnp.broadcast_to(sin_ref[...], (c, RET_DK // 2))

    ri = lax.broadcasted_iota(jnp.int32, (c, c), 0)
    ci = lax.broadcasted_iota(jnp.int32, (c, c), 1)
    incl = ri >= ci
    cnt_r = jnp.minimum(ri + 1, lv).astype(f32)
    cnt_c = jnp.minimum(ci + 1, lv).astype(f32)
    cnt_col = jnp.minimum(lax.broadcasted_iota(jnp.int32, (c, 1), 0) + 1, lv).astype(f32)
    half = RET_DK // 2

    def rope(x):
        x1, x2 = x[:, :half], x[:, half:]
        return jnp.concatenate([x1 * cos - x2 * sin, x1 * sin + x2 * cos], axis=1)

    for h in range(RET_HEADS):
        lg = math.log(1.0 - 2.0 ** (-5.0 - h))
        q = rope(src[:, h * RET_DK:(h + 1) * RET_DK].astype(f32))
        k = rope(src[:, RET_QK + h * RET_DK:RET_QK + (h + 1) * RET_DK].astype(f32)) * (RET_DK ** -0.5)
        v = src[:, 2 * RET_QK + h * RET_DV:2 * RET_QK + (h + 1) * RET_DV].astype(bf16)
        gate = src[:, 2 * RET_QK + RET_V + h * RET_DV:2 * RET_QK + RET_V + (h + 1) * RET_DV].astype(f32)
        dec = jnp.exp(jnp.where(incl, (cnt_r - cnt_c) * lg, -jnp.inf))
        scores = _nt(q.astype(bf16), k.astype(bf16)) * dec
        s_old = st[h]
        o = _mm(scores.astype(bf16), v) + _mm((q * jnp.exp(cnt_col * lg)).astype(bf16), s_old.astype(bf16))
        kd = (k * jnp.exp((lv - cnt_col) * lg)).astype(bf16)
        st[h] = s_old * math.exp(lv * lg) + _tn(kd, v)
        mu = jnp.mean(o, axis=-1, keepdims=True)
        var = jnp.mean(jnp.square(o - mu), axis=-1, keepdims=True)
        o = (o - mu) * lax.rsqrt(var + EPS) * ng_ref[h:h + 1, :]
        o_ref[0, :, h * RET_DV:(h + 1) * RET_DV] = (_silu(gate) * o)[0:lv].astype(bf16)

    @pl.when(l == nl - 1)
    def _():
        retn_ref[0] = st[...]


def mixer_ret(proj, cos, sin, ret0, norm_g, c):
    b, l, n = proj.shape
    tl = min(c, l)
    nl = l // tl
    kern = functools.partial(_ret_kernel, c=c, lv=tl)
    return pl.pallas_call(
        kern,
        grid=(b, nl),
        in_specs=[
            pl.BlockSpec((1, tl, n), lambda i, j: (i, j, 0)),
            pl.BlockSpec((tl, RET_DK // 2), lambda i, j: (j, 0)),
            pl.BlockSpec((tl, RET_DK // 2), lambda i, j: (j, 0)),
            pl.BlockSpec((1, RET_HEADS, RET_DK, RET_DV), lambda i, j: (i, 0, 0, 0)),
            pl.BlockSpec((RET_HEADS, RET_DV), lambda i, j: (0, 0)),
        ],
        out_specs=[
            pl.BlockSpec((1, tl, RET_V), lambda i, j: (i, j, 0)),
            pl.BlockSpec((1, RET_HEADS, RET_DK, RET_DV), lambda i, j: (i, 0, 0, 0)),
        ],
        out_shape=[
            jax.ShapeDtypeStruct((b, l, RET_V), bf16),
            jax.ShapeDtypeStruct((b, RET_HEADS, RET_DK, RET_DV), f32),
        ],
        scratch_shapes=[
            pltpu.VMEM((RET_HEADS, RET_DK, RET_DV), f32),
            pltpu.VMEM((c, n), f32),
        ],
        compiler_params=_params(("arbitrary", "arbitrary")),
        name="mixer_ret",
    )(proj, cos, sin, ret0, norm_g)


def _top16(s, exact):
    g, n, t = s.shape
    rows = lax.broadcasted_iota(jnp.int32, (g, n, t), 1)
    krow = lax.broadcasted_iota(jnp.int32, (g, PEER_TOPK, t), 1)
    rank = jnp.full((g, n, t), float(PEER_TOPK), f32)
    sv = jnp.zeros((g, PEER_TOPK, t), f32)
    work = s
    for k in range(PEER_TOPK):
        m = jnp.max(work, axis=1, keepdims=True)
        if exact:
            idx = jnp.min(jnp.where(work == m, rows, n), axis=1, keepdims=True)
            sel = rows == idx
        else:
            sel = work == m
        rank = jnp.where(sel, float(k), rank)
        work = jnp.where(sel, -jnp.inf, work)
        sv = jnp.where(krow == k, m, sv)
    return sv, rank


def _miscount(rank):
    cnt = jnp.sum(jnp.where(rank < float(PEER_TOPK), 1.0, 0.0), axis=1, keepdims=True)
    return jnp.max(jnp.abs(cnt - float(PEER_TOPK)))


def _topk_kernel(x_ref, g_ref, sc_ref, sh_ref, wqt_ref, keys_ref, a1_ref, a2_ref, a1t_ref,
                 hmt_ref, r2_ref, e2_ref, lc_ref, g1_ref, sv_scr, rk_scr, rc_scr):
    y = _rms(x_ref[...]) * g_ref[...]
    hm = y * (1.0 + sc_ref[0]) + sh_ref[0]
    hmt = hm.T.astype(bf16)
    hmt_ref[...] = hmt
    qt = _mm(wqt_ref[...], hmt)
    tt = qt.shape[1]
    half = PEER_DQ // 2
    hs = range(PEER_HEADS)
    s_all = jnp.stack(
        [_mm(keys_ref[0], qt[h * PEER_DQ:h * PEER_DQ + half, :].astype(bf16)) for h in hs]
        + [_mm(keys_ref[1], qt[h * PEER_DQ + half:(h + 1) * PEER_DQ, :].astype(bf16)) for h in hs])
    keys_rows = slice(0, PEER_NKEYS)
    pad = jnp.zeros((PEER_HEADS, PEER_NKP - PEER_NKEYS, 128), f32)
    crow = lax.broadcasted_iota(jnp.int32, (PEER_HEADS, _NCAND_PAD, 128), 1)

    def candidates(sv):
        cand = jnp.stack([_mm(a1_ref[...], sv[h], precision=HI) + _mm(a2_ref[...], sv[PEER_HEADS + h], precision=HI)
                          for h in hs])
        return jnp.where(crow < _NCAND, cand, -jnp.inf)

    for lg in range(tt // 128):
        ls = slice(lg * 128, (lg + 1) * 128)
        s_blk = s_all[:, :, ls]

        def stage1(rows, exact):
            sv, rank = _top16(s_blk[rows], exact)
            sv_scr[rows, :, ls] = sv
            rk_scr[rows, :, ls] = rank
            return _miscount(rank)

        def stage2(exact):
            _, rankc = _top16(candidates(sv_scr[:, :, ls]), exact)
            rc_scr[:, :, ls] = rankc
            return _miscount(rankc)

        for rows in (slice(0, PEER_HEADS), slice(PEER_HEADS, 2 * PEER_HEADS)):
            @pl.when(stage1(rows, False) > 0.0)
            def _():
                stage1(rows, True)

        @pl.when(stage2(False) > 0.0)
        def _():
            stage2(True)

        sv = sv_scr[:, :, ls]
        sv1, sv2 = sv[:PEER_HEADS], sv[PEER_HEADS:]
        rank1 = rk_scr[0:PEER_HEADS, :, ls]
        sel_all = jnp.where(rc_scr[:, :, ls] < float(PEER_TOPK), 1.0, 0.0)
        cand = candidates(sv)
        z = jnp.sum(sel_all * jnp.exp(jnp.where(crow < _NCAND, cand - cand[:, 0:1, :], 0.0)), axis=1, keepdims=True)
        selb = sel_all.astype(bf16)
        cnt = jnp.stack([_mm(a1t_ref[...], selb[h]) for h in hs])
        lc = jnp.zeros((PEER_HEADS, PEER_NKEYS, 128), f32)
        for k1 in range(PEER_TOPK):
            lc = jnp.where(rank1 == float(k1), cnt[:, k1:k1 + 1, :], lc)
        r2_ref[:, keys_rows, ls] = rk_scr[PEER_HEADS:2 * PEER_HEADS, :, ls]
        e2_ref[:, keys_rows, ls] = jnp.exp(s_blk[PEER_HEADS:] - sv2[:, 0:1, :])
        lc_ref[:, keys_rows, ls] = lc
        g1_ref[:, keys_rows, ls] = jnp.exp(s_blk[:PEER_HEADS] - sv1[:, 0:1, :]) / z
        for ref in (r2_ref, e2_ref, lc_ref, g1_ref):
            ref[:, PEER_NKEYS:, ls] = pad


def peer_topk(x, gamma, sc, sh, mod_map, wq_t, keys, tt):
    t = x.shape[0]
    r = sc.shape[1]
    a1 = jnp.zeros((_NCAND_PAD, PEER_TOPK), f32).at[jnp.arange(_NCAND), jnp.array([a for a, _ in _CAND])].set(1.0)
    a2 = jnp.zeros((_NCAND_PAD, PEER_TOPK), f32).at[jnp.arange(_NCAND), jnp.array([b for _, b in _CAND])].set(1.0)
    full = lambda shape: pl.BlockSpec(shape, lambda i: (0,) * len(shape))
    tok = pl.BlockSpec((PEER_HEADS, PEER_NKP, tt), lambda i: (0, 0, i))
    tok_shape = jax.ShapeDtypeStruct((PEER_HEADS, PEER_NKP, t), f32)
    return pl.pallas_call(
        _topk_kernel,
        grid=(t // tt,),
        in_specs=[
            pl.BlockSpec((tt, D_MODEL), lambda i: (i, 0)),
            full((1, D_MODEL)),
            pl.BlockSpec((1, r, D_MODEL), lambda i: mod_map(i)),
            pl.BlockSpec((1, r, D_MODEL), lambda i: mod_map(i)),
            full((PEER_HEADS * PEER_DQ, D_MODEL)),
            full((2, PEER_NKEYS, PEER_DQ // 2)),
            full((_NCAND_PAD, PEER_TOPK)),
            full((_NCAND_PAD, PEER_TOPK)),
            full((PEER_TOPK, _NCAND_PAD)),
        ],
        out_specs=[pl.BlockSpec((D_MODEL, tt), lambda i: (0, i)), tok, tok, tok, tok],
        out_shape=[jax.ShapeDtypeStruct((D_MODEL, t), bf16), tok_shape, tok_shape, tok_shape, tok_shape],
        scratch_shapes=[
            pltpu.VMEM((2 * PEER_HEADS, PEER_TOPK, tt), f32),
            pltpu.VMEM((2 * PEER_HEADS, PEER_NKEYS, tt), f32),
            pltpu.VMEM((PEER_HEADS, _NCAND_PAD, tt), f32),
        ],
        compiler_params=_params(("arbitrary",)),
        name="peer_topk",
    )(x, gamma.reshape(1, D_MODEL), sc, sh, wq_t, keys, a1, a2, a1.T.astype(bf16))


def _peer_kernel(hmt_ref, u_ref, vt_ref, r2_ref, e2_ref, lc_ref, g1_ref, x_ref, g_ref, fg_ref, o_ref,
                 yt, act_a, act_b, ht_a, ht_b, *, nsub, final):
    j = pl.program_id(1)
    nj = pl.num_programs(1)
    na = PEER_SB // PEER_NKEYS
    tt = hmt_ref.shape[1]

    @pl.when(j == 0)
    def _():
        yt[...] = jnp.zeros(yt.shape, f32)

    def mm1(s, ht):
        ht[...] = _mm(u_ref[s], hmt_ref[...])

    def gate_act(s, ht, act):
        for a in range(na):
            n1 = (j * nsub + s) * na + a
            lrows = [lc_ref[h, pl.ds(n1, 1), :] for h in range(PEER_HEADS)]
            grows = [g1_ref[h, pl.ds(n1, 1), :] for h in range(PEER_HEADS)]
            for lg in range(tt // 128):
                ls = slice(lg * 128, (lg + 1) * 128)
                w = None
                for h in range(PEER_HEADS):
                    wh = jnp.where(r2_ref[h, 0:PEER_NKEYS, ls] < lrows[h][:, ls], e2_ref[h, 0:PEER_NKEYS, ls] * grows[h][:, ls], 0.0)
                    w = wh if w is None else w + wh
                hb = ht[a * PEER_NKEYS:(a + 1) * PEER_NKEYS, ls]
                act[a * PEER_NKEYS:(a + 1) * PEER_NKEYS, ls] = (_gelu(hb) * w).astype(bf16)

    def mm2(s, act):
        yt[...] += _mm(vt_ref[s], act[...])

    act_b[...] = jnp.zeros(act_b.shape, bf16)
    mm1(0, ht_a)

    def body(i, carry):
        s = 2 * i
        mm1(s + 1, ht_b)
        gate_act(s, ht_a, act_a)
        mm2(jnp.maximum(s - 1, 0), act_b)
        mm1(jnp.minimum(s + 2, nsub - 1), ht_a)
        gate_act(s + 1, ht_b, act_b)
        mm2(s, act_a)
        return carry

    lax.fori_loop(0, nsub // 2, body, 0)
    mm2(nsub - 1, act_b)

    @pl.when(j == nj - 1)
    def _():
        xn = x_ref[...] + g_ref[0] * yt[...].T
        if final:
            xn = _rms(xn) * fg_ref[...]
        o_ref[...] = xn


def peer_dense(hmt, u4, vt4, layer, r2, e2, lc, g1, x, gate, mod_map, final_g, tt, eb):
    t = x.shape[0]
    r = gate.shape[1]
    nsub = eb // PEER_SB
    final = final_g is not None
    fg = (final_g if final else jnp.ones((D_MODEL,), f32)).reshape(1, D_MODEL)
    tok = pl.BlockSpec((PEER_HEADS, PEER_NKP, tt), lambda i, j: (0, 0, i))
    return pl.pallas_call(
        functools.partial(_peer_kernel, nsub=nsub, final=final),
        grid=(t // tt, N_EXPERTS // eb),
        in_specs=[
            pl.BlockSpec((D_MODEL, tt), lambda i, j: (0, i)),
            pl.BlockSpec((None, nsub, PEER_SB, D_MODEL), lambda i, j: (layer, j, 0, 0)),
            pl.BlockSpec((None, nsub, D_MODEL, PEER_SB), lambda i, j: (layer, j, 0, 0)),
            tok, tok, tok, tok,
            pl.BlockSpec((tt, D_MODEL), lambda i, j: (i, 0)),
            pl.BlockSpec((1, r, D_MODEL), lambda i, j: mod_map(i)),
            pl.BlockSpec((1, D_MODEL), lambda i, j: (0, 0)),
        ],
        out_specs=pl.BlockSpec((tt, D_MODEL), lambda i, j: (i, 0)),
        out_shape=jax.ShapeDtypeStruct((t, D_MODEL), f32),
        scratch_shapes=[
            pltpu.VMEM((D_MODEL, tt), f32),
            pltpu.VMEM((PEER_SB, tt), bf16), pltpu.VMEM((PEER_SB, tt), bf16),
            pltpu.VMEM((PEER_SB, tt), f32), pltpu.VMEM((PEER_SB, tt), f32),
        ],
        compiler_params=_params(("arbitrary", "arbitrary")),
        name="peer_dense",
    )(hmt, u4, vt4, r2, e2, lc, g1, x, gate, fg)


def _prepare(ab_w_in, ab_conv_w, ab_conv_b, gdn_a_log, gdn_dt_bias, gdn_norm_g, ssm_a_log, ssm_dt_bias, ssm_d,
             ssm_norm_g, ab_w_out, ret_w_in, ret_norm_g, ret_w_out, peer_w_q, peer_keys, peer_u, peer_v):
    zeros8 = jnp.zeros((GDN_HEADS,), f32)
    nsb = N_EXPERTS // PEER_SB
    return dict(
        ab_w_main=ab_w_in[0][:, :AB_MAIN].astype(bf16),
        ab_w_small_t=ab_w_in[0][:, AB_MAIN:].T.astype(bf16),
        conv_w=ab_conv_w[0],
        conv_b=ab_conv_b[0].reshape(1, CONV_CH),
        pcol=jnp.stack([jnp.concatenate([zeros8, gdn_dt_bias[0], ssm_dt_bias[0]]),
                        jnp.concatenate([zeros8, gdn_a_log[0], ssm_a_log[0]])], axis=1),
        gdn_norm_g=gdn_norm_g[0].reshape(1, GDN_DV),
        dskip=jnp.repeat(ssm_d[0], SSM_P).reshape(1, SSM_INNER),
        ssm_norm_g=ssm_norm_g[0].reshape(1, SSM_INNER),
        ab_w_out=ab_w_out[0].astype(bf16),
        ret_w_in=ret_w_in[0].astype(bf16),
        ret_norm_g=ret_norm_g[0],
        ret_w_out=ret_w_out[0].astype(bf16),
        wq_t=[peer_w_q[i].T.astype(bf16) for i in range(DEPTH)],
        keys=[peer_keys[i].astype(bf16) for i in range(DEPTH)],
        u=peer_u.astype(bf16).reshape(DEPTH, nsb, PEER_SB, D_MODEL),
        vt=peer_v.astype(bf16).reshape(DEPTH, nsb, PEER_SB, D_MODEL).transpose(0, 1, 3, 2),
    )


def _rope_tables(pos0, length):
    inv = ROPE_BASE ** (-jnp.arange(0, RET_DK, 2, dtype=f32) / RET_DK)
    ang = (pos0 + jnp.arange(length, dtype=f32))[:, None] * inv[None, :]
    return jnp.cos(ang), jnp.sin(ang)


def _trunk(x, mods, pos0, conv0, gdn0, ssm0, ret0, p, norm1_g, norm2_g, final_g):
    b, l, _ = x.shape
    t = b * l
    c = CHUNK if l >= CHUNK else 8
    xt = x.reshape(t, D_MODEL)
    if l > 1:
        tm = min(512, l)
        tt = min(256, l)
        mod_arr = lambda m: m.reshape(b, 1, D_MODEL)
        mod_map = lambda rows: (lambda i: (i * rows // l, 0, 0))
    else:
        tm = t
        tt = t
        mod_arr = lambda m: m.reshape(1, t, D_MODEL)
        mod_map = lambda rows: (lambda i: (0, i, 0))
    convs = gdns = ssms = rets = None
    for layer in range(DEPTH):
        sh1, sc1, g1, sh2, sc2, g2 = (mod_arr(m) for m in jnp.split(mods[layer], 6, axis=-1))
        if layer == 0:
            proj, smt = norm_proj(xt, norm1_g[layer], sc1, sh1, mod_map(tm), p["ab_w_main"], p["ab_w_small_t"],
                                  tm, AB_MAIN // 4)
            tl = min(c, l)
            smt = smt.reshape(AB_SMALL, b, l // tl, tl).transpose(1, 2, 0, 3)
            o, convs, gdns, ssms = mixer_ab(proj.reshape(b, l, -1), smt, conv0, gdn0, jnp.swapaxes(ssm0, -1, -2),
                                            p["conv_w"], p["conv_b"],
                                            p["pcol"], p["gdn_norm_g"], p["dskip"], p["ssm_norm_g"], c)
            xt = out_proj(o.reshape(t, -1), p["ab_w_out"], xt, g1, mod_map(tm), tm)
        else:
            proj = norm_proj(xt, norm1_g[layer], sc1, sh1, mod_map(tm), p["ret_w_in"], None, tm, RET_IN // 4)
            cos, sin = _rope_tables(pos0, l)
            o, rets = mixer_ret(proj.reshape(b, l, -1), cos, sin, ret0, p["ret_norm_g"], c)
            xt = out_proj(o.reshape(t, -1), p["ret_w_out"], xt, g1, mod_map(tm), tm)
        hmt, r2, e2, lc, gg = peer_topk(xt, norm2_g[layer], sc2, sh2, mod_map(tt), p["wq_t"][layer], p["keys"][layer], tt)
        xt = peer_dense(hmt, p["u"], p["vt"], layer, r2, e2, lc, gg, xt, g2, mod_map(tt),
                        final_g if layer == DEPTH - 1 else None, tt, PEER_EB)
    return xt.reshape(b, l, D_MODEL), convs[None], gdns[None], ssms[None], rets[None]


def kernel(x_prompt, x_sample, c_prompt, c_sample, state_conv, state_gdn, state_ssm, state_ret, ada_w, ada_b,
           norm1_g, norm2_g, ab_w_in, ab_conv_w, ab_conv_b, gdn_a_log, gdn_dt_bias, gdn_norm_g, ssm_a_log,
           ssm_dt_bias, ssm_d, ssm_norm_g, ab_w_out, ret_w_in, ret_norm_g, ret_w_out, peer_w_q, peer_keys,
           peer_u, peer_v, final_g):
    p = _prepare(ab_w_in, ab_conv_w, ab_conv_b, gdn_a_log, gdn_dt_bias, gdn_norm_g, ssm_a_log, ssm_dt_bias, ssm_d,
                 ssm_norm_g, ab_w_out, ret_w_in, ret_norm_g, ret_w_out, peer_w_q, peer_keys, peer_u, peer_v)
    nb = x_prompt.shape[0]
    mods = ada_mod(jnp.concatenate([c_prompt, c_sample], axis=0), ada_w, ada_b)
    zeros = lambda s: jnp.zeros((nb,) + s.shape[2:], s.dtype)
    y_p, p_conv, p_gdn, p_ssm, p_ret = _trunk(
        x_prompt, mods[:, :nb], 0, zeros(state_conv), zeros(state_gdn), zeros(state_ssm), zeros(state_ret),
        p, norm1_g, norm2_g, final_g)
    y_s, s_conv, s_gdn, s_ssm, s_ret = _trunk(
        x_sample, mods[:, nb:], PAST_LEN, state_conv[0], state_gdn[0], state_ssm[0], state_ret[0],
        p, norm1_g, norm2_g, final_g)
    return (y_p, y_s, p_conv, p_gdn, p_ssm, p_ret, s_conv, s_gdn, s_ssm, s_ret)
```

```python
import functools
import math

import jax
import jax.numpy as jnp
from jax import lax
from jax.experimental import pallas as pl
from jax.experimental.pallas import tpu as pltpu

f32 = jnp.float32
bf16 = jnp.bfloat16
HI = lax.Precision.HIGHEST

D_MODEL = 1024
DEPTH = 2
PAST_LEN = 16384
GDN_HEADS = 8
GDN_DK = 128
GDN_DV = 128
SSM_HEADS = 16
SSM_P = 64
SSM_N = 128
SSM_G = 2
CONV_W = 4
RET_HEADS = 4
RET_DK = 256
RET_DV = 512
ROPE_BASE = 10000.0
PEER_HEADS = 8
PEER_NKEYS = 128
PEER_TOPK = 16
PEER_DQ = 256
CHUNK = 64
EPS = 1e-6

GDN_QK = GDN_HEADS * GDN_DK
GDN_V = GDN_HEADS * GDN_DV
SSM_INNER = SSM_HEADS * SSM_P
SSM_BC = SSM_G * SSM_N
CONV_CH = 2 * GDN_QK + GDN_V + SSM_INNER + 2 * SSM_BC
AB_MAIN = CONV_CH + GDN_V + SSM_INNER
AB_SMALL = 2 * GDN_HEADS + SSM_HEADS
RET_QK = RET_HEADS * RET_DK
RET_V = RET_HEADS * RET_DV
RET_IN = 2 * RET_QK + 2 * RET_V
N_EXPERTS = PEER_NKEYS * PEER_NKEYS

VMEM_LIMIT = 56 * 1024 * 1024
PEER_SB = 256
PEER_EB = 4096
PEER_NKP = PEER_NKEYS + 8

_CAND = [(a, b) for a in range(PEER_TOPK) for b in range(PEER_TOPK) if (a + 1) * (b + 1) <= PEER_TOPK]
_NCAND = len(_CAND)
_NCAND_PAD = -(-_NCAND // 16) * 16


def _params(sem):
    return pltpu.CompilerParams(dimension_semantics=sem, vmem_limit_bytes=VMEM_LIMIT)


def _nt(a, b, **kw):
    return lax.dot_general(a, b, (((1,), (1,)), ((), ())), preferred_element_type=f32, **kw)


def _tn(a, b, **kw):
    return lax.dot_general(a, b, (((0,), (0,)), ((), ())), preferred_element_type=f32, **kw)


def _mm(a, b, **kw):
    return jnp.dot(a, b, preferred_element_type=f32, **kw)


def _bmm(a, b, **kw):
    return jnp.einsum('hik,hkj->hij', a, b, preferred_element_type=f32, **kw)


def _bnt(a, b, **kw):
    return jnp.einsum('hik,hjk->hij', a, b, preferred_element_type=f32, **kw)


def _btn(a, b, **kw):
    return jnp.einsum('hki,hkj->hij', a, b, preferred_element_type=f32, **kw)


def _split(x):
    hi = x.astype(bf16)
    return hi, (x - hi.astype(f32)).astype(bf16)


def _bmm3(a, b):
    ah, al = _split(a)
    bh, bl = _split(b)
    return _bmm(ah, bh) + (_bmm(ah, bl) + _bmm(al, bh))


def _silu(x):
    return x * jax.nn.sigmoid(x)


def _rms(x):
    return x * lax.rsqrt(jnp.mean(x * x, axis=-1, keepdims=True) + EPS)


def _mod_kernel(c_ref, w_ref, b_ref, o_ref):
    a = _silu(c_ref[...]).astype(bf16)
    o_ref[0] = _mm(a, w_ref[0].astype(bf16)) + b_ref[0]


def ada_mod(c_all, ada_w, ada_b):
    m = c_all.shape[0]
    tn = 768
    n = ada_w.shape[-1]
    return pl.pallas_call(
        _mod_kernel,
        grid=(DEPTH, n // tn),
        in_specs=[
            pl.BlockSpec((m, D_MODEL), lambda l, j: (0, 0)),
            pl.BlockSpec((1, D_MODEL, tn), lambda l, j: (l, 0, j)),
            pl.BlockSpec((1, 1, tn), lambda l, j: (l, 0, j)),
        ],
        out_specs=pl.BlockSpec((1, m, tn), lambda l, j: (l, 0, j)),
        out_shape=jax.ShapeDtypeStruct((DEPTH, m, n), f32),
        compiler_params=_params(("arbitrary", "arbitrary")),
        name="ada_mod",
    )(c_all, ada_w, ada_b.reshape(DEPTH, 1, n))


def _proj_kernel(x_ref, g_ref, sc_ref, sh_ref, w_ref, *rest, has_small):
    if has_small:
        ws_ref, o_ref, os_ref, hm_ref = rest
    else:
        o_ref, hm_ref = rest
    j = pl.program_id(1)

    @pl.when(j == 0)
    def _():
        y = _rms(x_ref[...]) * g_ref[...]
        hm = (y * (1.0 + sc_ref[0]) + sh_ref[0]).astype(bf16)
        hm_ref[...] = hm
        if has_small:
            os_ref[...] = _nt(ws_ref[...], hm)

    o_ref[...] = _mm(hm_ref[...], w_ref[...]).astype(o_ref.dtype)


def norm_proj(x, gamma, sc, sh, mod_map, w, ws_t, tm, tn):
    t = x.shape[0]
    n = w.shape[1]
    r = sc.shape[1]
    has_small = ws_t is not None
    in_specs = [
        pl.BlockSpec((tm, D_MODEL), lambda i, j: (i, 0)),
        pl.BlockSpec((1, D_MODEL), lambda i, j: (0, 0)),
        pl.BlockSpec((1, r, D_MODEL), lambda i, j: mod_map(i)),
        pl.BlockSpec((1, r, D_MODEL), lambda i, j: mod_map(i)),
        pl.BlockSpec((D_MODEL, tn), lambda i, j: (0, j)),
    ]
    out_specs = [pl.BlockSpec((tm, tn), lambda i, j: (i, j))]
    out_shape = [jax.ShapeDtypeStruct((t, n), bf16)]
    args = [x, gamma.reshape(1, D_MODEL), sc, sh, w]
    if has_small:
        s = ws_t.shape[0]
        in_specs.append(pl.BlockSpec((s, D_MODEL), lambda i, j: (0, 0)))
        out_specs.append(pl.BlockSpec((s, tm), lambda i, j: (0, i)))
        out_shape.append(jax.ShapeDtypeStruct((s, t), f32))
        args.append(ws_t)
    res = pl.pallas_call(
        functools.partial(_proj_kernel, has_small=has_small),
        grid=(t // tm, n // tn),
        in_specs=in_specs,
        out_specs=out_specs,
        out_shape=out_shape,
        scratch_shapes=[pltpu.VMEM((tm, D_MODEL), bf16)],
        compiler_params=_params(("arbitrary", "arbitrary")),
        name="norm_proj",
    )(*args)
    return res if has_small else res[0]


def _out_kernel(o_ref, w_ref, x_ref, g_ref, y_ref):
    y_ref[...] = x_ref[...] + g_ref[0] * _mm(o_ref[...], w_ref[...])


def out_proj(o, w, x, gate, mod_map, tm):
    t, k = o.shape
    r = gate.shape[1]
    return pl.pallas_call(
        _out_kernel,
        grid=(t // tm,),
        in_specs=[
            pl.BlockSpec((tm, k), lambda i: (i, 0)),
            pl.BlockSpec((k, D_MODEL), lambda i: (0, 0)),
            pl.BlockSpec((tm, D_MODEL), lambda i: (i, 0)),
            pl.BlockSpec((1, r, D_MODEL), lambda i: mod_map(i)),
        ],
        out_specs=pl.BlockSpec((tm, D_MODEL), lambda i: (i, 0)),
        out_shape=jax.ShapeDtypeStruct((t, D_MODEL), f32),
        compiler_params=_params(("arbitrary",)),
        name="out_proj",
    )(o, w, x, gate)


def _unit_lower_inverse(lm, c):
    eye = (lax.broadcasted_iota(jnp.int32, (c, c), 0) == lax.broadcasted_iota(jnp.int32, (c, c), 1)).astype(f32)
    p = eye - lm
    lp = lm
    k = 2
    while k < c:
        lp = _bmm3(lp, lp)
        p = p + _bmm3(p, lp)
        k *= 2
    return p


def _ab_kernel(proj_ref, smt_ref, conv0_ref, gdn0_ref, ssm0_ref, convw_ref, convb_ref, pcol_ref, gng_ref,
               dskip_ref, sng_ref, o_ref, convn_ref, gdnn_ref, ssmn_ref, xbuf, act, sg, ss, ob, *, c, lv):
    l = pl.program_id(1)
    nl = pl.num_programs(1)

    @pl.when(l == 0)
    def _():
        xbuf[...] = jnp.zeros(xbuf.shape, f32)
        xbuf[5:8, :] = conv0_ref[0]
        sg[...] = gdn0_ref[0]
        for h in range(SSM_HEADS):
            ss[h] = ssm0_ref[0, h].T

    xbuf[8:8 + lv, :] = proj_ref[0, :, 0:CONV_CH].astype(f32)
    y = convb_ref[...]
    for i in range(CONV_W):
        y = y + convw_ref[i:i + 1, :] * xbuf[5 + i:5 + i + c, :]
    act[...] = _silu(y)
    tail = xbuf[5 + lv:8 + lv, :]
    xbuf[5:8, :] = tail

    if lv == c:
        sm = smt_ref[0, 0]
    else:
        lane = lax.broadcasted_iota(jnp.int32, (AB_SMALL, c), 1)
        sm = jnp.where(lane < lv, jnp.broadcast_to(smt_ref[0, 0], (AB_SMALL, c)), 0.0)
    valid = lax.broadcasted_iota(jnp.int32, (AB_SMALL, c), 1) < lv
    bias_col = pcol_ref[:, 0:1]
    alog_col = pcol_ref[:, 1:2]
    beta_t = jnp.where(valid, jax.nn.sigmoid(sm), 0.0)
    sp_t = jnp.where(valid, jax.nn.softplus(sm + bias_col), 0.0)
    la_t = -jnp.exp(alog_col) * sp_t
    rr = lax.broadcasted_iota(jnp.int32, (c, c), 0)
    qq = lax.broadcasted_iota(jnp.int32, (c, c), 1)
    triu = (rr <= qq).astype(f32)
    eye = (rr == qq).astype(f32)
    cum_t = _mm(la_t, triu, precision=HI)
    last_t = cum_t[:, c - 1:c]
    ecum_t = jnp.exp(cum_t)
    elc_t = jnp.exp(last_t - cum_t)
    elast_t = jnp.exp(last_t)
    g0, g1, s0, s1 = 0, GDN_HEADS, 2 * GDN_HEADS, AB_SMALL
    rows = jnp.concatenate([
        beta_t[g0:g1],
        cum_t[g1:s0],
        ecum_t[g1:s0],
        beta_t[g0:g1] * ecum_t[g1:s0],
        elc_t[g1:s0],
        cum_t[s0:s1],
        ecum_t[s0:s1],
        sp_t[s0:s1] * elc_t[s0:s1],
        jnp.zeros((128 - 88, c), f32),
    ], axis=0)
    cols = _nt(eye, rows, precision=HI)
    incl = rr >= qq
    strict = rr > qq

    def colstack(base, n):
        return jnp.stack([cols[:, base + h:base + h + 1] for h in range(n)])

    def rowstack(x, base, n):
        return jnp.stack([x[base + h:base + h + 1, :] for h in range(n)])

    hs = range(GDN_HEADS)
    q = jnp.stack([act[:, h * GDN_DK:(h + 1) * GDN_DK] for h in hs])
    k = jnp.stack([act[:, GDN_QK + h * GDN_DK:GDN_QK + (h + 1) * GDN_DK] for h in hs])
    v = jnp.stack([act[:, 2 * GDN_QK + h * GDN_DV:2 * GDN_QK + (h + 1) * GDN_DV] for h in hs])
    q = q * lax.rsqrt(jnp.sum(q * q, axis=-1, keepdims=True) + EPS) * (GDN_DK ** -0.5)
    k = k * lax.rsqrt(jnp.sum(k * k, axis=-1, keepdims=True) + EPS)
    beta_c, cum_c, ecum_c, becum_c, elc_c = (colstack(b, GDN_HEADS) for b in (0, 8, 16, 24, 32))
    dec = jnp.exp(jnp.where(incl, cum_c - rowstack(cum_t, g1, GDN_HEADS), -jnp.inf))
    kb = k.astype(bf16)
    qb = q.astype(bf16)
    lm = jnp.where(strict, beta_c * _bnt(kb, kb) * dec, 0.0)
    pinv = _unit_lower_inverse(lm, c)
    rhs = jnp.concatenate([v * beta_c, k * becum_c], axis=2)
    sol = _bmm3(pinv, rhs)
    u0 = sol[:, :, :GDN_DV]
    w = sol[:, :, GDN_DV:]
    qk = _bnt(qb, kb) * dec
    s_old = sg[...]
    sb = s_old.astype(bf16)
    u = u0 - _bmm(w.astype(bf16), sb)
    ub = u.astype(bf16)
    o = _bmm((q * ecum_c).astype(bf16), sb) + _bmm(qk.astype(bf16), ub)
    kd = (k * elc_c).astype(bf16)
    sg[...] = s_old * rowstack(elast_t, g1, GDN_HEADS) + _btn(kd, ub)
    oa = _rms(o) * gng_ref[...]
    for h in hs:
        gate = proj_ref[0, :, CONV_CH + h * GDN_DV:CONV_CH + (h + 1) * GDN_DV].astype(f32)
        o_ref[0, :, h * GDN_DV:(h + 1) * GDN_DV] = (oa[h][0:lv] * _silu(gate)).astype(bf16)

    rep = SSM_HEADS // SSM_G
    xs0 = 2 * GDN_QK + GDN_V
    bm0 = xs0 + SSM_INNER
    cm0 = bm0 + SSM_BC
    hs = range(SSM_HEADS)
    bmb = [act[:, bm0 + g * SSM_N:bm0 + (g + 1) * SSM_N].astype(bf16) for g in range(SSM_G)]
    cmb = [act[:, cm0 + g * SSM_N:cm0 + (g + 1) * SSM_N].astype(bf16) for g in range(SSM_G)]
    cbg = [_nt(cmb[g], bmb[g]) for g in range(SSM_G)]
    cb = jnp.stack([cbg[h // rep] for h in hs])
    bm16 = jnp.stack([bmb[h // rep] for h in hs])
    cm16 = jnp.stack([cmb[h // rep] for h in hs])
    xs = jnp.stack([act[:, xs0 + h * SSM_P:xs0 + (h + 1) * SSM_P] for h in hs])
    dsk = jnp.stack([dskip_ref[:, h * SSM_P:(h + 1) * SSM_P] for h in hs])
    cum_c, ecum_c, dtelc_c = (colstack(b, SSM_HEADS) for b in (40, 56, 72))
    m = cb * jnp.exp(jnp.where(incl, cum_c - rowstack(cum_t, s0, SSM_HEADS), -jnp.inf)) * rowstack(sp_t, s0, SSM_HEADS)
    s_old = ss[...]
    o = _bmm(m.astype(bf16), xs.astype(bf16)) + _bmm(cm16, s_old.astype(bf16)) * ecum_c + dsk * xs
    ss[...] = s_old * rowstack(elast_t, s0, SSM_HEADS) + _btn(bm16, (xs * dtelc_c).astype(bf16))
    for h in hs:
        ob[:, h * SSM_P:(h + 1) * SSM_P] = o[h]
    z = proj_ref[0, :, CONV_CH + GDN_V:CONV_CH + GDN_V + SSM_INNER].astype(f32)
    obv = _rms(ob[0:lv, :] * _silu(z)) * sng_ref[...]
    o_ref[0, :, GDN_V:GDN_V + SSM_INNER] = obv.astype(bf16)

    @pl.when(l == nl - 1)
    def _():
        convn_ref[0] = xbuf[5:8, :]
        gdnn_ref[0] = sg[...]
        ssmn_ref[0] = ss[...]


def mixer_ab(proj, smt, conv0, gdn0, ssm0, conv_w, conv_b, pcol, gdn_norm_g, dskip, ssm_norm_g, c):
    b, l, npad = proj.shape
    tl = min(c, l)
    nl = l // tl
    kern = functools.partial(_ab_kernel, c=c, lv=tl)
    full = lambda shape: pl.BlockSpec(shape, lambda i, j: (0,) * len(shape))
    return pl.pallas_call(
        kern,
        grid=(b, nl),
        in_specs=[
            pl.BlockSpec((1, tl, npad), lambda i, j: (i, j, 0)),
            pl.BlockSpec((1, 1, AB_SMALL, tl), lambda i, j: (i, j, 0, 0)),
            pl.BlockSpec((1, CONV_W - 1, CONV_CH), lambda i, j: (i, 0, 0)),
            pl.BlockSpec((1, GDN_HEADS, GDN_DK, GDN_DV), lambda i, j: (i, 0, 0, 0)),
            pl.BlockSpec((1, SSM_HEADS, SSM_P, SSM_N), lambda i, j: (i, 0, 0, 0)),
            full((CONV_W, CONV_CH)),
            full((1, CONV_CH)),
            full((AB_SMALL, 2)),
            full((1, GDN_DV)),
            full((1, SSM_INNER)),
            full((1, SSM_INNER)),
        ],
        out_specs=[
            pl.BlockSpec((1, tl, GDN_V + SSM_INNER), lambda i, j: (i, j, 0)),
            pl.BlockSpec((1, CONV_W - 1, CONV_CH), lambda i, j: (i, 0, 0)),
            pl.BlockSpec((1, GDN_HEADS, GDN_DK, GDN_DV), lambda i, j: (i, 0, 0, 0)),
            pl.BlockSpec((1, SSM_HEADS, SSM_N, SSM_P), lambda i, j: (i, 0, 0, 0)),
        ],
        out_shape=[
            jax.ShapeDtypeStruct((b, l, GDN_V + SSM_INNER), bf16),
            jax.ShapeDtypeStruct((b, CONV_W - 1, CONV_CH), f32),
            jax.ShapeDtypeStruct((b, GDN_HEADS, GDN_DK, GDN_DV), f32),
            jax.ShapeDtypeStruct((b, SSM_HEADS, SSM_N, SSM_P), f32),
        ],
        scratch_shapes=[
            pltpu.VMEM((8 + c, CONV_CH), f32),
            pltpu.VMEM((c, CONV_CH), f32),
            pltpu.VMEM((GDN_HEADS, GDN_DK, GDN_DV), f32),
            pltpu.VMEM((SSM_HEADS, SSM_N, SSM_P), f32),
            pltpu.VMEM((c, SSM_INNER), f32),
        ],
        compiler_params=_params(("arbitrary", "arbitrary")),
        name="mixer_ab",
    )(proj, smt, conv0, gdn0, ssm0, conv_w, conv_b, pcol, gdn_norm_g, dskip, ssm_norm_g)


def _ret_kernel(proj_ref, cos_ref, sin_ref, ret0_ref, ng_ref, o_ref, retn_ref, st, buf, *, c, lv):
    l = pl.program_id(1)
    nl = pl.num_programs(1)

    @pl.when(l == 0)
    def _():
        st[...] = ret0_ref[0]
        if lv < c:
            buf[...] = jnp.zeros(buf.shape, f32)

    if lv == c:
        src = proj_ref.at[0]
        cos = cos_ref[...]
        sin = sin_ref[...]
    else:
        buf[0:lv, :] = proj_ref[0].astype(f32)
        src = buf
        cos = jnp.broadcast_to(cos_ref[...], (c, RET_DK // 2))
        sin = jnp.broadcast_to(sin_ref[...], (c, RET_DK // 2))

    ri = lax.broadcasted_iota(jnp.int32, (c, c), 0)
    ci = lax.broadcasted_iota(jnp.int32, (c, c), 1)
    incl = ri >= ci
    cnt_r = jnp.minimum(ri + 1, lv).astype(f32)
    cnt_c = jnp.minimum(ci + 1, lv).astype(f32)
    cnt_col = jnp.minimum(lax.broadcasted_iota(jnp.int32, (c, 1), 0) + 1, lv).astype(f32)
    half = RET_DK // 2

    def rope(x):
        x1, x2 = x[:, :half], x[:, half:]
        return jnp.concatenate([x1 * cos - x2 * sin, x1 * sin + x2 * cos], axis=1)

    for h in range(RET_HEADS):
        lg = math.log(1.0 - 2.0 ** (-5.0 - h))
        q = rope(src[:, h * RET_DK:(h + 1) * RET_DK].astype(f32))
        k = rope(src[:, RET_QK + h * RET_DK:RET_QK + (h + 1) * RET_DK].astype(f32)) * (RET_DK ** -0.5)
        v = src[:, 2 * RET_QK + h * RET_DV:2 * RET_QK + (h + 1) * RET_DV].astype(bf16)
        gate = src[:, 2 * RET_QK + RET_V + h * RET_DV:2 * RET_QK + RET_V + (h + 1) * RET_DV].astype(f32)
        dec = jnp.exp(jnp.where(incl, (cnt_r - cnt_c) * lg, -jnp.inf))
        scores = _nt(q.astype(bf16), k.astype(bf16)) * dec
        s_old = st[h]
        o = _mm(scores.astype(bf16), v) + _mm((q * jnp.exp(cnt_col * lg)).astype(bf16), s_old.astype(bf16))
        kd = (k * jnp.exp((lv - cnt_col) * lg)).astype(bf16)
        st[h] = s_old * math.exp(lv * lg) + _tn(kd, v)
        mu = jnp.mean(o, axis=-1, keepdims=True)
        var = jnp.mean(jnp.square(o - mu), axis=-1, keepdims=True)
        o = (o - mu) * lax.rsqrt(var + EPS) * ng_ref[h:h + 1, :]
        o_ref[0, :, h * RET_DV:(h + 1) * RET_DV] = (_silu(gate) * o)[0:lv].astype(bf16)

    @pl.when(l == nl - 1)
    def _():
        retn_ref[0] = st[...]


def mixer_ret(proj, cos, sin, ret0, norm_g, c):
    b, l, n = proj.shape
    tl = min(c, l)
    nl = l // tl
    kern = functools.partial(_ret_kernel, c=c, lv=tl)
    return pl.pallas_call(
        kern,
        grid=(b, nl),
        in_specs=[
            pl.BlockSpec((1, tl, n), lambda i, j: (i, j, 0)),
            pl.BlockSpec((tl, RET_DK // 2), lambda i, j: (j, 0)),
            pl.BlockSpec((tl, RET_DK // 2), lambda i, j: (j, 0)),
            pl.BlockSpec((1, RET_HEADS, RET_DK, RET_DV), lambda i, j: (i, 0, 0, 0)),
            pl.BlockSpec((RET_HEADS, RET_DV), lambda i, j: (0, 0)),
        ],
        out_specs=[
            pl.BlockSpec((1, tl, RET_V), lambda i, j: (i, j, 0)),
            pl.BlockSpec((1, RET_HEADS, RET_DK, RET_DV), lambda i, j: (i, 0, 0, 0)),
        ],
        out_shape=[
            jax.ShapeDtypeStruct((b, l, RET_V), bf16),
            jax.ShapeDtypeStruct((b, RET_HEADS, RET_DK, RET_DV), f32),
        ],
        scratch_shapes=[
            pltpu.VMEM((RET_HEADS, RET_DK, RET_DV), f32),
            pltpu.VMEM((c, n), f32),
        ],
        compiler_params=_params(("arbitrary", "arbitrary")),
        name="mixer_ret",
    )(proj, cos, sin, ret0, norm_g)


def _top16(s, exact):
    g, n, t = s.shape
    rows = lax.broadcasted_iota(jnp.int32, (g, n, t), 1)
    krow = lax.broadcasted_iota(jnp.int32, (g, PEER_TOPK, t), 1)
    rank = jnp.full((g, n, t), float(PEER_TOPK), f32)
    sv = jnp.zeros((g, PEER_TOPK, t), f32)
    work = s
    for k in range(PEER_TOPK):
        m = jnp.max(work, axis=1, keepdims=True)
        if exact:
            idx = jnp.min(jnp.where(work == m, rows, n), axis=1, keepdims=True)
            sel = rows == idx
        else:
            sel = work == m
        rank = jnp.where(sel, float(k), rank)
        work = jnp.where(sel, -jnp.inf, work)
        sv = jnp.where(krow == k, m, sv)
    return sv, rank


def _miscount(rank):
    cnt = jnp.sum(jnp.where(rank < float(PEER_TOPK), 1.0, 0.0), axis=1, keepdims=True)
    return jnp.max(jnp.abs(cnt - float(PEER_TOPK)))


def _topk_kernel(x_ref, g_ref, sc_ref, sh_ref, wqt_ref, keys_ref, a1_ref, a2_ref, a1t_ref,
                 hmt_ref, r2_ref, e2_ref, lc_ref, g1_ref, sv_scr, rk_scr, rc_scr):
    y = _rms(x_ref[...]) * g_ref[...]
    hm = y * (1.0 + sc_ref[0]) + sh_ref[0]
    hmt = hm.T.astype(bf16)
    hmt_ref[...] = hmt
    qt = _mm(wqt_ref[...], hmt)
    tt = qt.shape[1]
    half = PEER_DQ // 2
    hs = range(PEER_HEADS)
    s_all = jnp.stack(
        [_mm(keys_ref[0], qt[h * PEER_DQ:h * PEER_DQ + half, :].astype(bf16)) for h in hs]
        + [_mm(keys_ref[1], qt[h * PEER_DQ + half:(h + 1) * PEER_DQ, :].astype(bf16)) for h in hs])
    keys_rows = slice(0, PEER_NKEYS)
    pad = jnp.zeros((PEER_HEADS, PEER_NKP - PEER_NKEYS, 128), f32)
    crow = lax.broadcasted_iota(jnp.int32, (PEER_HEADS, _NCAND_PAD, 128), 1)

    def candidates(sv):
        cand = jnp.stack([_mm(a1_ref[...], sv[h], precision=HI) + _mm(a2_ref[...], sv[PEER_HEADS + h], precision=HI)
                          for h in hs])
        return jnp.where(crow < _NCAND, cand, -jnp.inf)

    for lg in range(tt // 128):
        ls = slice(lg * 128, (lg + 1) * 128)
        s_blk = s_all[:, :, ls]

        def stage1(rows, exact):
            sv, rank = _top16(s_blk[rows], exact)
            sv_scr[rows, :, ls] = sv
            rk_scr[rows, :, ls] = rank
            return _miscount(rank)

        def stage2(exact):
            _, rankc = _top16(candidates(sv_scr[:, :, ls]), exact)
            rc_scr[:, :, ls] = rankc
            return _miscount(rankc)

        for rows in (slice(0, PEER_HEADS), slice(PEER_HEADS, 2 * PEER_HEADS)):
            @pl.when(stage1(rows, False) > 0.0)
            def _():
                stage1(rows, True)

        @pl.when(stage2(False) > 0.0)
        def _():
            stage2(True)

        sv = sv_scr[:, :, ls]
        sv1, sv2 = sv[:PEER_HEADS], sv[PEER_HEADS:]
        rank1 = rk_scr[0:PEER_HEADS, :, ls]
        sel_all = jnp.where(rc_scr[:, :, ls] < float(PEER_TOPK), 1.0, 0.0)
        cand = candidates(sv)
        z = jnp.sum(sel_all * jnp.exp(jnp.where(crow < _NCAND, cand - cand[:, 0:1, :], 0.0)), axis=1, keepdims=True)
        selb = sel_all.astype(bf16)
        cnt = jnp.stack([_mm(a1t_ref[...], selb[h]) for h in hs])
        lc = jnp.zeros((PEER_HEADS, PEER_NKEYS, 128), f32)
        for k1 in range(PEER_TOPK):
            lc = jnp.where(rank1 == float(k1), cnt[:, k1:k1 + 1, :], lc)
        r2_ref[:, keys_rows, ls] = rk_scr[PEER_HEADS:2 * PEER_HEADS, :, ls]
        e2_ref[:, keys_rows, ls] = jnp.exp(s_blk[PEER_HEADS:] - sv2[:, 0:1, :])
        lc_ref[:, keys_rows, ls] = lc
        g1_ref[:, keys_rows, ls] = jnp.exp(s_blk[:PEER_HEADS] - sv1[:, 0:1, :]) * (0.5 / z)
        for ref in (r2_ref, e2_ref, lc_ref, g1_ref):
            ref[:, PEER_NKEYS:, ls] = pad


def peer_topk(x, gamma, sc, sh, mod_map, wq_t, keys, tt):
    t = x.shape[0]
    r = sc.shape[1]
    a1 = jnp.zeros((_NCAND_PAD, PEER_TOPK), f32).at[jnp.arange(_NCAND), jnp.array([a for a, _ in _CAND])].set(1.0)
    a2 = jnp.zeros((_NCAND_PAD, PEER_TOPK), f32).at[jnp.arange(_NCAND), jnp.array([b for _, b in _CAND])].set(1.0)
    full = lambda shape: pl.BlockSpec(shape, lambda i: (0,) * len(shape))
    tok = pl.BlockSpec((PEER_HEADS, PEER_NKP, tt), lambda i: (0, 0, i))
    tok_shape = jax.ShapeDtypeStruct((PEER_HEADS, PEER_NKP, t), f32)
    return pl.pallas_call(
        _topk_kernel,
        grid=(t // tt,),
        in_specs=[
            pl.BlockSpec((tt, D_MODEL), lambda i: (i, 0)),
            full((1, D_MODEL)),
            pl.BlockSpec((1, r, D_MODEL), lambda i: mod_map(i)),
            pl.BlockSpec((1, r, D_MODEL), lambda i: mod_map(i)),
            full((PEER_HEADS * PEER_DQ, D_MODEL)),
            full((2, PEER_NKEYS, PEER_DQ // 2)),
            full((_NCAND_PAD, PEER_TOPK)),
            full((_NCAND_PAD, PEER_TOPK)),
            full((PEER_TOPK, _NCAND_PAD)),
        ],
        out_specs=[pl.BlockSpec((D_MODEL, tt), lambda i: (0, i)), tok, tok, tok, tok],
        out_shape=[jax.ShapeDtypeStruct((D_MODEL, t), bf16), tok_shape, tok_shape, tok_shape, tok_shape],
        scratch_shapes=[
            pltpu.VMEM((2 * PEER_HEADS, PEER_TOPK, tt), f32),
            pltpu.VMEM((2 * PEER_HEADS, PEER_NKEYS, tt), f32),
            pltpu.VMEM((PEER_HEADS, _NCAND_PAD, tt), f32),
        ],
        compiler_params=_params(("arbitrary",)),
        name="peer_topk",
    )(x, gamma.reshape(1, D_MODEL), sc, sh, wq_t, keys, a1, a2, a1.T.astype(bf16))


def _peer_kernel(hmt_ref, u_ref, vt_ref, r2_ref, e2_ref, lc_ref, g1_ref, x_ref, g_ref, fg_ref, o_ref,
                 yt, act_a, act_b, ht_a, ht_b, *, nsub, final):
    j = pl.program_id(1)
    nj = pl.num_programs(1)
    na = PEER_SB // PEER_NKEYS
    tt = hmt_ref.shape[1]

    @pl.when(j == 0)
    def _():
        yt[...] = jnp.zeros(yt.shape, f32)

    def mm1(s, ht):
        ht[...] = _mm(u_ref[s], hmt_ref[...])

    def gate_act(s, ht, act):
        for a in range(na):
            n1 = (j * nsub + s) * na + a
            lrows = [lc_ref[h, pl.ds(n1, 1), :] for h in range(PEER_HEADS)]
            grows = [g1_ref[h, pl.ds(n1, 1), :] for h in range(PEER_HEADS)]
            for lg in range(tt // 128):
                ls = slice(lg * 128, (lg + 1) * 128)
                w = None
                for h in range(PEER_HEADS):
                    wh = jnp.where(r2_ref[h, 0:PEER_NKEYS, ls] < lrows[h][:, ls], e2_ref[h, 0:PEER_NKEYS, ls] * grows[h][:, ls], 0.0)
                    w = wh if w is None else w + wh
                hb = ht[a * PEER_NKEYS:(a + 1) * PEER_NKEYS, ls]
                act[a * PEER_NKEYS:(a + 1) * PEER_NKEYS, ls] = (hb * (1.0 + lax.erf(hb * (2.0 ** -0.5))) * w).astype(bf16)

    def mm2(s, act):
        yt[...] += _mm(vt_ref[s], act[...])

    act_b[...] = jnp.zeros(act_b.shape, bf16)
    mm1(0, ht_a)

    def body(i, carry):
        s = 2 * i
        mm1(s + 1, ht_b)
        gate_act(s, ht_a, act_a)
        mm2(jnp.maximum(s - 1, 0), act_b)
        mm1(jnp.minimum(s + 2, nsub - 1), ht_a)
        gate_act(s + 1, ht_b, act_b)
        mm2(s, act_a)
        return carry

    lax.fori_loop(0, nsub // 2, body, 0)
    mm2(nsub - 1, act_b)

    @pl.when(j == nj - 1)
    def _():
        xn = x_ref[...] + g_ref[0] * yt[...].T
        if final:
            xn = _rms(xn) * fg_ref[...]
        o_ref[...] = xn


def peer_dense(hmt, u4, vt4, layer, r2, e2, lc, g1, x, gate, mod_map, final_g, tt, eb):
    t = x.shape[0]
    r = gate.shape[1]
    nsub = eb // PEER_SB
    final = final_g is not None
    fg = (final_g if final else jnp.ones((D_MODEL,), f32)).reshape(1, D_MODEL)
    tok = pl.BlockSpec((PEER_HEADS, PEER_NKP, tt), lambda i, j: (0, 0, i))
    return pl.pallas_call(
        functools.partial(_peer_kernel, nsub=nsub, final=final),
        grid=(t // tt, N_EXPERTS // eb),
        in_specs=[
            pl.BlockSpec((D_MODEL, tt), lambda i, j: (0, i)),
            pl.BlockSpec((None, nsub, PEER_SB, D_MODEL), lambda i, j: (layer, j, 0, 0)),
            pl.BlockSpec((None, nsub, D_MODEL, PEER_SB), lambda i, j: (layer, j, 0, 0)),
            tok, tok, tok, tok,
            pl.BlockSpec((tt, D_MODEL), lambda i, j: (i, 0)),
            pl.BlockSpec((1, r, D_MODEL), lambda i, j: mod_map(i)),
            pl.BlockSpec((1, D_MODEL), lambda i, j: (0, 0)),
        ],
        out_specs=pl.BlockSpec((tt, D_MODEL), lambda i, j: (i, 0)),
        out_shape=jax.ShapeDtypeStruct((t, D_MODEL), f32),
        scratch_shapes=[
            pltpu.VMEM((D_MODEL, tt), f32),
            pltpu.VMEM((PEER_SB, tt), bf16), pltpu.VMEM((PEER_SB, tt), bf16),
            pltpu.VMEM((PEER_SB, tt), f32), pltpu.VMEM((PEER_SB, tt), f32),
        ],
        compiler_params=_params(("arbitrary", "arbitrary")),
        name="peer_dense",
    )(hmt, u4, vt4, r2, e2, lc, g1, x, gate, fg)


def _prepare(ab_w_in, ab_conv_w, ab_conv_b, gdn_a_log, gdn_dt_bias, gdn_norm_g, ssm_a_log, ssm_dt_bias, ssm_d,
             ssm_norm_g, ab_w_out, ret_w_in, ret_norm_g, ret_w_out, peer_w_q, peer_keys, peer_u, peer_v):
    zeros8 = jnp.zeros((GDN_HEADS,), f32)
    nsb = N_EXPERTS // PEER_SB
    return dict(
        ab_w_main=ab_w_in[0][:, :AB_MAIN].astype(bf16),
        ab_w_small_t=ab_w_in[0][:, AB_MAIN:].T.astype(bf16),
        conv_w=ab_conv_w[0],
        conv_b=ab_conv_b[0].reshape(1, CONV_CH),
        pcol=jnp.stack([jnp.concatenate([zeros8, gdn_dt_bias[0], ssm_dt_bias[0]]),
                        jnp.concatenate([zeros8, gdn_a_log[0], ssm_a_log[0]])], axis=1),
        gdn_norm_g=gdn_norm_g[0].reshape(1, GDN_DV),
        dskip=jnp.repeat(ssm_d[0], SSM_P).reshape(1, SSM_INNER),
        ssm_norm_g=ssm_norm_g[0].reshape(1, SSM_INNER),
        ab_w_out=ab_w_out[0].astype(bf16),
        ret_w_in=ret_w_in[0].astype(bf16),
        ret_norm_g=ret_norm_g[0],
        ret_w_out=ret_w_out[0].astype(bf16),
        wq_t=[peer_w_q[i].T.astype(bf16) for i in range(DEPTH)],
        keys=[peer_keys[i].astype(bf16) for i in range(DEPTH)],
        u=peer_u.astype(bf16).reshape(DEPTH, nsb, PEER_SB, D_MODEL),
        vt=peer_v.astype(bf16).reshape(DEPTH, nsb, PEER_SB, D_MODEL).transpose(0, 1, 3, 2),
    )


def _rope_tables(pos0, length):
    inv = ROPE_BASE ** (-jnp.arange(0, RET_DK, 2, dtype=f32) / RET_DK)
    ang = (pos0 + jnp.arange(length, dtype=f32))[:, None] * inv[None, :]
    return jnp.cos(ang), jnp.sin(ang)


def _trunk(x, mods, pos0, conv0, gdn0, ssm0, ret0, p, norm1_g, norm2_g, final_g):
    b, l, _ = x.shape
    t = b * l
    c = CHUNK if l >= CHUNK else 8
    xt = x.reshape(t, D_MODEL)
    if l > 1:
        tm = min(512, l)
        tp = min(256, l)
        tt = min(256, l)
        mod_arr = lambda m: m.reshape(b, 1, D_MODEL)
        mod_map = lambda rows: (lambda i: (i * rows // l, 0, 0))
    else:
        tm = t
        tp = t
        tt = t
        mod_arr = lambda m: m.reshape(1, t, D_MODEL)
        mod_map = lambda rows: (lambda i: (0, i, 0))
    convs = gdns = ssms = rets = None
    for layer in range(DEPTH):
        sh1, sc1, g1, sh2, sc2, g2 = (mod_arr(m) for m in jnp.split(mods[layer], 6, axis=-1))
        if layer == 0:
            proj, smt = norm_proj(xt, norm1_g[layer], sc1, sh1, mod_map(tp), p["ab_w_main"], p["ab_w_small_t"],
                                  tp, AB_MAIN)
            tl = min(c, l)
            smt = smt.reshape(AB_SMALL, b, l // tl, tl).transpose(1, 2, 0, 3)
            o, convs, gdns, ssms = mixer_ab(proj.reshape(b, l, -1), smt, conv0, gdn0, jnp.swapaxes(ssm0, -1, -2),
                                            p["conv_w"], p["conv_b"],
                                            p["pcol"], p["gdn_norm_g"], p["dskip"], p["ssm_norm_g"], c)
            xt = out_proj(o.reshape(t, -1), p["ab_w_out"], xt, g1, mod_map(tm), tm)
        else:
            proj = norm_proj(xt, norm1_g[layer], sc1, sh1, mod_map(tp), p["ret_w_in"], None, tp, RET_IN)
            cos, sin = _rope_tables(pos0, l)
            o, rets = mixer_ret(proj.reshape(b, l, -1), cos, sin, ret0, p["ret_norm_g"], c)
            xt = out_proj(o.reshape(t, -1), p["ret_w_out"], xt, g1, mod_map(tm), tm)
        hmt, r2, e2, lc, gg = peer_topk(xt, norm2_g[layer], sc2, sh2, mod_map(tt), p["wq_t"][layer], p["keys"][layer], tt)
        xt = peer_dense(hmt, p["u"], p["vt"], layer, r2, e2, lc, gg, xt, g2, mod_map(tt),
                        final_g if layer == DEPTH - 1 else None, tt, PEER_EB)
    return xt.reshape(b, l, D_MODEL), convs[None], gdns[None], ssms[None], rets[None]


def kernel(x_prompt, x_sample, c_prompt, c_sample, state_conv, state_gdn, state_ssm, state_ret, ada_w, ada_b,
           norm1_g, norm2_g, ab_w_in, ab_conv_w, ab_conv_b, gdn_a_log, gdn_dt_bias, gdn_norm_g, ssm_a_log,
           ssm_dt_bias, ssm_d, ssm_norm_g, ab_w_out, ret_w_in, ret_norm_g, ret_w_out, peer_w_q, peer_keys,
           peer_u, peer_v, final_g):
    p = _prepare(ab_w_in, ab_conv_w, ab_conv_b, gdn_a_log, gdn_dt_bias, gdn_norm_g, ssm_a_log, ssm_dt_bias, ssm_d,
                 ssm_norm_g, ab_w_out, ret_w_in, ret_norm_g, ret_w_out, peer_w_q, peer_keys, peer_u, peer_v)
    nb = x_prompt.shape[0]
    mods = ada_mod(jnp.concatenate([c_prompt, c_sample], axis=0), ada_w, ada_b)
    zeros = lambda s: jnp.zeros((nb,) + s.shape[2:], s.dtype)
    y_p, p_conv, p_gdn, p_ssm, p_ret = _trunk(
        x_prompt, mods[:, :nb], 0, zeros(state_conv), zeros(state_gdn), zeros(state_ssm), zeros(state_ret),
        p, norm1_g, norm2_g, final_g)
    y_s, s_conv, s_gdn, s_ssm, s_ret = _trunk(
        x_sample, mods[:, nb:], PAST_LEN, state_conv[0], state_gdn[0], state_ssm[0], state_ret[0],
        p, norm1_g, norm2_g, final_g)
    return (y_p, y_s, p_conv, p_gdn, p_ssm, p_ret, s_conv, s_gdn, s_ssm, s_ret)
```

```python
import functools
import math

import jax
import jax.numpy as jnp
from jax import lax
from jax.experimental import pallas as pl
from jax.experimental.pallas import tpu as pltpu

f32 = jnp.float32
bf16 = jnp.bfloat16
HI = lax.Precision.HIGHEST

D_MODEL = 1024
DEPTH = 2
PAST_LEN = 16384
GDN_HEADS = 8
GDN_DK = 128
GDN_DV = 128
SSM_HEADS = 16
SSM_P = 64
SSM_N = 128
SSM_G = 2
CONV_W = 4
RET_HEADS = 4
RET_DK = 256
RET_DV = 512
ROPE_BASE = 10000.0
PEER_HEADS = 8
PEER_NKEYS = 128
PEER_TOPK = 16
PEER_DQ = 256
CHUNK = 64
EPS = 1e-6

GDN_QK = GDN_HEADS * GDN_DK
GDN_V = GDN_HEADS * GDN_DV
SSM_INNER = SSM_HEADS * SSM_P
SSM_BC = SSM_G * SSM_N
CONV_CH = 2 * GDN_QK + GDN_V + SSM_INNER + 2 * SSM_BC
AB_MAIN = CONV_CH + GDN_V + SSM_INNER
AB_SMALL = 2 * GDN_HEADS + SSM_HEADS
RET_QK = RET_HEADS * RET_DK
RET_V = RET_HEADS * RET_DV
RET_IN = 2 * RET_QK + 2 * RET_V
N_EXPERTS = PEER_NKEYS * PEER_NKEYS

VMEM_LIMIT = 56 * 1024 * 1024
PEER_SB = 256
PEER_EB = 4096
PEER_NKP = PEER_NKEYS + 8

_CAND = [(a, b) for a in range(PEER_TOPK) for b in range(PEER_TOPK) if (a + 1) * (b + 1) <= PEER_TOPK]
_NCAND = len(_CAND)
_NCAND_PAD = -(-_NCAND // 16) * 16


def _params(sem):
    return pltpu.CompilerParams(dimension_semantics=sem, vmem_limit_bytes=VMEM_LIMIT)


def _nt(a, b, **kw):
    return lax.dot_general(a, b, (((1,), (1,)), ((), ())), preferred_element_type=f32, **kw)


def _tn(a, b, **kw):
    return lax.dot_general(a, b, (((0,), (0,)), ((), ())), preferred_element_type=f32, **kw)


def _mm(a, b, **kw):
    return jnp.dot(a, b, preferred_element_type=f32, **kw)


def _bmm(a, b, **kw):
    return jnp.einsum('hik,hkj->hij', a, b, preferred_element_type=f32, **kw)


def _bnt(a, b, **kw):
    return jnp.einsum('hik,hjk->hij', a, b, preferred_element_type=f32, **kw)


def _btn(a, b, **kw):
    return jnp.einsum('hki,hkj->hij', a, b, preferred_element_type=f32, **kw)


def _split(x):
    hi = x.astype(bf16)
    return hi, (x - hi.astype(f32)).astype(bf16)


def _bmm3(a, b):
    ah, al = _split(a)
    bh, bl = _split(b)
    return _bmm(ah, bh) + (_bmm(ah, bl) + _bmm(al, bh))


def _silu(x):
    return x * jax.nn.sigmoid(x)


def _rms(x):
    return x * lax.rsqrt(jnp.mean(x * x, axis=-1, keepdims=True) + EPS)


def _mod_kernel(c_ref, w_ref, b_ref, o_ref):
    a = _silu(c_ref[...]).astype(bf16)
    o_ref[0] = _mm(a, w_ref[0].astype(bf16)) + b_ref[0]


def ada_mod(c_all, ada_w, ada_b):
    m = c_all.shape[0]
    tn = 768
    n = ada_w.shape[-1]
    return pl.pallas_call(
        _mod_kernel,
        grid=(DEPTH, n // tn),
        in_specs=[
            pl.BlockSpec((m, D_MODEL), lambda l, j: (0, 0)),
            pl.BlockSpec((1, D_MODEL, tn), lambda l, j: (l, 0, j)),
            pl.BlockSpec((1, 1, tn), lambda l, j: (l, 0, j)),
        ],
        out_specs=pl.BlockSpec((1, m, tn), lambda l, j: (l, 0, j)),
        out_shape=jax.ShapeDtypeStruct((DEPTH, m, n), f32),
        compiler_params=_params(("arbitrary", "arbitrary")),
        name="ada_mod",
    )(c_all, ada_w, ada_b.reshape(DEPTH, 1, n))


def _proj_kernel(x_ref, g_ref, sc_ref, sh_ref, w_ref, *rest, has_small):
    if has_small:
        ws_ref, o_ref, os_ref, hm_ref = rest
    else:
        o_ref, hm_ref = rest
    j = pl.program_id(1)

    @pl.when(j == 0)
    def _():
        y = _rms(x_ref[...]) * g_ref[...]
        hm = (y * (1.0 + sc_ref[0]) + sh_ref[0]).astype(bf16)
        hm_ref[...] = hm
        if has_small:
            os_ref[...] = _nt(ws_ref[...], hm)

    o_ref[...] = _mm(hm_ref[...], w_ref[...]).astype(o_ref.dtype)


def norm_proj(x, gamma, sc, sh, mod_map, w, ws_t, tm, tn):
    t = x.shape[0]
    n = w.shape[1]
    r = sc.shape[1]
    has_small = ws_t is not None
    in_specs = [
        pl.BlockSpec((tm, D_MODEL), lambda i, j: (i, 0)),
        pl.BlockSpec((1, D_MODEL), lambda i, j: (0, 0)),
        pl.BlockSpec((1, r, D_MODEL), lambda i, j: mod_map(i)),
        pl.BlockSpec((1, r, D_MODEL), lambda i, j: mod_map(i)),
        pl.BlockSpec((D_MODEL, tn), lambda i, j: (0, j)),
    ]
    out_specs = [pl.BlockSpec((tm, tn), lambda i, j: (i, j))]
    out_shape = [jax.ShapeDtypeStruct((t, n), bf16)]
    args = [x, gamma.reshape(1, D_MODEL), sc, sh, w]
    if has_small:
        s = ws_t.shape[0]
        in_specs.append(pl.BlockSpec((s, D_MODEL), lambda i, j: (0, 0)))
        out_specs.append(pl.BlockSpec((s, tm), lambda i, j: (0, i)))
        out_shape.append(jax.ShapeDtypeStruct((s, t), f32))
        args.append(ws_t)
    res = pl.pallas_call(
        functools.partial(_proj_kernel, has_small=has_small),
        grid=(t // tm, n // tn),
        in_specs=in_specs,
        out_specs=out_specs,
        out_shape=out_shape,
        scratch_shapes=[pltpu.VMEM((tm, D_MODEL), bf16)],
        compiler_params=_params(("arbitrary", "arbitrary")),
        name="norm_proj",
    )(*args)
    return res if has_small else res[0]


def _out_kernel(o_ref, w_ref, x_ref, g_ref, y_ref):
    y_ref[...] = x_ref[...] + g_ref[0] * _mm(o_ref[...], w_ref[...])


def out_proj(o, w, x, gate, mod_map, tm):
    t, k = o.shape
    r = gate.shape[1]
    return pl.pallas_call(
        _out_kernel,
        grid=(t // tm,),
        in_specs=[
            pl.BlockSpec((tm, k), lambda i: (i, 0)),
            pl.BlockSpec((k, D_MODEL), lambda i: (0, 0)),
            pl.BlockSpec((tm, D_MODEL), lambda i: (i, 0)),
            pl.BlockSpec((1, r, D_MODEL), lambda i: mod_map(i)),
        ],
        out_specs=pl.BlockSpec((tm, D_MODEL), lambda i: (i, 0)),
        out_shape=jax.ShapeDtypeStruct((t, D_MODEL), f32),
        compiler_params=_params(("arbitrary",)),
        name="out_proj",
    )(o, w, x, gate)


def _unit_lower_inverse(lm, c):
    eye = (lax.broadcasted_iota(jnp.int32, (c, c), 0) == lax.broadcasted_iota(jnp.int32, (c, c), 1)).astype(f32)
    p = eye - lm
    lp = lm
    k = 2
    while k < c:
        lp = _bmm3(lp, lp)
        p = p + _bmm3(p, lp)
        k *= 2
    return p


def _ab_kernel(proj_ref, smt_ref, conv0_ref, gdn0_ref, ssm0_ref, convw_ref, convb_ref, pcol_ref, gng_ref,
               dskip_ref, sng_ref, o_ref, convn_ref, gdnn_ref, ssmn_ref, xbuf, act, sg, ss, ob, *, c, lv):
    l = pl.program_id(1)
    nl = pl.num_programs(1)

    @pl.when(l == 0)
    def _():
        xbuf[...] = jnp.zeros(xbuf.shape, f32)
        xbuf[5:8, :] = conv0_ref[0]
        sg[...] = gdn0_ref[0]
        for h in range(SSM_HEADS):
            ss[h] = ssm0_ref[0, h].T

    xbuf[8:8 + lv, :] = proj_ref[0, :, 0:CONV_CH].astype(f32)
    y = convb_ref[...]
    for i in range(CONV_W):
        y = y + convw_ref[i:i + 1, :] * xbuf[5 + i:5 + i + c, :]
    act[...] = _silu(y)
    tail = xbuf[5 + lv:8 + lv, :]
    xbuf[5:8, :] = tail

    if lv == c:
        sm = smt_ref[0, 0]
    else:
        lane = lax.broadcasted_iota(jnp.int32, (AB_SMALL, c), 1)
        sm = jnp.where(lane < lv, jnp.broadcast_to(smt_ref[0, 0], (AB_SMALL, c)), 0.0)
    valid = lax.broadcasted_iota(jnp.int32, (AB_SMALL, c), 1) < lv
    bias_col = pcol_ref[:, 0:1]
    alog_col = pcol_ref[:, 1:2]
    beta_t = jnp.where(valid, jax.nn.sigmoid(sm), 0.0)
    sp_t = jnp.where(valid, jax.nn.softplus(sm + bias_col), 0.0)
    la_t = -jnp.exp(alog_col) * sp_t
    rr = lax.broadcasted_iota(jnp.int32, (c, c), 0)
    qq = lax.broadcasted_iota(jnp.int32, (c, c), 1)
    triu = (rr <= qq).astype(f32)
    eye = (rr == qq).astype(f32)
    cum_t = _mm(la_t, triu, precision=HI)
    last_t = cum_t[:, c - 1:c]
    ecum_t = jnp.exp(cum_t)
    elc_t = jnp.exp(last_t - cum_t)
    elast_t = jnp.exp(last_t)
    g0, g1, s0, s1 = 0, GDN_HEADS, 2 * GDN_HEADS, AB_SMALL
    rows = jnp.concatenate([
        beta_t[g0:g1],
        cum_t[g1:s0],
        ecum_t[g1:s0],
        beta_t[g0:g1] * ecum_t[g1:s0],
        elc_t[g1:s0],
        cum_t[s0:s1],
        ecum_t[s0:s1],
        sp_t[s0:s1] * elc_t[s0:s1],
        jnp.zeros((128 - 88, c), f32),
    ], axis=0)
    cols = _nt(eye, rows, precision=HI)
    incl = rr >= qq
    strict = rr > qq

    def colstack(base, n):
        return jnp.stack([cols[:, base + h:base + h + 1] for h in range(n)])

    def rowstack(x, base, n):
        return jnp.stack([x[base + h:base + h + 1, :] for h in range(n)])

    hs = range(GDN_HEADS)
    q = jnp.stack([act[:, h * GDN_DK:(h + 1) * GDN_DK] for h in hs])
    k = jnp.stack([act[:, GDN_QK + h * GDN_DK:GDN_QK + (h + 1) * GDN_DK] for h in hs])
    v = jnp.stack([act[:, 2 * GDN_QK + h * GDN_DV:2 * GDN_QK + (h + 1) * GDN_DV] for h in hs])
    q = q * lax.rsqrt(jnp.sum(q * q, axis=-1, keepdims=True) + EPS) * (GDN_DK ** -0.5)
    k = k * lax.rsqrt(jnp.sum(k * k, axis=-1, keepdims=True) + EPS)
    beta_c, cum_c, ecum_c, becum_c, elc_c = (colstack(b, GDN_HEADS) for b in (0, 8, 16, 24, 32))
    dec = jnp.exp(jnp.where(incl, cum_c - rowstack(cum_t, g1, GDN_HEADS), -jnp.inf))
    kb = k.astype(bf16)
    qb = q.astype(bf16)
    lm = jnp.where(strict, beta_c * _bnt(kb, kb) * dec, 0.0)
    pinv = _unit_lower_inverse(lm, c)
    rhs = jnp.concatenate([v * beta_c, k * becum_c], axis=2)
    sol = _bmm3(pinv, rhs)
    u0 = sol[:, :, :GDN_DV]
    w = sol[:, :, GDN_DV:]
    qk = _bnt(qb, kb) * dec
    s_old = sg[...]
    sb = s_old.astype(bf16)
    u = u0 - _bmm(w.astype(bf16), sb)
    ub = u.astype(bf16)
    o = _bmm((q * ecum_c).astype(bf16), sb) + _bmm(qk.astype(bf16), ub)
    kd = (k * elc_c).astype(bf16)
    sg[...] = s_old * rowstack(elast_t, g1, GDN_HEADS) + _btn(kd, ub)
    oa = _rms(o) * gng_ref[...]
    for h in hs:
        gate = proj_ref[0, :, CONV_CH + h * GDN_DV:CONV_CH + (h + 1) * GDN_DV].astype(f32)
        o_ref[0, :, h * GDN_DV:(h + 1) * GDN_DV] = (oa[h][0:lv] * _silu(gate)).astype(bf16)

    rep = SSM_HEADS // SSM_G
    xs0 = 2 * GDN_QK + GDN_V
    bm0 = xs0 + SSM_INNER
    cm0 = bm0 + SSM_BC
    hs = range(SSM_HEADS)
    bmb = [act[:, bm0 + g * SSM_N:bm0 + (g + 1) * SSM_N].astype(bf16) for g in range(SSM_G)]
    cmb = [act[:, cm0 + g * SSM_N:cm0 + (g + 1) * SSM_N].astype(bf16) for g in range(SSM_G)]
    cbg = [_nt(cmb[g], bmb[g]) for g in range(SSM_G)]
    cb = jnp.stack([cbg[h // rep] for h in hs])
    bm16 = jnp.stack([bmb[h // rep] for h in hs])
    cm16 = jnp.stack([cmb[h // rep] for h in hs])
    xs = jnp.stack([act[:, xs0 + h * SSM_P:xs0 + (h + 1) * SSM_P] for h in hs])
    dsk = jnp.stack([dskip_ref[:, h * SSM_P:(h + 1) * SSM_P] for h in hs])
    cum_c, ecum_c, dtelc_c = (colstack(b, SSM_HEADS) for b in (40, 56, 72))
    m = cb * jnp.exp(jnp.where(incl, cum_c - rowstack(cum_t, s0, SSM_HEADS), -jnp.inf)) * rowstack(sp_t, s0, SSM_HEADS)
    s_old = ss[...]
    o = _bmm(m.astype(bf16), xs.astype(bf16)) + _bmm(cm16, s_old.astype(bf16)) * ecum_c + dsk * xs
    ss[...] = s_old * rowstack(elast_t, s0, SSM_HEADS) + _btn(bm16, (xs * dtelc_c).astype(bf16))
    for h in hs:
        ob[:, h * SSM_P:(h + 1) * SSM_P] = o[h]
    z = proj_ref[0, :, CONV_CH + GDN_V:CONV_CH + GDN_V + SSM_INNER].astype(f32)
    obv = _rms(ob[0:lv, :] * _silu(z)) * sng_ref[...]
    o_ref[0, :, GDN_V:GDN_V + SSM_INNER] = obv.astype(bf16)

    @pl.when(l == nl - 1)
    def _():
        convn_ref[0] = xbuf[5:8, :]
        gdnn_ref[0] = sg[...]
        ssmn_ref[0] = ss[...]


def mixer_ab(proj, smt, conv0, gdn0, ssm0, conv_w, conv_b, pcol, gdn_norm_g, dskip, ssm_norm_g, c):
    b, l, npad = proj.shape
    tl = min(c, l)
    nl = l // tl
    kern = functools.partial(_ab_kernel, c=c, lv=tl)
    full = lambda shape: pl.BlockSpec(shape, lambda i, j: (0,) * len(shape))
    return pl.pallas_call(
        kern,
        grid=(b, nl),
        in_specs=[
            pl.BlockSpec((1, tl, npad), lambda i, j: (i, j, 0)),
            pl.BlockSpec((1, 1, AB_SMALL, tl), lambda i, j: (i, j, 0, 0)),
            pl.BlockSpec((1, CONV_W - 1, CONV_CH), lambda i, j: (i, 0, 0)),
            pl.BlockSpec((1, GDN_HEADS, GDN_DK, GDN_DV), lambda i, j: (i, 0, 0, 0)),
            pl.BlockSpec((1, SSM_HEADS, SSM_P, SSM_N), lambda i, j: (i, 0, 0, 0)),
            full((CONV_W, CONV_CH)),
            full((1, CONV_CH)),
            full((AB_SMALL, 2)),
            full((1, GDN_DV)),
            full((1, SSM_INNER)),
            full((1, SSM_INNER)),
        ],
        out_specs=[
            pl.BlockSpec((1, tl, GDN_V + SSM_INNER), lambda i, j: (i, j, 0)),
            pl.BlockSpec((1, CONV_W - 1, CONV_CH), lambda i, j: (i, 0, 0)),
            pl.BlockSpec((1, GDN_HEADS, GDN_DK, GDN_DV), lambda i, j: (i, 0, 0, 0)),
            pl.BlockSpec((1, SSM_HEADS, SSM_N, SSM_P), lambda i, j: (i, 0, 0, 0)),
        ],
        out_shape=[
            jax.ShapeDtypeStruct((b, l, GDN_V + SSM_INNER), bf16),
            jax.ShapeDtypeStruct((b, CONV_W - 1, CONV_CH), f32),
            jax.ShapeDtypeStruct((b, GDN_HEADS, GDN_DK, GDN_DV), f32),
            jax.ShapeDtypeStruct((b, SSM_HEADS, SSM_N, SSM_P), f32),
        ],
        scratch_shapes=[
            pltpu.VMEM((8 + c, CONV_CH), f32),
            pltpu.VMEM((c, CONV_CH), f32),
            pltpu.VMEM((GDN_HEADS, GDN_DK, GDN_DV), f32),
            pltpu.VMEM((SSM_HEADS, SSM_N, SSM_P), f32),
            pltpu.VMEM((c, SSM_INNER), f32),
        ],
        compiler_params=_params(("arbitrary", "arbitrary")),
        name="mixer_ab",
    )(proj, smt, conv0, gdn0, ssm0, conv_w, conv_b, pcol, gdn_norm_g, dskip, ssm_norm_g)


def _ret_kernel(proj_ref, cos_ref, sin_ref, ret0_ref, ng_ref, o_ref, retn_ref, st, buf, *, c, lv):
    l = pl.program_id(1)
    nl = pl.num_programs(1)

    @pl.when(l == 0)
    def _():
        st[...] = ret0_ref[0]
        if lv < c:
            buf[...] = jnp.zeros(buf.shape, f32)

    if lv == c:
        src = proj_ref.at[0]
        cos = cos_ref[...]
        sin = sin_ref[...]
    else:
        buf[0:lv, :] = proj_ref[0].astype(f32)
        src = buf
        cos = jnp.broadcast_to(cos_ref[...], (c, RET_DK // 2))
        sin = jnp.broadcast_to(sin_ref[...], (c, RET_DK // 2))

    ri = lax.broadcasted_iota(jnp.int32, (c, c), 0)
    ci = lax.broadcasted_iota(jnp.int32, (c, c), 1)
    incl = ri >= ci
    cnt_r = jnp.minimum(ri + 1, lv).astype(f32)
    cnt_c = jnp.minimum(ci + 1, lv).astype(f32)
    cnt_col = jnp.minimum(lax.broadcasted_iota(jnp.int32, (c, 1), 0) + 1, lv).astype(f32)
    half = RET_DK // 2

    def rope(x):
        x1, x2 = x[:, :, :half], x[:, :, half:]
        return jnp.concatenate([x1 * cos - x2 * sin, x1 * sin + x2 * cos], axis=2)

    hs = range(RET_HEADS)
    lgs = [math.log(1.0 - 2.0 ** (-5.0 - h)) for h in hs]
    q = rope(jnp.stack([src[:, h * RET_DK:(h + 1) * RET_DK].astype(f32) for h in hs]))
    k = rope(jnp.stack([src[:, RET_QK + h * RET_DK:RET_QK + (h + 1) * RET_DK].astype(f32) for h in hs])) * (RET_DK ** -0.5)
    v = jnp.stack([src[:, 2 * RET_QK + h * RET_DV:2 * RET_QK + (h + 1) * RET_DV].astype(bf16) for h in hs])
    dec = jnp.stack([jnp.exp(jnp.where(incl, (cnt_r - cnt_c) * lg, -jnp.inf)) for lg in lgs])
    ecum = jnp.stack([jnp.exp(cnt_col * lg) for lg in lgs])
    elc = jnp.stack([jnp.exp((lv - cnt_col) * lg) for lg in lgs])
    elast = jnp.stack([jnp.full((1, 1), math.exp(lv * lg), f32) for lg in lgs])
    scores = _bnt(q.astype(bf16), k.astype(bf16)) * dec
    s_old = st[...]
    o = _bmm(scores.astype(bf16), v) + _bmm((q * ecum).astype(bf16), s_old.astype(bf16))
    st[...] = s_old * elast + _btn((k * elc).astype(bf16), v)
    mu = jnp.mean(o, axis=-1, keepdims=True)
    var = jnp.mean(jnp.square(o - mu), axis=-1, keepdims=True)
    o = (o - mu) * lax.rsqrt(var + EPS)
    for h in hs:
        gate = src[:, 2 * RET_QK + RET_V + h * RET_DV:2 * RET_QK + RET_V + (h + 1) * RET_DV].astype(f32)
        o_ref[0, :, h * RET_DV:(h + 1) * RET_DV] = (_silu(gate) * (o[h] * ng_ref[h:h + 1, :]))[0:lv].astype(bf16)

    @pl.when(l == nl - 1)
    def _():
        retn_ref[0] = st[...]


def mixer_ret(proj, cos, sin, ret0, norm_g, c):
    b, l, n = proj.shape
    tl = min(c, l)
    nl = l // tl
    kern = functools.partial(_ret_kernel, c=c, lv=tl)
    return pl.pallas_call(
        kern,
        grid=(b, nl),
        in_specs=[
            pl.BlockSpec((1, tl, n), lambda i, j: (i, j, 0)),
            pl.BlockSpec((tl, RET_DK // 2), lambda i, j: (j, 0)),
            pl.BlockSpec((tl, RET_DK // 2), lambda i, j: (j, 0)),
            pl.BlockSpec((1, RET_HEADS, RET_DK, RET_DV), lambda i, j: (i, 0, 0, 0)),
            pl.BlockSpec((RET_HEADS, RET_DV), lambda i, j: (0, 0)),
        ],
        out_specs=[
            pl.BlockSpec((1, tl, RET_V), lambda i, j: (i, j, 0)),
            pl.BlockSpec((1, RET_HEADS, RET_DK, RET_DV), lambda i, j: (i, 0, 0, 0)),
        ],
        out_shape=[
            jax.ShapeDtypeStruct((b, l, RET_V), bf16),
            jax.ShapeDtypeStruct((b, RET_HEADS, RET_DK, RET_DV), f32),
        ],
        scratch_shapes=[
            pltpu.VMEM((RET_HEADS, RET_DK, RET_DV), f32),
            pltpu.VMEM((c, n), f32),
        ],
        compiler_params=_params(("arbitrary", "arbitrary")),
        name="mixer_ret",
    )(proj, cos, sin, ret0, norm_g)


def _top16(s, exact):
    g, n, t = s.shape
    rows = lax.broadcasted_iota(jnp.int32, (g, n, t), 1)
    krow = lax.broadcasted_iota(jnp.int32, (g, PEER_TOPK, t), 1)
    rank = jnp.full((g, n, t), float(PEER_TOPK), f32)
    sv = jnp.zeros((g, PEER_TOPK, t), f32)
    work = s
    for k in range(PEER_TOPK):
        m = jnp.max(work, axis=1, keepdims=True)
        if exact:
            idx = jnp.min(jnp.where(work == m, rows, n), axis=1, keepdims=True)
            sel = rows == idx
        else:
            sel = work == m
        rank = jnp.where(sel, float(k), rank)
        work = jnp.where(sel, -jnp.inf, work)
        sv = jnp.where(krow == k, m, sv)
    return sv, rank


def _miscount(rank):
    cnt = jnp.sum(jnp.where(rank < float(PEER_TOPK), 1.0, 0.0), axis=1, keepdims=True)
    return jnp.max(jnp.abs(cnt - float(PEER_TOPK)))


def _topk_kernel(x_ref, g_ref, sc_ref, sh_ref, wqt_ref, keys_ref, a1_ref, a2_ref, a1t_ref,
                 hmt_ref, r2_ref, e2_ref, lc_ref, g1_ref, sv_scr, rk_scr, rc_scr):
    y = _rms(x_ref[...]) * g_ref[...]
    hm = y * (1.0 + sc_ref[0]) + sh_ref[0]
    hmt = hm.T.astype(bf16)
    hmt_ref[...] = hmt
    qt = _mm(wqt_ref[...], hmt)
    tt = qt.shape[1]
    half = PEER_DQ // 2
    hs = range(PEER_HEADS)
    s_all = jnp.stack(
        [_mm(keys_ref[0], qt[h * PEER_DQ:h * PEER_DQ + half, :].astype(bf16)) for h in hs]
        + [_mm(keys_ref[1], qt[h * PEER_DQ + half:(h + 1) * PEER_DQ, :].astype(bf16)) for h in hs])
    keys_rows = slice(0, PEER_NKEYS)
    pad = jnp.zeros((PEER_HEADS, PEER_NKP - PEER_NKEYS, 128), f32)
    crow = lax.broadcasted_iota(jnp.int32, (PEER_HEADS, _NCAND_PAD, 128), 1)

    def candidates(sv):
        cand = jnp.stack([_mm(a1_ref[...], sv[h], precision=HI) + _mm(a2_ref[...], sv[PEER_HEADS + h], precision=HI)
                          for h in hs])
        return jnp.where(crow < _NCAND, cand, -jnp.inf)

    for lg in range(tt // 128):
        ls = slice(lg * 128, (lg + 1) * 128)
        s_blk = s_all[:, :, ls]

        def stage1(rows, exact):
            sv, rank = _top16(s_blk[rows], exact)
            sv_scr[rows, :, ls] = sv
            rk_scr[rows, :, ls] = rank
            return _miscount(rank)

        def stage2(exact):
            _, rankc = _top16(candidates(sv_scr[:, :, ls]), exact)
            rc_scr[:, :, ls] = rankc
            return _miscount(rankc)

        for rows in (slice(0, PEER_HEADS), slice(PEER_HEADS, 2 * PEER_HEADS)):
            @pl.when(stage1(rows, False) > 0.0)
            def _():
                stage1(rows, True)

        @pl.when(stage2(False) > 0.0)
        def _():
            stage2(True)

        sv = sv_scr[:, :, ls]
        sv1, sv2 = sv[:PEER_HEADS], sv[PEER_HEADS:]
        rank1 = rk_scr[0:PEER_HEADS, :, ls]
        sel_all = jnp.where(rc_scr[:, :, ls] < float(PEER_TOPK), 1.0, 0.0)
        cand = candidates(sv)
        z = jnp.sum(sel_all * jnp.exp(jnp.where(crow < _NCAND, cand - cand[:, 0:1, :], 0.0)), axis=1, keepdims=True)
        selb = sel_all.astype(bf16)
        cnt = jnp.stack([_mm(a1t_ref[...], selb[h]) for h in hs])
        lc = jnp.zeros((PEER_HEADS, PEER_NKEYS, 128), f32)
        for k1 in range(PEER_TOPK):
            lc = jnp.where(rank1 == float(k1), cnt[:, k1:k1 + 1, :], lc)
        r2_ref[:, keys_rows, ls] = rk_scr[PEER_HEADS:2 * PEER_HEADS, :, ls]
        e2_ref[:, keys_rows, ls] = jnp.exp(s_blk[PEER_HEADS:] - sv2[:, 0:1, :])
        lc_ref[:, keys_rows, ls] = lc
        g1_ref[:, keys_rows, ls] = jnp.exp(s_blk[:PEER_HEADS] - sv1[:, 0:1, :]) * (0.5 / z)
        for ref in (r2_ref, e2_ref, lc_ref, g1_ref):
            ref[:, PEER_NKEYS:, ls] = pad


def peer_topk(x, gamma, sc, sh, mod_map, wq_t, keys, tt):
    t = x.shape[0]
    r = sc.shape[1]
    a1 = jnp.zeros((_NCAND_PAD, PEER_TOPK), f32).at[jnp.arange(_NCAND), jnp.array([a for a, _ in _CAND])].set(1.0)
    a2 = jnp.zeros((_NCAND_PAD, PEER_TOPK), f32).at[jnp.arange(_NCAND), jnp.array([b for _, b in _CAND])].set(1.0)
    full = lambda shape: pl.BlockSpec(shape, lambda i: (0,) * len(shape))
    tok = pl.BlockSpec((PEER_HEADS, PEER_NKP, tt), lambda i: (0, 0, i))
    tok_shape = jax.ShapeDtypeStruct((PEER_HEADS, PEER_NKP, t), f32)
    return pl.pallas_call(
        _topk_kernel,
        grid=(t // tt,),
        in_specs=[
            pl.BlockSpec((tt, D_MODEL), lambda i: (i, 0)),
            full((1, D_MODEL)),
            pl.BlockSpec((1, r, D_MODEL), lambda i: mod_map(i)),
            pl.BlockSpec((1, r, D_MODEL), lambda i: mod_map(i)),
            full((PEER_HEADS * PEER_DQ, D_MODEL)),
            full((2, PEER_NKEYS, PEER_DQ // 2)),
            full((_NCAND_PAD, PEER_TOPK)),
            full((_NCAND_PAD, PEER_TOPK)),
            full((PEER_TOPK, _NCAND_PAD)),
        ],
        out_specs=[pl.BlockSpec((D_MODEL, tt), lambda i: (0, i)), tok, tok, tok, tok],
        out_shape=[jax.ShapeDtypeStruct((D_MODEL, t), bf16), tok_shape, tok_shape, tok_shape, tok_shape],
        scratch_shapes=[
            pltpu.VMEM((2 * PEER_HEADS, PEER_TOPK, tt), f32),
            pltpu.VMEM((2 * PEER_HEADS, PEER_NKEYS, tt), f32),
            pltpu.VMEM((PEER_HEADS, _NCAND_PAD, tt), f32),
        ],
        compiler_params=_params(("arbitrary",)),
        name="peer_topk",
    )(x, gamma.reshape(1, D_MODEL), sc, sh, wq_t, keys, a1, a2, a1.T.astype(bf16))


def _peer_kernel(hmt_ref, u_ref, vt_ref, r2_ref, e2_ref, lc_ref, g1_ref, x_ref, g_ref, fg_ref, o_ref,
                 yt, act_a, act_b, ht_a, ht_b, *, nsub, final):
    j = pl.program_id(1)
    nj = pl.num_programs(1)
    na = PEER_SB // PEER_NKEYS
    tt = hmt_ref.shape[1]

    @pl.when(j == 0)
    def _():
        yt[...] = jnp.zeros(yt.shape, f32)

    def mm1(s, ht):
        ht[...] = _mm(u_ref[s], hmt_ref[...])

    def gate_act(s, ht, act):
        for a in range(na):
            n1 = (j * nsub + s) * na + a
            lrows = [lc_ref[h, pl.ds(n1, 1), :] for h in range(PEER_HEADS)]
            grows = [g1_ref[h, pl.ds(n1, 1), :] for h in range(PEER_HEADS)]
            for lg in range(tt // 128):
                ls = slice(lg * 128, (lg + 1) * 128)
                w = None
                for h in range(PEER_HEADS):
                    wh = jnp.where(r2_ref[h, 0:PEER_NKEYS, ls] < lrows[h][:, ls], e2_ref[h, 0:PEER_NKEYS, ls] * grows[h][:, ls], 0.0)
                    w = wh if w is None else w + wh
                hb = ht[a * PEER_NKEYS:(a + 1) * PEER_NKEYS, ls]
                act[a * PEER_NKEYS:(a + 1) * PEER_NKEYS, ls] = (hb * (1.0 + lax.erf(hb * (2.0 ** -0.5))) * w).astype(bf16)

    def mm2(s, act):
        yt[...] += _mm(vt_ref[s], act[...])

    act_b[...] = jnp.zeros(act_b.shape, bf16)
    mm1(0, ht_a)

    def pair(s, last):
        mm1(s + 1, ht_b)
        gate_act(s, ht_a, act_a)
        mm2(jnp.maximum(s - 1, 0), act_b)
        if not last:
            mm1(s + 2, ht_a)
        gate_act(s + 1, ht_b, act_b)
        mm2(s, act_a)

    def body(i, carry):
        pair(2 * i, False)
        return carry

    lax.fori_loop(0, nsub // 2 - 1, body, 0)
    pair(nsub - 2, True)
    mm2(nsub - 1, act_b)

    @pl.when(j == nj - 1)
    def _():
        xn = x_ref[...] + g_ref[0] * yt[...].T
        if final:
            xn = _rms(xn) * fg_ref[...]
        o_ref[...] = xn


def peer_dense(hmt, u4, vt4, layer, r2, e2, lc, g1, x, gate, mod_map, final_g, tt, eb):
    t = x.shape[0]
    r = gate.shape[1]
    nsub = eb // PEER_SB
    final = final_g is not None
    fg = (final_g if final else jnp.ones((D_MODEL,), f32)).reshape(1, D_MODEL)
    tok = pl.BlockSpec((PEER_HEADS, PEER_NKP, tt), lambda i, j: (0, 0, i))
    return pl.pallas_call(
        functools.partial(_peer_kernel, nsub=nsub, final=final),
        grid=(t // tt, N_EXPERTS // eb),
        in_specs=[
            pl.BlockSpec((D_MODEL, tt), lambda i, j: (0, i)),
            pl.BlockSpec((None, nsub, PEER_SB, D_MODEL), lambda i, j: (layer, j, 0, 0)),
            pl.BlockSpec((None, nsub, D_MODEL, PEER_SB), lambda i, j: (layer, j, 0, 0)),
            tok, tok, tok, tok,
            pl.BlockSpec((tt, D_MODEL), lambda i, j: (i, 0)),
            pl.BlockSpec((1, r, D_MODEL), lambda i, j: mod_map(i)),
            pl.BlockSpec((1, D_MODEL), lambda i, j: (0, 0)),
        ],
        out_specs=pl.BlockSpec((tt, D_MODEL), lambda i, j: (i, 0)),
        out_shape=jax.ShapeDtypeStruct((t, D_MODEL), f32),
        scratch_shapes=[
            pltpu.VMEM((D_MODEL, tt), f32),
            pltpu.VMEM((PEER_SB, tt), bf16), pltpu.VMEM((PEER_SB, tt), bf16),
            pltpu.VMEM((PEER_SB, tt), f32), pltpu.VMEM((PEER_SB, tt), f32),
        ],
        compiler_params=_params(("arbitrary", "arbitrary")),
        name="peer_dense",
    )(hmt, u4, vt4, r2, e2, lc, g1, x, gate, fg)


def _prepare(ab_w_in, ab_conv_w, ab_conv_b, gdn_a_log, gdn_dt_bias, gdn_norm_g, ssm_a_log, ssm_dt_bias, ssm_d,
             ssm_norm_g, ab_w_out, ret_w_in, ret_norm_g, ret_w_out, peer_w_q, peer_keys, peer_u, peer_v):
    zeros8 = jnp.zeros((GDN_HEADS,), f32)
    nsb = N_EXPERTS // PEER_SB
    return dict(
        ab_w_main=ab_w_in[0][:, :AB_MAIN].astype(bf16),
        ab_w_small_t=ab_w_in[0][:, AB_MAIN:].T.astype(bf16),
        conv_w=ab_conv_w[0],
        conv_b=ab_conv_b[0].reshape(1, CONV_CH),
        pcol=jnp.stack([jnp.concatenate([zeros8, gdn_dt_bias[0], ssm_dt_bias[0]]),
                        jnp.concatenate([zeros8, gdn_a_log[0], ssm_a_log[0]])], axis=1),
        gdn_norm_g=gdn_norm_g[0].reshape(1, GDN_DV),
        dskip=jnp.repeat(ssm_d[0], SSM_P).reshape(1, SSM_INNER),
        ssm_norm_g=ssm_norm_g[0].reshape(1, SSM_INNER),
        ab_w_out=ab_w_out[0].astype(bf16),
        ret_w_in=ret_w_in[0].astype(bf16),
        ret_norm_g=ret_norm_g[0],
        ret_w_out=ret_w_out[0].astype(bf16),
        wq_t=[peer_w_q[i].T.astype(bf16) for i in range(DEPTH)],
        keys=[peer_keys[i].astype(bf16) for i in range(DEPTH)],
        u=peer_u.astype(bf16).reshape(DEPTH, nsb, PEER_SB, D_MODEL),
        vt=peer_v.astype(bf16).reshape(DEPTH, nsb, PEER_SB, D_MODEL).transpose(0, 1, 3, 2),
    )


def _rope_tables(pos0, length):
    inv = ROPE_BASE ** (-jnp.arange(0, RET_DK, 2, dtype=f32) / RET_DK)
    ang = (pos0 + jnp.arange(length, dtype=f32))[:, None] * inv[None, :]
    return jnp.cos(ang), jnp.sin(ang)


def _trunk(x, mods, pos0, conv0, gdn0, ssm0, ret0, p, norm1_g, norm2_g, final_g):
    b, l, _ = x.shape
    t = b * l
    c = CHUNK if l >= CHUNK else 8
    xt = x.reshape(t, D_MODEL)
    if l > 1:
        tm = min(512, l)
        tp = min(256, l)
        tt = min(256, l)
        mod_arr = lambda m: m.reshape(b, 1, D_MODEL)
        mod_map = lambda rows: (lambda i: (i * rows // l, 0, 0))
    else:
        tm = t
        tp = t
        tt = t
        mod_arr = lambda m: m.reshape(1, t, D_MODEL)
        mod_map = lambda rows: (lambda i: (0, i, 0))
    convs = gdns = ssms = rets = None
    for layer in range(DEPTH):
        sh1, sc1, g1, sh2, sc2, g2 = (mod_arr(m) for m in jnp.split(mods[layer], 6, axis=-1))
        if layer == 0:
            proj, smt = norm_proj(xt, norm1_g[layer], sc1, sh1, mod_map(tp), p["ab_w_main"], p["ab_w_small_t"],
                                  tp, AB_MAIN)
            tl = min(c, l)
            smt = smt.reshape(AB_SMALL, b, l // tl, tl).transpose(1, 2, 0, 3)
            o, convs, gdns, ssms = mixer_ab(proj.reshape(b, l, -1), smt, conv0, gdn0, jnp.swapaxes(ssm0, -1, -2),
                                            p["conv_w"], p["conv_b"],
                                            p["pcol"], p["gdn_norm_g"], p["dskip"], p["ssm_norm_g"], c)
            xt = out_proj(o.reshape(t, -1), p["ab_w_out"], xt, g1, mod_map(tm), tm)
        else:
            proj = norm_proj(xt, norm1_g[layer], sc1, sh1, mod_map(tp), p["ret_w_in"], None, tp, RET_IN)
            cos, sin = _rope_tables(pos0, l)
            o, rets = mixer_ret(proj.reshape(b, l, -1), cos, sin, ret0, p["ret_norm_g"], c)
            xt = out_proj(o.reshape(t, -1), p["ret_w_out"], xt, g1, mod_map(tm), tm)
        hmt, r2, e2, lc, gg = peer_topk(xt, norm2_g[layer], sc2, sh2, mod_map(tt), p["wq_t"][layer], p["keys"][layer], tt)
        xt = peer_dense(hmt, p["u"], p["vt"], layer, r2, e2, lc, gg, xt, g2, mod_map(tt),
                        final_g if layer == DEPTH - 1 else None, tt, PEER_EB)
    return xt.reshape(b, l, D_MODEL), convs[None], gdns[None], ssms[None], rets[None]


def kernel(x_prompt, x_sample, c_prompt, c_sample, state_conv, state_gdn, state_ssm, state_ret, ada_w, ada_b,
           norm1_g, norm2_g, ab_w_in, ab_conv_w, ab_conv_b, gdn_a_log, gdn_dt_bias, gdn_norm_g, ssm_a_log,
           ssm_dt_bias, ssm_d, ssm_norm_g, ab_w_out, ret_w_in, ret_norm_g, ret_w_out, peer_w_q, peer_keys,
           peer_u, peer_v, final_g):
    p = _prepare(ab_w_in, ab_conv_w, ab_conv_b, gdn_a_log, gdn_dt_bias, gdn_norm_g, ssm_a_log, ssm_dt_bias, ssm_d,
                 ssm_norm_g, ab_w_out, ret_w_in, ret_norm_g, ret_w_out, peer_w_q, peer_keys, peer_u, peer_v)
    nb = x_prompt.shape[0]
    mods = ada_mod(jnp.concatenate([c_prompt, c_sample], axis=0), ada_w, ada_b)
    zeros = lambda s: jnp.zeros((nb,) + s.shape[2:], s.dtype)
    y_p, p_conv, p_gdn, p_ssm, p_ret = _trunk(
        x_prompt, mods[:, :nb], 0, zeros(state_conv), zeros(state_gdn), zeros(state_ssm), zeros(state_ret),
        p, norm1_g, norm2_g, final_g)
    y_s, s_conv, s_gdn, s_ssm, s_ret = _trunk(
        x_sample, mods[:, nb:], PAST_LEN, state_conv[0], state_gdn[0], state_ssm[0], state_ret[0],
        p, norm1_g, norm2_g, final_g)
    return (y_p, y_s, p_conv, p_gdn, p_ssm, p_ret, s_conv, s_gdn, s_ssm, s_ret)
```

```python
import functools
import math

import jax
import jax.numpy as jnp
from jax import lax
from jax.experimental import pallas as pl
from jax.experimental.pallas import tpu as pltpu

f32 = jnp.float32
bf16 = jnp.bfloat16
HI = lax.Precision.HIGHEST

D_MODEL = 1024
DEPTH = 2
PAST_LEN = 16384
GDN_HEADS = 8
GDN_DK = 128
GDN_DV = 128
SSM_HEADS = 16
SSM_P = 64
SSM_N = 128
SSM_G = 2
CONV_W = 4
RET_HEADS = 4
RET_DK = 256
RET_DV = 512
ROPE_BASE = 10000.0
PEER_HEADS = 8
PEER_NKEYS = 128
PEER_TOPK = 16
PEER_DQ = 256
CHUNK = 64
EPS = 1e-6

GDN_QK = GDN_HEADS * GDN_DK
GDN_V = GDN_HEADS * GDN_DV
SSM_INNER = SSM_HEADS * SSM_P
SSM_BC = SSM_G * SSM_N
CONV_CH = 2 * GDN_QK + GDN_V + SSM_INNER + 2 * SSM_BC
AB_MAIN = CONV_CH + GDN_V + SSM_INNER
AB_SMALL = 2 * GDN_HEADS + SSM_HEADS
RET_QK = RET_HEADS * RET_DK
RET_V = RET_HEADS * RET_DV
RET_IN = 2 * RET_QK + 2 * RET_V
N_EXPERTS = PEER_NKEYS * PEER_NKEYS

VMEM_LIMIT = 56 * 1024 * 1024
PEER_SB = 256
PEER_EB = 4096
PEER_NKP = PEER_NKEYS + 8

_CAND = [(a, b) for a in range(PEER_TOPK) for b in range(PEER_TOPK) if (a + 1) * (b + 1) <= PEER_TOPK]
_NCAND = len(_CAND)
_NCAND_PAD = -(-_NCAND // 16) * 16


def _params(sem):
    return pltpu.CompilerParams(dimension_semantics=sem, vmem_limit_bytes=VMEM_LIMIT)


def _nt(a, b, **kw):
    return lax.dot_general(a, b, (((1,), (1,)), ((), ())), preferred_element_type=f32, **kw)


def _tn(a, b, **kw):
    return lax.dot_general(a, b, (((0,), (0,)), ((), ())), preferred_element_type=f32, **kw)


def _mm(a, b, **kw):
    return jnp.dot(a, b, preferred_element_type=f32, **kw)


def _bmm(a, b, **kw):
    return jnp.einsum('hik,hkj->hij', a, b, preferred_element_type=f32, **kw)


def _bnt(a, b, **kw):
    return jnp.einsum('hik,hjk->hij', a, b, preferred_element_type=f32, **kw)


def _btn(a, b, **kw):
    return jnp.einsum('hki,hkj->hij', a, b, preferred_element_type=f32, **kw)


def _split(x):
    hi = x.astype(bf16)
    return hi, (x - hi.astype(f32)).astype(bf16)


def _bmm3(a, b):
    ah, al = _split(a)
    bh, bl = _split(b)
    return _bmm(ah, bh) + (_bmm(ah, bl) + _bmm(al, bh))


def _silu(x):
    return x * jax.nn.sigmoid(x)


def _rms(x):
    return x * lax.rsqrt(jnp.mean(x * x, axis=-1, keepdims=True) + EPS)


def _mod_kernel(c_ref, w_ref, b_ref, o_ref):
    a = _silu(c_ref[...]).astype(bf16)
    o_ref[0] = _mm(a, w_ref[0].astype(bf16)) + b_ref[0]


def ada_mod(c_all, ada_w, ada_b):
    m = c_all.shape[0]
    tn = 768
    n = ada_w.shape[-1]
    return pl.pallas_call(
        _mod_kernel,
        grid=(DEPTH, n // tn),
        in_specs=[
            pl.BlockSpec((m, D_MODEL), lambda l, j: (0, 0)),
            pl.BlockSpec((1, D_MODEL, tn), lambda l, j: (l, 0, j)),
            pl.BlockSpec((1, 1, tn), lambda l, j: (l, 0, j)),
        ],
        out_specs=pl.BlockSpec((1, m, tn), lambda l, j: (l, 0, j)),
        out_shape=jax.ShapeDtypeStruct((DEPTH, m, n), f32),
        compiler_params=_params(("arbitrary", "arbitrary")),
        name="ada_mod",
    )(c_all, ada_w, ada_b.reshape(DEPTH, 1, n))


def _proj_kernel(x_ref, g_ref, sc_ref, sh_ref, w_ref, *rest, has_small):
    if has_small:
        ws_ref, o_ref, os_ref, hm_ref = rest
    else:
        o_ref, hm_ref = rest
    j = pl.program_id(1)

    @pl.when(j == 0)
    def _():
        y = _rms(x_ref[...]) * g_ref[...]
        hm = (y * (1.0 + sc_ref[0]) + sh_ref[0]).astype(bf16)
        hm_ref[...] = hm
        if has_small:
            os_ref[...] = _nt(ws_ref[...], hm)

    o_ref[...] = _mm(hm_ref[...], w_ref[...]).astype(o_ref.dtype)


def norm_proj(x, gamma, sc, sh, mod_map, w, ws_t, tm, tn):
    t = x.shape[0]
    n = w.shape[1]
    r = sc.shape[1]
    has_small = ws_t is not None
    in_specs = [
        pl.BlockSpec((tm, D_MODEL), lambda i, j: (i, 0)),
        pl.BlockSpec((1, D_MODEL), lambda i, j: (0, 0)),
        pl.BlockSpec((1, r, D_MODEL), lambda i, j: mod_map(i)),
        pl.BlockSpec((1, r, D_MODEL), lambda i, j: mod_map(i)),
        pl.BlockSpec((D_MODEL, tn), lambda i, j: (0, j)),
    ]
    out_specs = [pl.BlockSpec((tm, tn), lambda i, j: (i, j))]
    out_shape = [jax.ShapeDtypeStruct((t, n), bf16)]
    args = [x, gamma.reshape(1, D_MODEL), sc, sh, w]
    if has_small:
        s = ws_t.shape[0]
        in_specs.append(pl.BlockSpec((s, D_MODEL), lambda i, j: (0, 0)))
        out_specs.append(pl.BlockSpec((s, tm), lambda i, j: (0, i)))
        out_shape.append(jax.ShapeDtypeStruct((s, t), f32))
        args.append(ws_t)
    res = pl.pallas_call(
        functools.partial(_proj_kernel, has_small=has_small),
        grid=(t // tm, n // tn),
        in_specs=in_specs,
        out_specs=out_specs,
        out_shape=out_shape,
        scratch_shapes=[pltpu.VMEM((tm, D_MODEL), bf16)],
        compiler_params=_params(("arbitrary", "arbitrary")),
        name="norm_proj",
    )(*args)
    return res if has_small else res[0]


def _out_kernel(o_ref, w_ref, x_ref, g_ref, y_ref):
    y_ref[...] = x_ref[...] + g_ref[0] * _mm(o_ref[...], w_ref[...])


def out_proj(o, w, x, gate, mod_map, tm):
    t, k = o.shape
    r = gate.shape[1]
    return pl.pallas_call(
        _out_kernel,
        grid=(t // tm,),
        in_specs=[
            pl.BlockSpec((tm, k), lambda i: (i, 0)),
            pl.BlockSpec((k, D_MODEL), lambda i: (0, 0)),
            pl.BlockSpec((tm, D_MODEL), lambda i: (i, 0)),
            pl.BlockSpec((1, r, D_MODEL), lambda i: mod_map(i)),
        ],
        out_specs=pl.BlockSpec((tm, D_MODEL), lambda i: (i, 0)),
        out_shape=jax.ShapeDtypeStruct((t, D_MODEL), f32),
        compiler_params=_params(("arbitrary",)),
        name="out_proj",
    )(o, w, x, gate)


def _unit_lower_inverse(lm, c):
    eye = (lax.broadcasted_iota(jnp.int32, (c, c), 0) == lax.broadcasted_iota(jnp.int32, (c, c), 1)).astype(f32)
    p = eye - lm
    lp = lm
    k = 2
    while k < c:
        lp = _bmm3(lp, lp)
        p = p + _bmm3(p, lp)
        k *= 2
    return p


def _ab_kernel(proj_ref, smt_ref, conv0_ref, gdn0_ref, ssm0_ref, convw_ref, convb_ref, pcol_ref, gng_ref,
               dskip_ref, sng_ref, o_ref, convn_ref, gdnn_ref, ssmn_ref, xbuf, act, sg, ss, ob, *, c, lv):
    l = pl.program_id(1)
    nl = pl.num_programs(1)

    @pl.when(l == 0)
    def _():
        xbuf[...] = jnp.zeros(xbuf.shape, f32)
        xbuf[5:8, :] = conv0_ref[0]
        sg[...] = gdn0_ref[0]
        for h in range(SSM_HEADS):
            ss[h] = ssm0_ref[0, h].T

    xbuf[8:8 + lv, :] = proj_ref[0, :, 0:CONV_CH].astype(f32)
    y = convb_ref[...]
    for i in range(CONV_W):
        y = y + convw_ref[i:i + 1, :] * xbuf[5 + i:5 + i + c, :]
    act[...] = _silu(y)
    tail = xbuf[5 + lv:8 + lv, :]
    xbuf[5:8, :] = tail

    if lv == c:
        sm = smt_ref[0, 0]
    else:
        lane = lax.broadcasted_iota(jnp.int32, (AB_SMALL, c), 1)
        sm = jnp.where(lane < lv, jnp.broadcast_to(smt_ref[0, 0], (AB_SMALL, c)), 0.0)
    valid = lax.broadcasted_iota(jnp.int32, (AB_SMALL, c), 1) < lv
    bias_col = pcol_ref[:, 0:1]
    alog_col = pcol_ref[:, 1:2]
    beta_t = jnp.where(valid, jax.nn.sigmoid(sm), 0.0)
    sp_t = jnp.where(valid, jax.nn.softplus(sm + bias_col), 0.0)
    la_t = -jnp.exp(alog_col) * sp_t
    rr = lax.broadcasted_iota(jnp.int32, (c, c), 0)
    qq = lax.broadcasted_iota(jnp.int32, (c, c), 1)
    triu = (rr <= qq).astype(f32)
    eye = (rr == qq).astype(f32)
    cum_t = _mm(la_t, triu, precision=HI)
    last_t = cum_t[:, c - 1:c]
    ecum_t = jnp.exp(cum_t)
    elc_t = jnp.exp(last_t - cum_t)
    elast_t = jnp.exp(last_t)
    g0, g1, s0, s1 = 0, GDN_HEADS, 2 * GDN_HEADS, AB_SMALL
    rows = jnp.concatenate([
        beta_t[g0:g1],
        cum_t[g1:s0],
        ecum_t[g1:s0],
        beta_t[g0:g1] * ecum_t[g1:s0],
        elc_t[g1:s0],
        cum_t[s0:s1],
        ecum_t[s0:s1],
        sp_t[s0:s1] * elc_t[s0:s1],
        jnp.zeros((128 - 88, c), f32),
    ], axis=0)
    cols = _nt(eye, rows, precision=HI)
    incl = rr >= qq
    strict = rr > qq

    def colstack(base, n):
        return jnp.stack([cols[:, base + h:base + h + 1] for h in range(n)])

    def rowstack(x, base, n):
        return jnp.stack([x[base + h:base + h + 1, :] for h in range(n)])

    hs = range(GDN_HEADS)
    q = jnp.stack([act[:, h * GDN_DK:(h + 1) * GDN_DK] for h in hs])
    k = jnp.stack([act[:, GDN_QK + h * GDN_DK:GDN_QK + (h + 1) * GDN_DK] for h in hs])
    v = jnp.stack([act[:, 2 * GDN_QK + h * GDN_DV:2 * GDN_QK + (h + 1) * GDN_DV] for h in hs])
    q = q * lax.rsqrt(jnp.sum(q * q, axis=-1, keepdims=True) + EPS) * (GDN_DK ** -0.5)
    k = k * lax.rsqrt(jnp.sum(k * k, axis=-1, keepdims=True) + EPS)
    beta_c, cum_c, ecum_c, becum_c, elc_c = (colstack(b, GDN_HEADS) for b in (0, 8, 16, 24, 32))
    dec = jnp.exp(jnp.where(incl, cum_c - rowstack(cum_t, g1, GDN_HEADS), -jnp.inf))
    kb = k.astype(bf16)
    qb = q.astype(bf16)
    lm = jnp.where(strict, beta_c * _bnt(kb, kb) * dec, 0.0)
    pinv = _unit_lower_inverse(lm, c)
    rhs = jnp.concatenate([v * beta_c, k * becum_c], axis=2)
    sol = _bmm3(pinv, rhs)
    u0 = sol[:, :, :GDN_DV]
    w = sol[:, :, GDN_DV:]
    qk = _bnt(qb, kb) * dec
    s_old = sg[...]
    sb = s_old.astype(bf16)
    u = u0 - _bmm(w.astype(bf16), sb)
    ub = u.astype(bf16)
    o = _bmm((q * ecum_c).astype(bf16), sb) + _bmm(qk.astype(bf16), ub)
    kd = (k * elc_c).astype(bf16)
    sg[...] = s_old * rowstack(elast_t, g1, GDN_HEADS) + _btn(kd, ub)
    oa = _rms(o) * gng_ref[...]
    for h in hs:
        gate = proj_ref[0, :, CONV_CH + h * GDN_DV:CONV_CH + (h + 1) * GDN_DV].astype(f32)
        o_ref[0, :, h * GDN_DV:(h + 1) * GDN_DV] = (oa[h][0:lv] * _silu(gate)).astype(bf16)

    rep = SSM_HEADS // SSM_G
    xs0 = 2 * GDN_QK + GDN_V
    bm0 = xs0 + SSM_INNER
    cm0 = bm0 + SSM_BC
    hs = range(SSM_HEADS)
    bmb = [act[:, bm0 + g * SSM_N:bm0 + (g + 1) * SSM_N].astype(bf16) for g in range(SSM_G)]
    cmb = [act[:, cm0 + g * SSM_N:cm0 + (g + 1) * SSM_N].astype(bf16) for g in range(SSM_G)]
    cbg = [_nt(cmb[g], bmb[g]) for g in range(SSM_G)]
    cb = jnp.stack([cbg[h // rep] for h in hs])
    bm16 = jnp.stack([bmb[h // rep] for h in hs])
    cm16 = jnp.stack([cmb[h // rep] for h in hs])
    xs = jnp.stack([act[:, xs0 + h * SSM_P:xs0 + (h + 1) * SSM_P] for h in hs])
    dsk = jnp.stack([dskip_ref[:, h * SSM_P:(h + 1) * SSM_P] for h in hs])
    cum_c, ecum_c, dtelc_c = (colstack(b, SSM_HEADS) for b in (40, 56, 72))
    m = cb * jnp.exp(jnp.where(incl, cum_c - rowstack(cum_t, s0, SSM_HEADS), -jnp.inf)) * rowstack(sp_t, s0, SSM_HEADS)
    s_old = ss[...]
    o = _bmm(m.astype(bf16), xs.astype(bf16)) + _bmm(cm16, s_old.astype(bf16)) * ecum_c + dsk * xs
    ss[...] = s_old * rowstack(elast_t, s0, SSM_HEADS) + _btn(bm16, (xs * dtelc_c).astype(bf16))
    for h in hs:
        ob[:, h * SSM_P:(h + 1) * SSM_P] = o[h]
    z = proj_ref[0, :, CONV_CH + GDN_V:CONV_CH + GDN_V + SSM_INNER].astype(f32)
    obv = _rms(ob[0:lv, :] * _silu(z)) * sng_ref[...]
    o_ref[0, :, GDN_V:GDN_V + SSM_INNER] = obv.astype(bf16)

    @pl.when(l == nl - 1)
    def _():
        convn_ref[0] = xbuf[5:8, :]
        gdnn_ref[0] = sg[...]
        ssmn_ref[0] = ss[...]


def mixer_ab(proj, smt, conv0, gdn0, ssm0, conv_w, conv_b, pcol, gdn_norm_g, dskip, ssm_norm_g, c):
    b, l, npad = proj.shape
    tl = min(c, l)
    nl = l // tl
    kern = functools.partial(_ab_kernel, c=c, lv=tl)
    full = lambda shape: pl.BlockSpec(shape, lambda i, j: (0,) * len(shape))
    return pl.pallas_call(
        kern,
        grid=(b, nl),
        in_specs=[
            pl.BlockSpec((1, tl, npad), lambda i, j: (i, j, 0)),
            pl.BlockSpec((1, 1, AB_SMALL, tl), lambda i, j: (i, j, 0, 0)),
            pl.BlockSpec((1, CONV_W - 1, CONV_CH), lambda i, j: (i, 0, 0)),
            pl.BlockSpec((1, GDN_HEADS, GDN_DK, GDN_DV), lambda i, j: (i, 0, 0, 0)),
            pl.BlockSpec((1, SSM_HEADS, SSM_P, SSM_N), lambda i, j: (i, 0, 0, 0)),
            full((CONV_W, CONV_CH)),
            full((1, CONV_CH)),
            full((AB_SMALL, 2)),
            full((1, GDN_DV)),
            full((1, SSM_INNER)),
            full((1, SSM_INNER)),
        ],
        out_specs=[
            pl.BlockSpec((1, tl, GDN_V + SSM_INNER), lambda i, j: (i, j, 0)),
            pl.BlockSpec((1, CONV_W - 1, CONV_CH), lambda i, j: (i, 0, 0)),
            pl.BlockSpec((1, GDN_HEADS, GDN_DK, GDN_DV), lambda i, j: (i, 0, 0, 0)),
            pl.BlockSpec((1, SSM_HEADS, SSM_N, SSM_P), lambda i, j: (i, 0, 0, 0)),
        ],
        out_shape=[
            jax.ShapeDtypeStruct((b, l, GDN_V + SSM_INNER), bf16),
            jax.ShapeDtypeStruct((b, CONV_W - 1, CONV_CH), f32),
            jax.ShapeDtypeStruct((b, GDN_HEADS, GDN_DK, GDN_DV), f32),
            jax.ShapeDtypeStruct((b, SSM_HEADS, SSM_N, SSM_P), f32),
        ],
        scratch_shapes=[
            pltpu.VMEM((8 + c, CONV_CH), f32),
            pltpu.VMEM((c, CONV_CH), f32),
            pltpu.VMEM((GDN_HEADS, GDN_DK, GDN_DV), f32),
            pltpu.VMEM((SSM_HEADS, SSM_N, SSM_P), f32),
            pltpu.VMEM((c, SSM_INNER), f32),
        ],
        compiler_params=_params(("arbitrary", "arbitrary")),
        name="mixer_ab",
    )(proj, smt, conv0, gdn0, ssm0, conv_w, conv_b, pcol, gdn_norm_g, dskip, ssm_norm_g)


def _ret_kernel(proj_ref, cos_ref, sin_ref, ret0_ref, ng_ref, o_ref, retn_ref, st, buf, *, c, lv):
    l = pl.program_id(1)
    nl = pl.num_programs(1)

    @pl.when(l == 0)
    def _():
        st[...] = ret0_ref[0]
        if lv < c:
            buf[...] = jnp.zeros(buf.shape, f32)

    if lv == c:
        src = proj_ref.at[0]
        cos = cos_ref[...]
        sin = sin_ref[...]
    else:
        buf[0:lv, :] = proj_ref[0].astype(f32)
        src = buf
        cos = jnp.broadcast_to(cos_ref[...], (c, RET_DK // 2))
        sin = jnp.broadcast_to(sin_ref[...], (c, RET_DK // 2))

    ri = lax.broadcasted_iota(jnp.int32, (c, c), 0)
    ci = lax.broadcasted_iota(jnp.int32, (c, c), 1)
    incl = ri >= ci
    cnt_r = jnp.minimum(ri + 1, lv).astype(f32)
    cnt_c = jnp.minimum(ci + 1, lv).astype(f32)
    cnt_col = jnp.minimum(lax.broadcasted_iota(jnp.int32, (c, 1), 0) + 1, lv).astype(f32)
    half = RET_DK // 2

    def rope(x):
        x1, x2 = x[:, :, :half], x[:, :, half:]
        return jnp.concatenate([x1 * cos - x2 * sin, x1 * sin + x2 * cos], axis=2)

    hs = range(RET_HEADS)
    lgs = [math.log(1.0 - 2.0 ** (-5.0 - h)) for h in hs]
    q = rope(jnp.stack([src[:, h * RET_DK:(h + 1) * RET_DK].astype(f32) for h in hs]))
    k = rope(jnp.stack([src[:, RET_QK + h * RET_DK:RET_QK + (h + 1) * RET_DK].astype(f32) for h in hs])) * (RET_DK ** -0.5)
    v = jnp.stack([src[:, 2 * RET_QK + h * RET_DV:2 * RET_QK + (h + 1) * RET_DV].astype(bf16) for h in hs])
    dec = jnp.stack([jnp.exp(jnp.where(incl, (cnt_r - cnt_c) * lg, -jnp.inf)) for lg in lgs])
    ecum = jnp.stack([jnp.exp(cnt_col * lg) for lg in lgs])
    elc = jnp.stack([jnp.exp((lv - cnt_col) * lg) for lg in lgs])
    elast = jnp.stack([jnp.full((1, 1), math.exp(lv * lg), f32) for lg in lgs])
    scores = _bnt(q.astype(bf16), k.astype(bf16)) * dec
    s_old = st[...]
    o = _bmm(scores.astype(bf16), v) + _bmm((q * ecum).astype(bf16), s_old.astype(bf16))
    st[...] = s_old * elast + _btn((k * elc).astype(bf16), v)
    mu = jnp.mean(o, axis=-1, keepdims=True)
    var = jnp.mean(jnp.square(o - mu), axis=-1, keepdims=True)
    o = (o - mu) * lax.rsqrt(var + EPS)
    for h in hs:
        gate = src[:, 2 * RET_QK + RET_V + h * RET_DV:2 * RET_QK + RET_V + (h + 1) * RET_DV].astype(f32)
        o_ref[0, :, h * RET_DV:(h + 1) * RET_DV] = (_silu(gate) * (o[h] * ng_ref[h:h + 1, :]))[0:lv].astype(bf16)

    @pl.when(l == nl - 1)
    def _():
        retn_ref[0] = st[...]


def mixer_ret(proj, cos, sin, ret0, norm_g, c):
    b, l, n = proj.shape
    tl = min(c, l)
    nl = l // tl
    kern = functools.partial(_ret_kernel, c=c, lv=tl)
    return pl.pallas_call(
        kern,
        grid=(b, nl),
        in_specs=[
            pl.BlockSpec((1, tl, n), lambda i, j: (i, j, 0)),
            pl.BlockSpec((tl, RET_DK // 2), lambda i, j: (j, 0)),
            pl.BlockSpec((tl, RET_DK // 2), lambda i, j: (j, 0)),
            pl.BlockSpec((1, RET_HEADS, RET_DK, RET_DV), lambda i, j: (i, 0, 0, 0)),
            pl.BlockSpec((RET_HEADS, RET_DV), lambda i, j: (0, 0)),
        ],
        out_specs=[
            pl.BlockSpec((1, tl, RET_V), lambda i, j: (i, j, 0)),
            pl.BlockSpec((1, RET_HEADS, RET_DK, RET_DV), lambda i, j: (i, 0, 0, 0)),
        ],
        out_shape=[
            jax.ShapeDtypeStruct((b, l, RET_V), bf16),
            jax.ShapeDtypeStruct((b, RET_HEADS, RET_DK, RET_DV), f32),
        ],
        scratch_shapes=[
            pltpu.VMEM((RET_HEADS, RET_DK, RET_DV), f32),
            pltpu.VMEM((c, n), f32),
        ],
        compiler_params=_params(("arbitrary", "arbitrary")),
        name="mixer_ret",
    )(proj, cos, sin, ret0, norm_g)


def _top16(s, exact):
    g, n, t = s.shape
    rows = lax.broadcasted_iota(jnp.int32, (g, n, t), 1)
    krow = lax.broadcasted_iota(jnp.int32, (g, PEER_TOPK, t), 1)
    rank = jnp.full((g, n, t), float(PEER_TOPK), f32)
    sv = jnp.zeros((g, PEER_TOPK, t), f32)
    work = s
    for k in range(PEER_TOPK):
        m = jnp.max(work, axis=1, keepdims=True)
        if exact:
            idx = jnp.min(jnp.where(work == m, rows, n), axis=1, keepdims=True)
            sel = rows == idx
        else:
            sel = work == m
        rank = jnp.where(sel, float(k), rank)
        work = jnp.where(sel, -jnp.inf, work)
        sv = jnp.where(krow == k, m, sv)
    return sv, rank


def _miscount(rank):
    cnt = jnp.sum(jnp.where(rank < float(PEER_TOPK), 1.0, 0.0), axis=1, keepdims=True)
    return jnp.max(jnp.abs(cnt - float(PEER_TOPK)))


def _topk_kernel(x_ref, g_ref, sc_ref, sh_ref, wqt_ref, keys_ref, a1_ref, a2_ref, a1t_ref,
                 hmt_ref, r2_ref, e2_ref, lc_ref, g1_ref, sv_scr, rk_scr, rc_scr, cd_scr):
    y = _rms(x_ref[...]) * g_ref[...]
    hm = y * (1.0 + sc_ref[0]) + sh_ref[0]
    hmt = hm.T.astype(bf16)
    hmt_ref[...] = hmt
    qt = _mm(wqt_ref[...], hmt)
    tt = qt.shape[1]
    half = PEER_DQ // 2
    hs = range(PEER_HEADS)
    s_all = jnp.stack(
        [_mm(keys_ref[0], qt[h * PEER_DQ:h * PEER_DQ + half, :].astype(bf16)) for h in hs]
        + [_mm(keys_ref[1], qt[h * PEER_DQ + half:(h + 1) * PEER_DQ, :].astype(bf16)) for h in hs])
    keys_rows = slice(0, PEER_NKEYS)
    pad = jnp.zeros((PEER_HEADS, PEER_NKP - PEER_NKEYS, 128), f32)
    crow = lax.broadcasted_iota(jnp.int32, (PEER_HEADS, _NCAND_PAD, 128), 1)

    def candidates(sv):
        cand = jnp.stack([_mm(a1_ref[...], sv[h], precision=HI) + _mm(a2_ref[...], sv[PEER_HEADS + h], precision=HI)
                          for h in hs])
        return jnp.where(crow < _NCAND, cand, -jnp.inf)

    for lg in range(tt // 128):
        ls = slice(lg * 128, (lg + 1) * 128)
        s_blk = s_all[:, :, ls]

        def stage1(rows, exact):
            sv, rank = _top16(s_blk[rows], exact)
            sv_scr[rows, :, ls] = sv
            rk_scr[rows, :, ls] = rank
            return _miscount(rank)

        def stage2(exact):
            cand = candidates(sv_scr[:, :, ls])
            _, rankc = _top16(cand, exact)
            cd_scr[:, :, ls] = cand
            rc_scr[:, :, ls] = rankc
            return _miscount(rankc)

        for rows in (slice(0, PEER_HEADS), slice(PEER_HEADS, 2 * PEER_HEADS)):
            @pl.when(stage1(rows, False) > 0.0)
            def _():
                stage1(rows, True)

        @pl.when(stage2(False) > 0.0)
        def _():
            stage2(True)

        sv = sv_scr[:, :, ls]
        sv1, sv2 = sv[:PEER_HEADS], sv[PEER_HEADS:]
        rank1 = rk_scr[0:PEER_HEADS, :, ls]
        sel_all = jnp.where(rc_scr[:, :, ls] < float(PEER_TOPK), 1.0, 0.0)
        cand = cd_scr[:, :, ls]
        z = jnp.sum(sel_all * jnp.exp(jnp.where(crow < _NCAND, cand - cand[:, 0:1, :], 0.0)), axis=1, keepdims=True)
        selb = sel_all.astype(bf16)
        cnt = jnp.stack([_mm(a1t_ref[...], selb[h]) for h in hs])
        lc = jnp.zeros((PEER_HEADS, PEER_NKEYS, 128), f32)
        for k1 in range(PEER_TOPK):
            lc = jnp.where(rank1 == float(k1), cnt[:, k1:k1 + 1, :], lc)
        r2_ref[:, keys_rows, ls] = rk_scr[PEER_HEADS:2 * PEER_HEADS, :, ls]
        e2_ref[:, keys_rows, ls] = jnp.exp(s_blk[PEER_HEADS:] - sv2[:, 0:1, :])
        lc_ref[:, keys_rows, ls] = lc
        g1_ref[:, keys_rows, ls] = jnp.exp(s_blk[:PEER_HEADS] - sv1[:, 0:1, :]) * (0.5 / z)
        for ref in (r2_ref, e2_ref, lc_ref, g1_ref):
            ref[:, PEER_NKEYS:, ls] = pad


def peer_topk(x, gamma, sc, sh, mod_map, wq_t, keys, tt):
    t = x.shape[0]
    r = sc.shape[1]
    a1 = jnp.zeros((_NCAND_PAD, PEER_TOPK), f32).at[jnp.arange(_NCAND), jnp.array([a for a, _ in _CAND])].set(1.0)
    a2 = jnp.zeros((_NCAND_PAD, PEER_TOPK), f32).at[jnp.arange(_NCAND), jnp.array([b for _, b in _CAND])].set(1.0)
    full = lambda shape: pl.BlockSpec(shape, lambda i: (0,) * len(shape))
    tok = pl.BlockSpec((PEER_HEADS, PEER_NKP, tt), lambda i: (0, 0, i))
    tok_shape = jax.ShapeDtypeStruct((PEER_HEADS, PEER_NKP, t), f32)
    return pl.pallas_call(
        _topk_kernel,
        grid=(t // tt,),
        in_specs=[
            pl.BlockSpec((tt, D_MODEL), lambda i: (i, 0)),
            full((1, D_MODEL)),
            pl.BlockSpec((1, r, D_MODEL), lambda i: mod_map(i)),
            pl.BlockSpec((1, r, D_MODEL), lambda i: mod_map(i)),
            full((PEER_HEADS * PEER_DQ, D_MODEL)),
            full((2, PEER_NKEYS, PEER_DQ // 2)),
            full((_NCAND_PAD, PEER_TOPK)),
            full((_NCAND_PAD, PEER_TOPK)),
            full((PEER_TOPK, _NCAND_PAD)),
        ],
        out_specs=[pl.BlockSpec((D_MODEL, tt), lambda i: (0, i)), tok, tok, tok, tok],
        out_shape=[jax.ShapeDtypeStruct((D_MODEL, t), bf16), tok_shape, tok_shape, tok_shape, tok_shape],
        scratch_shapes=[
            pltpu.VMEM((2 * PEER_HEADS, PEER_TOPK, tt), f32),
            pltpu.VMEM((2 * PEER_HEADS, PEER_NKEYS, tt), f32),
            pltpu.VMEM((PEER_HEADS, _NCAND_PAD, tt), f32),
            pltpu.VMEM((PEER_HEADS, _NCAND_PAD, tt), f32),
        ],
        compiler_params=_params(("arbitrary",)),
        name="peer_topk",
    )(x, gamma.reshape(1, D_MODEL), sc, sh, wq_t, keys, a1, a2, a1.T.astype(bf16))


def _peer_kernel(hmt_ref, u_ref, vt_ref, r2_ref, e2_ref, lc_ref, g1_ref, x_ref, g_ref, fg_ref, o_ref,
                 yt, act_a, act_b, ht_a, ht_b, *, nsub, final):
    j = pl.program_id(1)
    nj = pl.num_programs(1)
    na = PEER_SB // PEER_NKEYS
    tt = hmt_ref.shape[1]

    @pl.when(j == 0)
    def _():
        yt[...] = jnp.zeros(yt.shape, f32)

    def mm1(s, ht):
        ht[...] = _mm(u_ref[s], hmt_ref[...])

    def gate_act(s, ht, act):
        for a in range(na):
            n1 = (j * nsub + s) * na + a
            lrows = [lc_ref[h, pl.ds(n1, 1), :] for h in range(PEER_HEADS)]
            grows = [g1_ref[h, pl.ds(n1, 1), :] for h in range(PEER_HEADS)]
            for lg in range(tt // 128):
                ls = slice(lg * 128, (lg + 1) * 128)
                w = None
                for h in range(PEER_HEADS):
                    wh = jnp.where(r2_ref[h, 0:PEER_NKEYS, ls] < lrows[h][:, ls], e2_ref[h, 0:PEER_NKEYS, ls] * grows[h][:, ls], 0.0)
                    w = wh if w is None else w + wh
                hb = ht[a * PEER_NKEYS:(a + 1) * PEER_NKEYS, ls]
                act[a * PEER_NKEYS:(a + 1) * PEER_NKEYS, ls] = (hb * (1.0 + lax.erf(hb * (2.0 ** -0.5))) * w).astype(bf16)

    def mm2(s, act):
        yt[...] += _mm(vt_ref[s], act[...])

    act_b[...] = jnp.zeros(act_b.shape, bf16)
    mm1(0, ht_a)

    def pair(s, last):
        mm1(s + 1, ht_b)
        gate_act(s, ht_a, act_a)
        mm2(jnp.maximum(s - 1, 0), act_b)
        if not last:
            mm1(s + 2, ht_a)
        gate_act(s + 1, ht_b, act_b)
        mm2(s, act_a)

    def body(i, carry):
        pair(2 * i, False)
        return carry

    lax.fori_loop(0, nsub // 2 - 1, body, 0)
    pair(nsub - 2, True)
    mm2(nsub - 1, act_b)

    @pl.when(j == nj - 1)
    def _():
        xn = x_ref[...] + g_ref[0] * yt[...].T
        if final:
            xn = _rms(xn) * fg_ref[...]
        o_ref[...] = xn


def peer_dense(hmt, u4, vt4, layer, r2, e2, lc, g1, x, gate, mod_map, final_g, tt, eb):
    t = x.shape[0]
    r = gate.shape[1]
    nsub = eb // PEER_SB
    final = final_g is not None
    fg = (final_g if final else jnp.ones((D_MODEL,), f32)).reshape(1, D_MODEL)
    tok = pl.BlockSpec((PEER_HEADS, PEER_NKP, tt), lambda i, j: (0, 0, i))
    return pl.pallas_call(
        functools.partial(_peer_kernel, nsub=nsub, final=final),
        grid=(t // tt, N_EXPERTS // eb),
        in_specs=[
            pl.BlockSpec((D_MODEL, tt), lambda i, j: (0, i)),
            pl.BlockSpec((None, nsub, PEER_SB, D_MODEL), lambda i, j: (layer, j, 0, 0)),
            pl.BlockSpec((None, nsub, D_MODEL, PEER_SB), lambda i, j: (layer, j, 0, 0)),
            tok, tok, tok, tok,
            pl.BlockSpec((tt, D_MODEL), lambda i, j: (i, 0)),
            pl.BlockSpec((1, r, D_MODEL), lambda i, j: mod_map(i)),
            pl.BlockSpec((1, D_MODEL), lambda i, j: (0, 0)),
        ],
        out_specs=pl.BlockSpec((tt, D_MODEL), lambda i, j: (i, 0)),
        out_shape=jax.ShapeDtypeStruct((t, D_MODEL), f32),
        scratch_shapes=[
            pltpu.VMEM((D_MODEL, tt), f32),
            pltpu.VMEM((PEER_SB, tt), bf16), pltpu.VMEM((PEER_SB, tt), bf16),
            pltpu.VMEM((PEER_SB, tt), f32), pltpu.VMEM((PEER_SB, tt), f32),
        ],
        compiler_params=_params(("arbitrary", "arbitrary")),
        name="peer_dense",
    )(hmt, u4, vt4, r2, e2, lc, g1, x, gate, fg)


def _prepare(ab_w_in, ab_conv_w, ab_conv_b, gdn_a_log, gdn_dt_bias, gdn_norm_g, ssm_a_log, ssm_dt_bias, ssm_d,
             ssm_norm_g, ab_w_out, ret_w_in, ret_norm_g, ret_w_out, peer_w_q, peer_keys, peer_u, peer_v):
    zeros8 = jnp.zeros((GDN_HEADS,), f32)
    nsb = N_EXPERTS // PEER_SB
    return dict(
        ab_w_main=ab_w_in[0][:, :AB_MAIN].astype(bf16),
        ab_w_small_t=ab_w_in[0][:, AB_MAIN:].T.astype(bf16),
        conv_w=ab_conv_w[0],
        conv_b=ab_conv_b[0].reshape(1, CONV_CH),
        pcol=jnp.stack([jnp.concatenate([zeros8, gdn_dt_bias[0], ssm_dt_bias[0]]),
                        jnp.concatenate([zeros8, gdn_a_log[0], ssm_a_log[0]])], axis=1),
        gdn_norm_g=gdn_norm_g[0].reshape(1, GDN_DV),
        dskip=jnp.repeat(ssm_d[0], SSM_P).reshape(1, SSM_INNER),
        ssm_norm_g=ssm_norm_g[0].reshape(1, SSM_INNER),
        ab_w_out=ab_w_out[0].astype(bf16),
        ret_w_in=ret_w_in[0].astype(bf16),
        ret_norm_g=ret_norm_g[0],
        ret_w_out=ret_w_out[0].astype(bf16),
        wq_t=[peer_w_q[i].T.astype(bf16) for i in range(DEPTH)],
        keys=[peer_keys[i].astype(bf16) for i in range(DEPTH)],
        u=peer_u.astype(bf16).reshape(DEPTH, nsb, PEER_SB, D_MODEL),
        vt=peer_v.astype(bf16).reshape(DEPTH, nsb, PEER_SB, D_MODEL).transpose(0, 1, 3, 2),
    )


def _rope_tables(pos0, length):
    inv = ROPE_BASE ** (-jnp.arange(0, RET_DK, 2, dtype=f32) / RET_DK)
    ang = (pos0 + jnp.arange(length, dtype=f32))[:, None] * inv[None, :]
    return jnp.cos(ang), jnp.sin(ang)


def _trunk(x, mods, pos0, conv0, gdn0, ssm0, ret0, p, norm1_g, norm2_g, final_g):
    b, l, _ = x.shape
    t = b * l
    c = CHUNK if l >= CHUNK else 8
    xt = x.reshape(t, D_MODEL)
    if l > 1:
        tm = min(512, l)
        tp = min(256, l)
        tt = min(256, l)
        mod_arr = lambda m: m.reshape(b, 1, D_MODEL)
        mod_map = lambda rows: (lambda i: (i * rows // l, 0, 0))
    else:
        tm = t
        tp = t
        tt = t
        mod_arr = lambda m: m.reshape(1, t, D_MODEL)
        mod_map = lambda rows: (lambda i: (0, i, 0))
    convs = gdns = ssms = rets = None
    for layer in range(DEPTH):
        sh1, sc1, g1, sh2, sc2, g2 = (mod_arr(m) for m in jnp.split(mods[layer], 6, axis=-1))
        if layer == 0:
            proj, smt = norm_proj(xt, norm1_g[layer], sc1, sh1, mod_map(tp), p["ab_w_main"], p["ab_w_small_t"],
                                  tp, AB_MAIN)
            tl = min(c, l)
            smt = smt.reshape(AB_SMALL, b, l // tl, tl).transpose(1, 2, 0, 3)
            o, convs, gdns, ssms = mixer_ab(proj.reshape(b, l, -1), smt, conv0, gdn0, jnp.swapaxes(ssm0, -1, -2),
                                            p["conv_w"], p["conv_b"],
                                            p["pcol"], p["gdn_norm_g"], p["dskip"], p["ssm_norm_g"], c)
            xt = out_proj(o.reshape(t, -1), p["ab_w_out"], xt, g1, mod_map(tm), tm)
        else:
            proj = norm_proj(xt, norm1_g[layer], sc1, sh1, mod_map(tp), p["ret_w_in"], None, tp, RET_IN)
            cos, sin = _rope_tables(pos0, l)
            o, rets = mixer_ret(proj.reshape(b, l, -1), cos, sin, ret0, p["ret_norm_g"], c)
            xt = out_proj(o.reshape(t, -1), p["ret_w_out"], xt, g1, mod_map(tm), tm)
        hmt, r2, e2, lc, gg = peer_topk(xt, norm2_g[layer], sc2, sh2, mod_map(tt), p["wq_t"][layer], p["keys"][layer], tt)
        xt = peer_dense(hmt, p["u"], p["vt"], layer, r2, e2, lc, gg, xt, g2, mod_map(tt),
                        final_g if layer == DEPTH - 1 else None, tt, PEER_EB)
    return xt.reshape(b, l, D_MODEL), convs[None], gdns[None], ssms[None], rets[None]


def kernel(x_prompt, x_sample, c_prompt, c_sample, state_conv, state_gdn, state_ssm, state_ret, ada_w, ada_b,
           norm1_g, norm2_g, ab_w_in, ab_conv_w, ab_conv_b, gdn_a_log, gdn_dt_bias, gdn_norm_g, ssm_a_log,
           ssm_dt_bias, ssm_d, ssm_norm_g, ab_w_out, ret_w_in, ret_norm_g, ret_w_out, peer_w_q, peer_keys,
           peer_u, peer_v, final_g):
    p = _prepare(ab_w_in, ab_conv_w, ab_conv_b, gdn_a_log, gdn_dt_bias, gdn_norm_g, ssm_a_log, ssm_dt_bias, ssm_d,
                 ssm_norm_g, ab_w_out, ret_w_in, ret_norm_g, ret_w_out, peer_w_q, peer_keys, peer_u, peer_v)
    nb = x_prompt.shape[0]
    mods = ada_mod(jnp.concatenate([c_prompt, c_sample], axis=0), ada_w, ada_b)
    zeros = lambda s: jnp.zeros((nb,) + s.shape[2:], s.dtype)
    y_p, p_conv, p_gdn, p_ssm, p_ret = _trunk(
        x_prompt, mods[:, :nb], 0, zeros(state_conv), zeros(state_gdn), zeros(state_ssm), zeros(state_ret),
        p, norm1_g, norm2_g, final_g)
    y_s, s_conv, s_gdn, s_ssm, s_ret = _trunk(
        x_sample, mods[:, nb:], PAST_LEN, state_conv[0], state_gdn[0], state_ssm[0], state_ret[0],
        p, norm1_g, norm2_g, final_g)
    return (y_p, y_s, p_conv, p_gdn, p_ssm, p_ret, s_conv, s_gdn, s_ssm, s_ret)
```
